```python
import math
import jax, jax.numpy as jnp
from jax import lax
import numpy as np

D_MODEL = 1024
BATCH = 8
SEQ = 2048
DEPTH = 4
DEC_BATCH = 128
DEC_SEQ = 4
PAST_LEN = 16384
PAGE_SIZE = 128

N_GDN_HEADS = 4
GDN_HEAD_DIM = 128
GDN_WIDTH = N_GDN_HEADS * GDN_HEAD_DIM
GDN_CONV_WIDTH = 4
GDN_CHUNK = 64
CONF_WIDTH = D_MODEL - GDN_WIDTH
CONF_GROUPS = 4
CONF_KERNEL = 31
D_FF_DENSE = 2816
N_EXPERTS = 8
TOP_K = 2
D_FF_EXPERT = 3584
MOE_BLOCK = 128
N_DENSE_LAYERS = (DEPTH + 1) // 2
N_MOE_LAYERS = DEPTH // 2
N_MOD = 6
RMS_EPS = 1e-6
LN_EPS = 1e-5
QKV_WIDTH = 3 * GDN_WIDTH
IN_WIDTH = QKV_WIDTH + GDN_WIDTH + 2 * N_GDN_HEADS + 2 * CONF_WIDTH

kernel_name = "hybrid_gdn_conformer_adaln_moe_step"


def rms_norm(x, g):
    xf = x.astype(jnp.float32)
    y = xf * lax.rsqrt(jnp.mean(xf * xf, axis=-1, keepdims=True) + RMS_EPS)
    return (y * g.astype(jnp.float32)).astype(x.dtype)


def l2_normalize(x):
    return x * lax.rsqrt(jnp.sum(x * x, axis=-1, keepdims=True) + 1e-6)


def causal_depthwise_conv(x, buf, w):
    width = w.shape[0]
    xp = jnp.concatenate([buf.astype(x.dtype), x], axis=1)
    y = lax.conv_general_dilated(xp, w[:, None, :].astype(x.dtype), window_strides=(1,), padding="VALID",
                                 dimension_numbers=("NWC", "WIO", "NWC"), feature_group_count=x.shape[-1])
    return y, xp[:, xp.shape[1] - (width - 1):]


def gated_delta_rule(q, k, v, g, beta, s0):
    bsz, seq, nh, dk = q.shape
    dv = v.shape[-1]
    c = math.gcd(seq, GDN_CHUNK)
    n = seq // c

    def chunks(t):
        t = t.reshape((bsz, n, c, nh) + t.shape[3:])
        return jnp.moveaxis(t, 3, 1)

    q, k, v, g, beta = chunks(q), chunks(k), chunks(v), chunks(g), chunks(beta)
    gc = jnp.cumsum(g, axis=-1)
    incl = jnp.tril(jnp.ones((c, c), dtype=bool))
    strict = jnp.tril(jnp.ones((c, c), dtype=bool), -1)
    decay = jnp.exp(jnp.where(incl, gc[..., :, None] - gc[..., None, :], -jnp.inf))
    kb = k * beta[..., None]
    a_mat = jnp.where(strict, jnp.einsum("bhnid,bhnjd->bhnij", kb, k) * decay, 0.0)
    eye = jnp.eye(c, dtype=q.dtype)
    t_inv = lax.linalg.triangular_solve(eye + a_mat, jnp.broadcast_to(eye, a_mat.shape),
                                        left_side=True, lower=True, unit_diagonal=True)
    u = jnp.einsum("bhnij,bhnjd->bhnid", t_inv, v * beta[..., None])
    w = jnp.einsum("bhnij,bhnjd->bhnid", t_inv, kb * jnp.exp(gc)[..., None])
    qk = jnp.where(incl, jnp.einsum("bhnid,bhnjd->bhnij", q, k) * decay, 0.0)
    q_dec = q * jnp.exp(gc)[..., None]
    k_dec = k * jnp.exp(gc[..., -1:] - gc)[..., None]
    g_last = jnp.exp(gc[..., -1])

    def step(s, inp):
        w_n, u_n, qd_n, kd_n, qk_n, gl_n = inp
        v_new = u_n - jnp.einsum("bhik,bhkv->bhiv", w_n, s)
        o_n = jnp.einsum("bhik,bhkv->bhiv", qd_n, s) + jnp.einsum("bhij,bhjv->bhiv", qk_n, v_new)
        s = s * gl_n[..., None, None] + jnp.einsum("bhik,bhiv->bhkv", kd_n, v_new)
        return s, o_n

    xs = tuple(jnp.moveaxis(t, 2, 0) for t in (w, u, q_dec, k_dec, qk, g_last))
    s_final, o = lax.scan(step, s0, xs)
    o = jnp.moveaxis(o, 0, 2).reshape(bsz, nh, seq, dv)
    return jnp.moveaxis(o, 1, 2), s_final


def gdn_mixer(qkv_in, z, b_raw, a_raw, conv_buf, s0, w_conv, a_log, dt_bias, g_out):
    bsz, seq, _ = qkv_in.shape
    f32 = jnp.float32
    qkv, new_buf = causal_depthwise_conv(qkv_in, conv_buf, w_conv)
    qkv = jax.nn.silu(qkv.astype(f32)).reshape(bsz, seq, 3, N_GDN_HEADS, GDN_HEAD_DIM)
    q = l2_normalize(qkv[:, :, 0]) * (GDN_HEAD_DIM ** -0.5)
    k = l2_normalize(qkv[:, :, 1])
    v = qkv[:, :, 2]
    beta = jax.nn.sigmoid(b_raw.astype(f32))
    g = -jnp.exp(a_log.astype(f32)) * jax.nn.softplus(a_raw.astype(f32) + dt_bias.astype(f32))
    o, s = gated_delta_rule(q, k, v, g, beta, s0.astype(f32))
    o = rms_norm(o, g_out) * jax.nn.silu(z.astype(f32).reshape(bsz, seq, N_GDN_HEADS, GDN_HEAD_DIM))
    return o.reshape(bsz, seq, GDN_WIDTH).astype(qkv_in.dtype), new_buf, s.astype(s0.dtype)


def conformer_conv(u, gate, buf, w_dw, b_dw, g_ln, b_ln):
    bsz, seq, _ = u.shape
    h = u * jax.nn.sigmoid(gate)
    y, new_buf = causal_depthwise_conv(h, buf, w_dw)
    y = (y + b_dw).astype(jnp.float32).reshape(bsz, seq, CONF_GROUPS, CONF_WIDTH // CONF_GROUPS)
    mu = jnp.mean(y, axis=-1, keepdims=True)
    var = jnp.mean(jnp.square(y - mu), axis=-1, keepdims=True)
    y = ((y - mu) * lax.rsqrt(var + LN_EPS)).reshape(bsz, seq, CONF_WIDTH) * g_ln.astype(jnp.float32) + b_ln.astype(jnp.float32)
    return jax.nn.silu(y).astype(u.dtype), new_buf


def swiglu(x, wg, wu, wd):
    return (jax.nn.silu(x @ wg) * (x @ wu)) @ wd


def moe_swiglu(h, w_r, b_r, wg, wu, wd):
    n_tok, dm = h.shape
    logits = h.astype(jnp.float32) @ w_r.astype(jnp.float32) + b_r.astype(jnp.float32)
    probs = jax.nn.softmax(logits, axis=-1)
    top_p, top_e = lax.top_k(probs, TOP_K)
    top_p = top_p / jnp.sum(top_p, axis=-1, keepdims=True)
    n_asg = n_tok * TOP_K
    flat_e = top_e.reshape(n_asg)
    flat_p = top_p.reshape(n_asg)
    flat_tok = jnp.arange(n_asg, dtype=jnp.int32) // TOP_K
    order = jnp.argsort(flat_e)
    se, stok, sp = flat_e[order], flat_tok[order], flat_p[order]
    counts = jnp.bincount(flat_e, length=N_EXPERTS)
    padded = (counts + MOE_BLOCK - 1) // MOE_BLOCK * MOE_BLOCK
    pad_end = jnp.cumsum(padded)
    pad_start = pad_end - padded
    start = jnp.cumsum(counts) - counts
    dest = pad_start[se] + jnp.arange(n_asg) - start[se]
    n_blocks = (n_asg + MOE_BLOCK - 1) // MOE_BLOCK + N_EXPERTS
    n_slots = n_blocks * MOE_BLOCK
    slot_tok = jnp.full((n_slots,), n_tok, jnp.int32).at[dest].set(stok)
    block_e = jnp.minimum(jnp.searchsorted(pad_end, jnp.arange(n_blocks) * MOE_BLOCK, side="right"), N_EXPERTS - 1)
    h_pad = jnp.concatenate([h, jnp.zeros((1, dm), h.dtype)], axis=0)
    xb = h_pad[slot_tok].reshape(n_blocks, MOE_BLOCK, dm)

    def expert_block(args):
        xblk, e = args
        return swiglu(xblk, wg[e], wu[e], wd[e])

    yb = lax.map(expert_block, (xb, block_e)).reshape(n_slots, dm)
    contrib = yb[dest] * sp[:, None].astype(h.dtype)
    return jax.ops.segment_sum(contrib, stok, num_segments=n_tok)


def decoder_trunk(x, c, gdn_conv0, gdn_state0, conf_conv0, params):
    (w_ada, b_ada, g_norm1, g_norm2, w_in, w_gdn_conv, a_log, dt_bias, g_gdn_out,
     w_conf_dw, b_conf_dw, g_conf_ln, b_conf_ln, w_out,
     w_ff_gate, w_ff_up, w_ff_down, w_router, b_router, w_exp_gate, w_exp_up, w_exp_down, g_final) = params
    bsz, seq, _ = x.shape
    o1 = QKV_WIDTH
    o2 = o1 + GDN_WIDTH
    o3 = o2 + N_GDN_HEADS
    o4 = o3 + N_GDN_HEADS
    o5 = o4 + CONF_WIDTH
    new_s, new_conv, new_conf = [], [], []
    for layer in range(DEPTH):
        mod = (jax.nn.silu(c) @ w_ada[layer] + b_ada[layer])[:, None, :]
        shift1, scale1, gate1, shift2, scale2, gate2 = jnp.split(mod, N_MOD, axis=-1)
        h = rms_norm(x, g_norm1[layer]) * (1 + scale1) + shift1
        proj = h @ w_in[layer]
        qkv_in, z, b_raw, a_raw, glu_u, glu_g = jnp.split(proj, [o1, o2, o3, o4, o5], axis=-1)
        o_a, buf_a, s_a = gdn_mixer(qkv_in, z, b_raw, a_raw, gdn_conv0[layer], gdn_state0[layer],
                                    w_gdn_conv[layer], a_log[layer], dt_bias[layer], g_gdn_out[layer])
        o_b, buf_b = conformer_conv(glu_u, glu_g, conf_conv0[layer], w_conf_dw[layer], b_conf_dw[layer],
                                    g_conf_ln[layer], b_conf_ln[layer])
        x = x + gate1 * (jnp.concatenate([o_a, o_b], axis=-1) @ w_out[layer])
        h = rms_norm(x, g_norm2[layer]) * (1 + scale2) + shift2
        idx = layer // 2
        if layer % 2 == 0:
            f = swiglu(h, w_ff_gate[idx], w_ff_up[idx], w_ff_down[idx])
        else:
            f = moe_swiglu(h.reshape(bsz * seq, D_MODEL), w_router[idx], b_router[idx],
                           w_exp_gate[idx], w_exp_up[idx], w_exp_down[idx]).reshape(bsz, seq, D_MODEL)
        x = x + gate2 * f
        new_s.append(s_a)
        new_conv.append(buf_a)
        new_conf.append(buf_b)
    return rms_norm(x, g_final), jnp.stack(new_s), jnp.stack(new_conv), jnp.stack(new_conf)


def setup_inputs(seed: int = 0) -> dict:
    key = jax.random.key(seed)
    ks = jax.random.split(key, 32)
    f32 = jnp.float32
    D = D_MODEL

    def nrm(i, shape, scale):
        return jax.random.normal(ks[i], shape, f32) * scale

    dt = jnp.exp(jax.random.uniform(ks[14], (DEPTH, N_GDN_HEADS), f32, math.log(1e-3), math.log(1e-1)))
    return {
        "x_prompt": nrm(0, (BATCH, SEQ, D), 1.0),
        "x_sample": nrm(1, (DEC_BATCH, DEC_SEQ, D), 1.0),
        "c_prompt": nrm(2, (BATCH, D), 1.0),
        "c_sample": nrm(3, (DEC_BATCH, D), 1.0),
        "state_gdn": nrm(4, (DEPTH, DEC_BATCH, N_GDN_HEADS, GDN_HEAD_DIM, GDN_HEAD_DIM), 0.1),
        "state_gdn_conv": nrm(5, (DEPTH, DEC_BATCH, GDN_CONV_WIDTH - 1, QKV_WIDTH), 1.0),
        "state_conf_conv": nrm(6, (DEPTH, DEC_BATCH, CONF_KERNEL - 1, CONF_WIDTH), 0.5),
        "w_ada": nrm(7, (DEPTH, D, N_MOD * D), 0.5 * D ** -0.5),
        "b_ada": nrm(8, (DEPTH, N_MOD * D), 0.01),
        "g_norm1": 1.0 + nrm(9, (DEPTH, D), 0.01),
        "g_norm2": 1.0 + nrm(10, (DEPTH, D), 0.01),
        "w_in": nrm(11, (DEPTH, D, IN_WIDTH), D ** -0.5),
        "w_gdn_conv": nrm(12, (DEPTH, GDN_CONV_WIDTH, QKV_WIDTH), GDN_CONV_WIDTH ** -0.5),
        "a_log": jnp.log(jax.random.uniform(ks[13], (DEPTH, N_GDN_HEADS), f32, 1.0, 16.0)),
        "dt_bias": dt + jnp.log(-jnp.expm1(-dt)),
        "g_gdn_out": 1.0 + nrm(15, (DEPTH, GDN_HEAD_DIM), 0.01),
        "w_conf_dw": nrm(16, (DEPTH, CONF_KERNEL, CONF_WIDTH), CONF_KERNEL ** -0.5),
        "b_conf_dw": nrm(17, (DEPTH, CONF_WIDTH), 0.01),
        "g_conf_ln": 1.0 + nrm(18, (DEPTH, CONF_WIDTH), 0.01),
        "b_conf_ln": nrm(19, (DEPTH, CONF_WIDTH), 0.01),
        "w_out": nrm(20, (DEPTH, D, D), D ** -0.5),
        "w_ff_gate": nrm(21, (N_DENSE_LAYERS, D, D_FF_DENSE), D ** -0.5),
        "w_ff_up": nrm(22, (N_DENSE_LAYERS, D, D_FF_DENSE), D ** -0.5),
        "w_ff_down": nrm(23, (N_DENSE_LAYERS, D_FF_DENSE, D), D_FF_DENSE ** -0.5),
        "w_router": nrm(24, (N_MOE_LAYERS, D, N_EXPERTS), D ** -0.5),
        "b_router": nrm(25, (N_MOE_LAYERS, N_EXPERTS), 0.01),
        "w_exp_gate": nrm(26, (N_MOE_LAYERS, N_EXPERTS, D, D_FF_EXPERT), D ** -0.5),
        "w_exp_up": nrm(27, (N_MOE_LAYERS, N_EXPERTS, D, D_FF_EXPERT), D ** -0.5),
        "w_exp_down": nrm(28, (N_MOE_LAYERS, N_EXPERTS, D_FF_EXPERT, D), D_FF_EXPERT ** -0.5),
        "g_final": 1.0 + nrm(29, (D,), 0.01),
    }


def reference(x_prompt, x_sample, c_prompt, c_sample, state_gdn, state_gdn_conv, state_conf_conv,
              w_ada, b_ada, g_norm1, g_norm2, w_in, w_gdn_conv, a_log, dt_bias, g_gdn_out,
              w_conf_dw, b_conf_dw, g_conf_ln, b_conf_ln, w_out,
              w_ff_gate, w_ff_up, w_ff_down, w_router, b_router, w_exp_gate, w_exp_up, w_exp_down, g_final):
    params = (w_ada, b_ada, g_norm1, g_norm2, w_in, w_gdn_conv, a_log, dt_bias, g_gdn_out,
              w_conf_dw, b_conf_dw, g_conf_ln, b_conf_ln, w_out,
              w_ff_gate, w_ff_up, w_ff_down, w_router, b_router, w_exp_gate, w_exp_up, w_exp_down, g_final)
    bsz = x_prompt.shape[0]
    dt = x_prompt.dtype
    zero_s = jnp.zeros((DEPTH, bsz, N_GDN_HEADS, GDN_HEAD_DIM, GDN_HEAD_DIM), dt)
    zero_conv = jnp.zeros((DEPTH, bsz, GDN_CONV_WIDTH - 1, QKV_WIDTH), dt)
    zero_conf = jnp.zeros((DEPTH, bsz, CONF_KERNEL - 1, CONF_WIDTH), dt)
    y_prompt, s_p, conv_p, conf_p = decoder_trunk(x_prompt, c_prompt, zero_conv, zero_s, zero_conf, params)
    y_sample, s_s, conv_s, conf_s = decoder_trunk(x_sample, c_sample, state_gdn_conv, state_gdn,
                                                  state_conf_conv, params)
    return (y_prompt, y_sample, s_p, conv_p, conf_p, s_s, conv_s, conf_s)
```

```python
import functools
import math

import jax
import jax.numpy as jnp
from jax import lax
from jax.experimental import pallas as pl
from jax.experimental.pallas import tpu as pltpu

f32 = jnp.float32
bf16 = jnp.bfloat16
HIGHEST = lax.Precision.HIGHEST

D = 1024
NB_P, L_P = 8, 2048
NB_S, L_S = 128, 4
DEPTH = 4
NH, DH = 4, 128
GW = NH * DH
QKV = 3 * GW
GCONV = 4
GCHUNK = 64
CW = D - GW
CGROUPS = 4
CK = 31
FF_D = 2816
NE = 8
FF_E = 3584
N_MOD = 6
RMS_EPS = 1e-6
LN_EPS = 1e-5

SUBLANES = 8
LANES = 128
VMEM_LIMIT = 56 * 1024 * 1024

TPAD = SUBLANES
T_P = NB_P * L_P
T_S = NB_S * TPAD
T_ALL = T_P + T_S
TM = 1024
NT = T_ALL // TM
NT_P = T_P // TM
TILES_PER_SEQ = L_P // TM
PROJ_W = 3200
COL_GLU = 2048
COL_BA = 3072
TN_IN = 640
HIST_C = 32
SH1, SC1, G1, SH2, SC2, G2 = range(N_MOD)

LB_G = 512
LB_C = 512
TILE = 128
TF_D = 256
TF_E = 512
MOE_BLK = 1024
N_ASG = 2 * T_ALL
N_BLK = N_ASG // MOE_BLK + NE


def _cparams(sem):
    return pltpu.CompilerParams(dimension_semantics=sem, vmem_limit_bytes=VMEM_LIMIT)


def _dot(a, b):
    return jnp.dot(a.astype(bf16), b.astype(bf16), preferred_element_type=f32)


def _dot_hi(a, b):
    return jnp.dot(a, b, precision=HIGHEST, preferred_element_type=f32)


def _dot_nt(a, b):
    return lax.dot_general(a.astype(bf16), b.astype(bf16), (((1,), (1,)), ((), ())),
                           preferred_element_type=f32)


def _dot_nt_hi(a, b):
    return lax.dot_general(a, b, (((1,), (1,)), ((), ())), precision=HIGHEST,
                           preferred_element_type=f32)


def _dot_tn(a, b):
    return lax.dot_general(a, b, (((0,), (0,)), ((), ())), preferred_element_type=f32)


def _silu(x):
    return x * jax.nn.sigmoid(x)


def _rms(x, g):
    return x * lax.rsqrt(jnp.mean(x * x, axis=-1, keepdims=True) + RMS_EPS) * g


def _ada_kernel(c_ref, w_ref, b_ref, o_ref):
    a = _silu(c_ref[...])
    o_ref[0] = _dot(a, w_ref[0]) + b_ref[0]


def _ada(c_all, w_ada, b_ada):
    n = c_all.shape[0]
    tn = 1536
    return pl.pallas_call(
        _ada_kernel,
        grid=(DEPTH, N_MOD * D // tn),
        in_specs=[
            pl.BlockSpec((n, D), lambda l, j: (0, 0)),
            pl.BlockSpec((1, D, tn), lambda l, j: (l, 0, j)),
            pl.BlockSpec((1, 1, tn), lambda l, j: (l, 0, j)),
        ],
        out_specs=pl.BlockSpec((1, n, tn), lambda l, j: (l, 0, j)),
        out_shape=jax.ShapeDtypeStruct((DEPTH, n, N_MOD * D), f32),
        compiler_params=_cparams(("parallel", "parallel")),
        name="ada",
    )(c_all, w_ada, b_ada.reshape(DEPTH, 1, N_MOD * D))


def _mod_specs(k, ngrid):
    if ngrid == 1:
        return [pl.BlockSpec((1, 1, D), lambda i: (jnp.minimum(i // TILES_PER_SEQ, NB_P - 1), 0, k)),
                pl.BlockSpec((NB_S, D), lambda i: (0, k))]
    return [pl.BlockSpec((1, 1, D), lambda i, j: (jnp.minimum(i // TILES_PER_SEQ, NB_P - 1), 0, k)),
            pl.BlockSpec((NB_S, D), lambda i, j: (0, k))]


def _sample_rows(body):
    def step(b, carry):
        body(pl.ds(pl.multiple_of(b * TPAD, TPAD), TPAD), b)
        return carry
    lax.fori_loop(0, NB_S, step, 0)


def _in_kernel(x_ref, g_ref, shp, scp, shs, scs, w_ref, o_ref, h_scr, hf_scr):
    i = pl.program_id(0)
    j = pl.program_id(1)

    @pl.when(jnp.logical_and(j == 0, i < NT_P))
    def _():
        h = _rms(x_ref[...], g_ref[0]) * (1.0 + scp[0]) + shp[0]
        h_scr[...] = h.astype(bf16)

    @pl.when(jnp.logical_and(j == 0, i >= NT_P))
    def _():
        def body(rows, b):
            hf_scr[rows, :] = (_rms(x_ref[rows, :], g_ref[0]) * (1.0 + scs[pl.ds(b, 1), :])
                               + shs[pl.ds(b, 1), :])
        _sample_rows(body)
        h_scr[...] = hf_scr[...].astype(bf16)

    o_ref[...] = _dot(h_scr[...], w_ref[0])


def _in_proj(layer, x, g1, modp, mods, w_cat):
    return pl.pallas_call(
        _in_kernel,
        grid=(NT, PROJ_W // TN_IN),
        in_specs=[
            pl.BlockSpec((TM, D), lambda i, j: (i, 0)),
            pl.BlockSpec((1, 1, D), lambda i, j: (layer, 0, 0)),
            *_mod_specs(SH1, 2)[:1], *_mod_specs(SC1, 2)[:1],
            *_mod_specs(SH1, 2)[1:], *_mod_specs(SC1, 2)[1:],
            pl.BlockSpec((1, D, TN_IN), lambda i, j: (layer, 0, j)),
        ],
        out_specs=pl.BlockSpec((TM, TN_IN), lambda i, j: (i, j)),
        out_shape=jax.ShapeDtypeStruct((T_ALL, PROJ_W), f32),
        scratch_shapes=[pltpu.VMEM((TM, D), bf16), pltpu.VMEM((TM, D), f32)],
        compiler_params=_cparams(("parallel", "arbitrary")),
        name=f"in_proj_{layer}",
    )(x, g1, modp, modp, mods, mods, w_cat)


def _tile_masks(blk):
    ri = lax.broadcasted_iota(jnp.int32, (TILE, TILE), 0)
    ci = lax.broadcasted_iota(jnp.int32, (TILE, TILE), 1)
    same = (ri // blk) == (ci // blk)
    incl = jnp.logical_and(same, ri >= ci)
    strict = jnp.logical_and(same, ri > ci)
    return same, incl, strict


def _gate_cumsums(g_all, blk):
    same, incl, _ = _tile_masks(blk)
    gc = _dot_hi(incl.astype(f32), g_all)
    gt = _dot_hi(same.astype(f32), g_all)
    return gc, gt


def _unit_lower_inverse(a_mat, blk, base, nsq):
    ri = lax.broadcasted_iota(jnp.int32, (TILE, TILE), 0)
    ci = lax.broadcasted_iota(jnp.int32, (TILE, TILE), 1)
    p = jnp.where((ri // base) == (ci // base), -a_mat, 0.0)
    tinv = (ri == ci).astype(f32) + p
    for _ in range(nsq):
        p = _dot(p, p)
        tinv = tinv + _dot(p, tinv)
    b = base
    while b < blk:
        off = jnp.logical_and((ri // (2 * b)) == (ci // (2 * b)), (ri // b) != (ci // b))
        tinv = tinv - _dot(tinv, _dot(jnp.where(off, a_mat, 0.0), tinv))
        b *= 2
    return tinv


def _delta_prepare(q, k, v, beta, gc, gt, blk, nsq):
    _, incl, strict = _tile_masks(blk)
    lane = lax.broadcasted_iota(jnp.int32, (TILE, DH), 1)
    a_op = jnp.where(lane == 0, gc, jnp.where(lane == 1, 1.0, 0.0))
    b_op = jnp.where(lane == 0, 1.0, jnp.where(lane == 1, -gc, 0.0))
    dmat = _dot_nt_hi(a_op, b_op)
    decay = jnp.where(incl, jnp.exp(jnp.where(incl, dmat, 0.0)), 0.0)
    egc = jnp.exp(gc)
    kb = k * beta
    a_mat = jnp.where(strict, _dot_nt(kb, k) * decay, 0.0)
    x = jnp.concatenate([v * beta, kb * egc], axis=1)
    x = _dot(_unit_lower_inverse(a_mat, blk, SUBLANES, nsq), x)
    u = x[:, :DH]
    w = x[:, DH:]
    qk = _dot_nt(q, k) * decay
    q_dec = q * egc
    k_dec = k * jnp.exp(gt - gc)
    return u, w, q_dec, k_dec, qk


def _gdn_features(xc, ba, prm_ref, valid):
    s = _silu(xc)
    if valid is not None:
        s = jnp.where(valid, s, 0.0)
    qs, ks, vs = [], [], []
    for h in range(NH):
        qh = s[:, h * DH:(h + 1) * DH]
        kh = s[:, GW + h * DH:GW + (h + 1) * DH]
        qs.append(qh * lax.rsqrt(jnp.sum(qh * qh, axis=-1, keepdims=True) + 1e-6) * (DH ** -0.5))
        ks.append(kh * lax.rsqrt(jnp.sum(kh * kh, axis=-1, keepdims=True) + 1e-6))
        vs.append(s[:, 2 * GW + h * DH:2 * GW + (h + 1) * DH])
    beta = jax.nn.sigmoid(ba)
    xs = ba + prm_ref[pl.ds(1, 1), :]
    softplus = jnp.maximum(xs, 0.0) + jnp.log1p(jnp.exp(-jnp.abs(xs)))
    g = -jnp.exp(prm_ref[pl.ds(0, 1), :]) * softplus
    if valid is not None:
        beta = jnp.where(valid, beta, 0.0)
        g = jnp.where(valid, g, 0.0)
    return qs, ks, vs, beta, g


def _gated_out(o, z, gout):
    return _rms(o, gout) * _silu(z)


def _gdn_p_kernel(x_ref, halo_ref, ba_ref, wc_ref, prm_ref, gout_ref,
                  o_ref, s_ref, cst_ref,
                  q_scr, k_scr, v_scr, bg_scr, st_scr):
    j = pl.program_id(1)
    nj = pl.num_programs(1)

    @pl.when(j == 0)
    def _():
        st_scr[...] = jnp.zeros_like(st_scr)

    x = x_ref[:, :QKV]
    halo = jnp.where(j > 0, halo_ref[:, :QKV], 0.0)
    xp = jnp.concatenate([halo, x], axis=0)
    xc = x * wc_ref[0, pl.ds(GCONV - 1, 1), :]
    for s in range(1, GCONV):
        xc = xc + pltpu.roll(xp, s, 0)[SUBLANES:, :] * wc_ref[0, pl.ds(GCONV - 1 - s, 1), :]
    qs, ks, vs, beta, g = _gdn_features(xc, ba_ref[...], prm_ref.at[0], None)
    for h in range(NH):
        q_scr[:, h * DH:(h + 1) * DH] = qs[h]
        k_scr[:, h * DH:(h + 1) * DH] = ks[h]
        v_scr[:, h * DH:(h + 1) * DH] = vs[h]
    bg_scr[0] = beta
    bg_scr[1] = g

    nsq = int(math.log2(SUBLANES)) - 1
    nblk = TILE // GCHUNK

    def tile_body(t, carry):
        r0 = pl.multiple_of(t * TILE, TILE)
        rows = pl.ds(r0, TILE)
        beta_t = bg_scr[0, rows, :]
        gc_t, gt_t = _gate_cumsums(bg_scr[1, rows, :], GCHUNK)
        for h in range(NH):
            cols = slice(h * DH, (h + 1) * DH)
            q = q_scr[rows, cols]
            k = k_scr[rows, cols]
            v = v_scr[rows, cols]
            gc = gc_t[:, NH + h:NH + h + 1]
            gt = gt_t[:, NH + h:NH + h + 1]
            u, w, q_dec, k_dec, qk = _delta_prepare(q, k, v, beta_t[:, h:h + 1], gc, gt, GCHUNK, nsq)
            egt = jnp.exp(gt)
            s_cur = st_scr[h]
            vn, qs_ = [], []
            for c in range(nblk):
                rr = slice(c * GCHUNK, (c + 1) * GCHUNK)
                ws = _dot(jnp.concatenate([w[rr], q_dec[rr]], axis=0), s_cur)
                vn_c = u[rr] - ws[:GCHUNK]
                vn.append(vn_c)
                qs_.append(ws[GCHUNK:])
                s_cur = s_cur * egt[c * GCHUNK:c * GCHUNK + 1, :] + _dot_tn(k_dec[rr], vn_c)
            st_scr[h] = s_cur
            o = jnp.concatenate(qs_, axis=0) + _dot(qk, jnp.concatenate(vn, axis=0))
            z = x_ref[rows, pl.ds(QKV + h * DH, DH)]
            o_ref[rows, cols] = _gated_out(o, z, gout_ref[0]).astype(o_ref.dtype)
        return carry

    lax.fori_loop(0, LB_G // TILE, tile_body, 0)

    @pl.when(j == nj - 1)
    def _():
        s_ref[0] = st_scr[...]
        cst_ref[0] = x_ref[pl.ds(LB_G - SUBLANES, SUBLANES), :QKV]


def _gdn_prompt(layer, proj, w_gconv, gprm, g_gout):
    nj = L_P // LB_G
    return pl.pallas_call(
        _gdn_p_kernel,
        grid=(NB_P, nj),
        in_specs=[
            pl.BlockSpec((LB_G, COL_GLU), lambda b, j: (b * nj + j, 0)),
            pl.BlockSpec((SUBLANES, COL_GLU),
                         lambda b, j: (jnp.maximum((b * nj + j) * (LB_G // SUBLANES) - 1, 0), 0)),
            pl.BlockSpec((LB_G, LANES), lambda b, j: (b * nj + j, COL_BA // LANES)),
            pl.BlockSpec((1, GCONV, QKV), lambda b, j: (layer, 0, 0)),
            pl.BlockSpec((1, SUBLANES, LANES), lambda b, j: (layer, 0, 0)),
            pl.BlockSpec((1, 1, DH), lambda b, j: (layer, 0, 0)),
        ],
        out_specs=[
            pl.BlockSpec((LB_G, GW), lambda b, j: (b * nj + j, 0)),
            pl.BlockSpec((1, NH, DH, DH), lambda b, j: (b, 0, 0, 0)),
            pl.BlockSpec((1, SUBLANES, QKV), lambda b, j: (b, 0, 0)),
        ],
        out_shape=[
            jax.ShapeDtypeStruct((T_P, GW), bf16),
            jax.ShapeDtypeStruct((NB_P, NH, DH, DH), f32),
            jax.ShapeDtypeStruct((NB_P, SUBLANES, QKV), f32),
        ],
        scratch_shapes=[pltpu.VMEM((LB_G, GW), f32), pltpu.VMEM((LB_G, GW), f32),
                        pltpu.VMEM((LB_G, GW), f32), pltpu.VMEM((2, LB_G, LANES), f32),
                        pltpu.VMEM((NH, DH, DH), f32)],
        compiler_params=_cparams(("parallel", "arbitrary")),
        name=f"gdn_prompt_{layer}",
    )(proj, proj, proj, w_gconv, gprm, g_gout)


BT_S = TILE // TPAD


def _gdn_s_kernel(x_ref, ba_ref, cin_ref, s0_ref, wc_ref, prm_ref, gout_ref,
                  o_ref, s_ref, cst_ref):
    x = x_ref[:, :QKV]
    st = cin_ref[...]
    trow = lax.broadcasted_iota(jnp.int32, (TILE, 1), 0) % TPAD
    xc = x * wc_ref[0, pl.ds(GCONV - 1, 1), :]
    for s in range(1, GCONV):
        xs = jnp.where(trow >= s, pltpu.roll(x, s, 0), pltpu.roll(st, TILE - TPAD + s, 0))
        xc = xc + xs * wc_ref[0, pl.ds(GCONV - 1 - s, 1), :]
    valid = trow < L_S
    qs, ks, vs, beta, g = _gdn_features(xc, ba_ref[...], prm_ref.at[0], valid)
    cst_ref[...] = pltpu.roll(x, TPAD - (GCONV - 1) - (L_S - (GCONV - 1)), 0)

    nsq = int(math.ceil(math.log2(L_S))) - 1
    gc_t, gt_t = _gate_cumsums(g, TPAD)
    for h in range(NH):
        cols = slice(h * DH, (h + 1) * DH)
        gc = gc_t[:, NH + h:NH + h + 1]
        gt = gt_t[:, NH + h:NH + h + 1]
        u, w, q_dec, k_dec, qk = _delta_prepare(qs[h], ks[h], vs[h], beta[:, h:h + 1], gc, gt, TPAD, nsq)
        egt = jnp.exp(gt)
        vn, qs_ = [], []
        for b in range(BT_S):
            rr = slice(b * TPAD, (b + 1) * TPAD)
            s0 = s0_ref[b, h]
            ws = _dot(jnp.concatenate([w[rr], q_dec[rr]], axis=0), s0)
            vn_b = u[rr] - ws[:TPAD]
            vn.append(vn_b)
            qs_.append(ws[TPAD:])
            s_ref[b, h] = s0 * egt[b * TPAD:b * TPAD + 1, :] + _dot_tn(k_dec[rr], vn_b)
        o = jnp.concatenate(qs_, axis=0) + _dot(qk, jnp.concatenate(vn, axis=0))
        z = x_ref[:, pl.ds(QKV + h * DH, DH)]
        o_ref[:, cols] = _gated_out(o, z, gout_ref[0]).astype(o_ref.dtype)


def _gdn_sample(layer, proj, conv_in, s0, w_gconv, gprm, g_gout):
    row0 = T_P // TILE
    return pl.pallas_call(
        _gdn_s_kernel,
        grid=(T_S // TILE,),
        in_specs=[
            pl.BlockSpec((TILE, COL_GLU), lambda i: (row0 + i, 0)),
            pl.BlockSpec((TILE, LANES), lambda i: (row0 + i, COL_BA // LANES)),
            pl.BlockSpec((TILE, QKV), lambda i: (i, 0)),
            pl.BlockSpec((BT_S, NH, DH, DH), lambda i: (i, 0, 0, 0)),
            pl.BlockSpec((1, GCONV, QKV), lambda i: (layer, 0, 0)),
            pl.BlockSpec((1, SUBLANES, LANES), lambda i: (layer, 0, 0)),
            pl.BlockSpec((1, 1, DH), lambda i: (layer, 0, 0)),
        ],
        out_specs=[
            pl.BlockSpec((TILE, GW), lambda i: (i, 0)),
            pl.BlockSpec((BT_S, NH, DH, DH), lambda i: (i, 0, 0, 0)),
            pl.BlockSpec((TILE, QKV), lambda i: (i, 0)),
        ],
        out_shape=[
            jax.ShapeDtypeStruct((T_S, GW), bf16),
            jax.ShapeDtypeStruct((NB_S, NH, DH, DH), f32),
            jax.ShapeDtypeStruct((T_S, QKV), f32),
        ],
        compiler_params=_cparams(("parallel",)),
        name=f"gdn_sample_{layer}",
    )(proj, proj, conv_in, s0, w_gconv, gprm, g_gout)


def _group_ln_silu(y, gl_ref, bl_ref):
    outs = []
    for gi in range(CGROUPS):
        w = CW // CGROUPS
        yg = y[:, gi * w:(gi + 1) * w]
        mu = jnp.mean(yg, axis=-1, keepdims=True)
        d = yg - mu
        var = jnp.mean(d * d, axis=-1, keepdims=True)
        outs.append(d * lax.rsqrt(var + LN_EPS))
    yn = jnp.concatenate(outs, axis=1) * gl_ref[0] + bl_ref[0]
    return _silu(yn)


def _conf_p_kernel(x_ref, halo_ref, w_ref, b_ref, gl_ref, bl_ref, o_ref, hst_ref, xp_scr):
    j = pl.program_id(1)
    nj = pl.num_programs(1)
    h = x_ref[:, :CW] * jax.nn.sigmoid(x_ref[:, CW:])
    hh = halo_ref[:, :CW] * jax.nn.sigmoid(halo_ref[:, CW:])
    xp_scr[pl.ds(0, HIST_C), :] = jnp.where(j > 0, hh, 0.0)
    xp_scr[pl.ds(HIST_C, LB_C), :] = h
    off = HIST_C - (CK - 1)
    y = xp_scr[pl.ds(off, LB_C), :] * w_ref[0, pl.ds(0, 1), :]
    for t in range(1, CK):
        y = y + xp_scr[pl.ds(off + t, LB_C), :] * w_ref[0, pl.ds(t, 1), :]
    y = y + b_ref[0]
    o_ref[...] = _group_ln_silu(y, gl_ref, bl_ref).astype(o_ref.dtype)

    @pl.when(j == nj - 1)
    def _():
        hst_ref[0] = h[LB_C - HIST_C:, :]


def _conf_prompt(layer, proj, w_dw, b_dw, g_ln, b_ln):
    nj = L_P // LB_C
    cb = COL_GLU // (2 * CW)
    return pl.pallas_call(
        _conf_p_kernel,
        grid=(NB_P, nj),
        in_specs=[
            pl.BlockSpec((LB_C, 2 * CW), lambda b, j: (b * nj + j, cb)),
            pl.BlockSpec((HIST_C, 2 * CW),
                         lambda b, j: (jnp.maximum((b * nj + j) * (LB_C // HIST_C) - 1, 0), cb)),
            pl.BlockSpec((1, CK, CW), lambda b, j: (layer, 0, 0)),
            pl.BlockSpec((1, 1, CW), lambda b, j: (layer, 0, 0)),
            pl.BlockSpec((1, 1, CW), lambda b, j: (layer, 0, 0)),
            pl.BlockSpec((1, 1, CW), lambda b, j: (layer, 0, 0)),
        ],
        out_specs=[
            pl.BlockSpec((LB_C, CW), lambda b, j: (b * nj + j, 0)),
            pl.BlockSpec((1, HIST_C, CW), lambda b, j: (b, 0, 0)),
        ],
        out_shape=[
            jax.ShapeDtypeStruct((T_P, CW), bf16),
            jax.ShapeDtypeStruct((NB_P, HIST_C, CW), f32),
        ],
        scratch_shapes=[pltpu.VMEM((HIST_C + LB_C, CW), f32)],
        compiler_params=_cparams(("parallel", "arbitrary")),
        name=f"conf_prompt_{layer}",
    )(proj, proj, w_dw, b_dw, g_ln, b_ln)


BT_C = 16


def _conf_s_kernel(x_ref, hin_ref, wsh_ref, b_ref, gl_ref, bl_ref, o_ref, hst_ref, y_scr):
    h = x_ref[:, :CW] * jax.nn.sigmoid(x_ref[:, CW:])
    trow = lax.broadcasted_iota(jnp.int32, (TPAD, 1), 0)
    for b in range(BT_C):
        xp = jnp.concatenate([hin_ref[pl.ds(b * HIST_C, HIST_C), :], h[b * TPAD:(b + 1) * TPAD, :]], axis=0)
        y8 = jnp.zeros((TPAD, CW), f32)
        for t in range(L_S):
            yt = jnp.sum(xp * wsh_ref[0, t], axis=0, keepdims=True)
            y8 = jnp.where(trow == t, yt, y8)
        y_scr[pl.ds(b * TPAD, TPAD), :] = y8
        hst_ref[pl.ds(b * HIST_C, HIST_C), :] = pltpu.roll(xp, HIST_C + TPAD - L_S, 0)[:HIST_C, :]
    y = y_scr[...] + b_ref[0]
    o_ref[...] = _group_ln_silu(y, gl_ref, bl_ref).astype(o_ref.dtype)


def _conf_sample(layer, proj, hist_in, w_shift, b_dw, g_ln, b_ln):
    row0 = T_P // (BT_C * TPAD)
    cb = COL_GLU // (2 * CW)
    return pl.pallas_call(
        _conf_s_kernel,
        grid=(NB_S // BT_C,),
        in_specs=[
            pl.BlockSpec((BT_C * TPAD, 2 * CW), lambda i: (row0 + i, cb)),
            pl.BlockSpec((BT_C * HIST_C, CW), lambda i: (i, 0)),
            pl.BlockSpec((1, L_S, HIST_C + TPAD, CW), lambda i: (layer, 0, 0, 0)),
            pl.BlockSpec((1, 1, CW), lambda i: (layer, 0, 0)),
            pl.BlockSpec((1, 1, CW), lambda i: (layer, 0, 0)),
            pl.BlockSpec((1, 1, CW), lambda i: (layer, 0, 0)),
        ],
        out_specs=[
            pl.BlockSpec((BT_C * TPAD, CW), lambda i: (i, 0)),
            pl.BlockSpec((BT_C * HIST_C, CW), lambda i: (i, 0)),
        ],
        out_shape=[
            jax.ShapeDtypeStruct((T_S, CW), bf16),
            jax.ShapeDtypeStruct((NB_S * HIST_C, CW), f32),
        ],
        scratch_shapes=[pltpu.VMEM((BT_C * TPAD, CW), f32)],
        compiler_params=_cparams(("parallel",)),
        name=f"conf_sample_{layer}",
    )(proj, hist_in, w_shift, b_dw, g_ln, b_ln)


def _out_kernel(moe, x_ref, oap, obp, oas, obs, w_ref, g_ref,
                g1p, scp, shp, g1s, scs, shs, *rest):
    if moe:
        wr_ref, br_ref, xo_ref, h2_ref, rt_ref = rest
        hf_scr = h2_ref
    else:
        xo_ref, h2_ref, hf_scr = rest
    i = pl.program_id(0)

    def attn(oa, ob):
        return (jnp.dot(oa[...], w_ref[0, :GW, :], preferred_element_type=f32)
                + jnp.dot(ob[...], w_ref[0, GW:, :], preferred_element_type=f32))

    @pl.when(i < NT_P)
    def _():
        xn = x_ref[...] + g1p[0] * attn(oap, obp)
        xo_ref[...] = xn
        hf_scr[...] = _rms(xn, g_ref[0]) * (1.0 + scp[0]) + shp[0]

    @pl.when(i >= NT_P)
    def _():
        hf_scr[...] = attn(oas, obs)

        def body(rows, b):
            m = pl.ds(b, 1)
            xn = x_ref[rows, :] + g1s[m, :] * hf_scr[rows, :]
            xo_ref[rows, :] = xn
            hf_scr[rows, :] = _rms(xn, g_ref[0]) * (1.0 + scs[m, :]) + shs[m, :]
        _sample_rows(body)

    hf = hf_scr[...]
    if not moe:
        h2_ref[...] = hf.astype(h2_ref.dtype)
    if moe:
        logits = _dot_hi(hf, wr_ref[0]) + br_ref[0]
        lane = lax.broadcasted_iota(jnp.int32, logits.shape, 1)
        ex = jnp.exp(logits - jnp.max(logits, axis=-1, keepdims=True))
        probs = ex / jnp.sum(ex, axis=-1, keepdims=True)
        m1 = jnp.max(probs, axis=-1, keepdims=True)
        i1 = jnp.min(jnp.where(probs == m1, lane, LANES), axis=-1, keepdims=True)
        rest_p = jnp.where(lane == i1, -1.0, probs)
        m2 = jnp.max(rest_p, axis=-1, keepdims=True)
        i2 = jnp.min(jnp.where(rest_p == m2, lane, LANES), axis=-1, keepdims=True)
        den = m1 + m2
        rt_ref[...] = jnp.where(lane == 0, i1.astype(f32),
                                jnp.where(lane == 1, i2.astype(f32),
                                          jnp.where(lane == 2, m1 / den,
                                                    jnp.where(lane == 3, m2 / den, 0.0))))


def _out_proj(layer, moe, x, oa_p, ob_p, oa_s, ob_s, w_out, g2, modp, mods, wr=None, br=None):
    idx = layer // 2
    in_specs = [
        pl.BlockSpec((TM, D), lambda i: (i, 0)),
        pl.BlockSpec((TM, GW), lambda i: (jnp.minimum(i, NT_P - 1), 0)),
        pl.BlockSpec((TM, CW), lambda i: (jnp.minimum(i, NT_P - 1), 0)),
        pl.BlockSpec((TM, GW), lambda i: (0, 0)),
        pl.BlockSpec((TM, CW), lambda i: (0, 0)),
        pl.BlockSpec((1, D, D), lambda i: (layer, 0, 0)),
        pl.BlockSpec((1, 1, D), lambda i: (layer, 0, 0)),
        _mod_specs(G1, 1)[0], _mod_specs(SC2, 1)[0], _mod_specs(SH2, 1)[0],
        _mod_specs(G1, 1)[1], _mod_specs(SC2, 1)[1], _mod_specs(SH2, 1)[1],
    ]
    args = [x, oa_p, ob_p, oa_s, ob_s, w_out, g2, modp, modp, modp, mods, mods, mods]
    out_specs = [pl.BlockSpec((TM, D), lambda i: (i, 0)), pl.BlockSpec((TM, D), lambda i: (i, 0))]
    out_shape = [jax.ShapeDtypeStruct((T_ALL, D), f32),
                 jax.ShapeDtypeStruct((T_ALL, D), f32 if moe else bf16)]
    if moe:
        in_specs += [pl.BlockSpec((1, D, LANES), lambda i: (idx, 0, 0)),
                     pl.BlockSpec((1, 1, LANES), lambda i: (idx, 0, 0))]
        args += [wr, br]
        out_specs.append(pl.BlockSpec((TM, LANES), lambda i: (i, 0)))
        out_shape.append(jax.ShapeDtypeStruct((T_ALL, LANES), f32))
    return pl.pallas_call(
        functools.partial(_out_kernel, moe),
        grid=(NT,),
        in_specs=in_specs,
        out_specs=out_specs,
        out_shape=out_shape,
        scratch_shapes=[] if moe else [pltpu.VMEM((TM, D), f32)],
        compiler_params=_cparams(("parallel",)),
        name=f"out_proj_{layer}",
    )(*args)


def _residual(i, x_ref, f_ref, g2p, g2s, o_ref, gf_ref):
    def fin(v):
        return v if gf_ref is None else _rms(v, gf_ref[...])

    @pl.when(i < NT_P)
    def _():
        o_ref[...] = fin(x_ref[...] + g2p[0] * f_ref[...])

    @pl.when(i >= NT_P)
    def _():
        def body(rows, b):
            o_ref[rows, :] = fin(x_ref[rows, :] + g2s[pl.ds(b, 1), :] * f_ref[rows, :])
        _sample_rows(body)


def _ffn_kernel(h_ref, wg_ref, wu_ref, wd_ref, x_ref, g2p, g2s, o_ref, acc_scr):
    i = pl.program_id(0)
    f = pl.program_id(1)
    nf = pl.num_programs(1)
    h = h_ref[...]
    a = _silu(_dot(h, wg_ref[0])) * _dot(h, wu_ref[0])
    part = _dot(a, wd_ref[0])

    @pl.when(f == 0)
    def _():
        acc_scr[...] = part

    @pl.when(f > 0)
    def _():
        acc_scr[...] += part

    @pl.when(f == nf - 1)
    def _():
        _residual(i, x_ref, acc_scr, g2p, g2s, o_ref, None)


def _ffn_dense(layer, h2, x, wg, wu, wd, modp, mods):
    idx = layer // 2
    mp, ms = _mod_specs(G2, 2)
    return pl.pallas_call(
        _ffn_kernel,
        grid=(NT, FF_D // TF_D),
        in_specs=[
            pl.BlockSpec((TM, D), lambda i, f: (i, 0)),
            pl.BlockSpec((1, D, TF_D), lambda i, f: (idx, 0, f)),
            pl.BlockSpec((1, D, TF_D), lambda i, f: (idx, 0, f)),
            pl.BlockSpec((1, TF_D, D), lambda i, f: (idx, f, 0)),
            pl.BlockSpec((TM, D), lambda i, f: (i, 0)),
            mp, ms,
        ],
        out_specs=pl.BlockSpec((TM, D), lambda i, f: (i, 0)),
        out_shape=jax.ShapeDtypeStruct((T_ALL, D), f32),
        scratch_shapes=[pltpu.VMEM((TM, D), f32)],
        compiler_params=_cparams(("parallel", "arbitrary")),
        name=f"ffn_dense_{layer}",
    )(h2, wg, wu, wd, x, modp, mods)


def _expert_kernel(be_ref, tok_ref, nv_ref, h_hbm, sp_ref, wg_ref, wu_ref, wd_ref,
                   y_ref, xg_scr, xb_scr, sem):
    m = pl.program_id(0)
    f = pl.program_id(1)
    nf = pl.num_programs(1)
    active = m < nv_ref[0]

    @pl.when(jnp.logical_and(f == 0, active))
    def _():
        def issue(r, carry):
            tok = tok_ref[m * MOE_BLK + r]
            pltpu.make_async_copy(h_hbm.at[pl.ds(tok, 1), :], xg_scr.at[pl.ds(r, 1), :], sem).start()
            return carry
        lax.fori_loop(0, MOE_BLK, issue, 0)
        pltpu.make_async_copy(h_hbm.at[pl.ds(0, MOE_BLK), :], xg_scr, sem).wait()
        xb_scr[...] = xg_scr[...].astype(bf16)

    @pl.when(f == 0)
    def _():
        y_ref[...] = jnp.zeros_like(y_ref)

    @pl.when(active)
    def _():
        xb = xb_scr[...]
        a = _silu(_dot(xb, wg_ref[0, 0])) * _dot(xb, wu_ref[0, 0])
        y_ref[...] += _dot(a, wd_ref[0, 0])

    @pl.when(jnp.logical_and(f == nf - 1, active))
    def _():
        y_ref[...] = y_ref[...] * sp_ref[...]


def _experts(idx, h2, block_e, slot_tok, n_valid, slot_w, wg, wu, wd):
    nf = FF_E // TF_E

    def wmap_up(m, f, be, tok, nv):
        return (idx, be[m], 0, jnp.where(m < nv[0], f, nf - 1))

    def wmap_down(m, f, be, tok, nv):
        return (idx, be[m], jnp.where(m < nv[0], f, nf - 1), 0)

    grid_spec = pltpu.PrefetchScalarGridSpec(
        num_scalar_prefetch=3,
        grid=(N_BLK, nf),
        in_specs=[
            pl.BlockSpec(memory_space=pl.ANY),
            pl.BlockSpec((MOE_BLK, 1), lambda m, f, be, tok, nv: (m, 0)),
            pl.BlockSpec((1, 1, D, TF_E), wmap_up),
            pl.BlockSpec((1, 1, D, TF_E), wmap_up),
            pl.BlockSpec((1, 1, TF_E, D), wmap_down),
        ],
        out_specs=pl.BlockSpec((MOE_BLK, D), lambda m, f, be, tok, nv: (m, 0)),
        scratch_shapes=[pltpu.VMEM((MOE_BLK, D), f32), pltpu.VMEM((MOE_BLK, D), bf16),
                        pltpu.SemaphoreType.DMA(())],
    )
    return pl.pallas_call(
        _expert_kernel,
        grid_spec=grid_spec,
        out_shape=jax.ShapeDtypeStruct((N_BLK * MOE_BLK, D), f32),
        compiler_params=_cparams(("arbitrary", "arbitrary")),
        name=f"experts_{idx}",
    )(block_e, slot_tok, n_valid, h2, slot_w, wg, wu, wd)


def _combine_kernel(final, dest_ref, y_hbm, x_ref, g2p, g2s, *rest):
    if final:
        gf_ref, o_ref, g_scr, f_scr, sem = rest
    else:
        o_ref, g_scr, f_scr, sem = rest
        gf_ref = None
    i = pl.program_id(0)

    def issue(r, carry):
        t = i * TM + r
        for k in range(2):
            d = dest_ref[2 * t + k]
            pltpu.make_async_copy(y_hbm.at[pl.ds(d, 1), :], g_scr.at[k, pl.ds(r, 1), :], sem.at[k]).start()
        return carry
    lax.fori_loop(0, TM, issue, 0)
    for k in range(2):
        pltpu.make_async_copy(y_hbm.at[pl.ds(0, TM), :], g_scr.at[k], sem.at[k]).wait()
    f_scr[...] = g_scr[0] + g_scr[1]
    _residual(i, x_ref, f_scr, g2p, g2s, o_ref, gf_ref)


def _combine(final, dest, yb, x, modp, mods, g_final):
    mp, ms = _mod_specs(G2, 1)
    wrap = lambda spec: pl.BlockSpec(spec.block_shape, lambda i, d, _f=spec.index_map: _f(i))
    in_specs = [pl.BlockSpec(memory_space=pl.ANY),
                pl.BlockSpec((TM, D), lambda i, d: (i, 0)), wrap(mp), wrap(ms)]
    args = [yb, x, modp, mods]
    if final:
        in_specs.append(pl.BlockSpec((1, D), lambda i, d: (0, 0)))
        args.append(g_final)
    grid_spec = pltpu.PrefetchScalarGridSpec(
        num_scalar_prefetch=1,
        grid=(NT,),
        in_specs=in_specs,
        out_specs=pl.BlockSpec((TM, D), lambda i, d: (i, 0)),
        scratch_shapes=[pltpu.VMEM((2, TM, D), f32), pltpu.VMEM((TM, D), f32),
                        pltpu.SemaphoreType.DMA((2,))],
    )
    return pl.pallas_call(
        functools.partial(_combine_kernel, final),
        grid_spec=grid_spec,
        out_shape=jax.ShapeDtypeStruct((T_ALL, D), f32),
        compiler_params=_cparams(("arbitrary",)),
        name="combine_final" if final else "combine",
    )(dest, *args)


def _route(rt):
    e = rt[:, :2].astype(jnp.int32).reshape(N_ASG)
    p = rt[:, 2:4].reshape(N_ASG)
    onehot = (e[:, None] == jnp.arange(NE, dtype=jnp.int32)[None, :]).astype(jnp.int32)
    csum = jnp.cumsum(onehot, axis=0)
    counts = csum[-1]
    rank = jnp.sum(onehot * (csum - 1), axis=1)
    padded = (counts + MOE_BLK - 1) // MOE_BLK * MOE_BLK
    pad_end = jnp.cumsum(padded)
    pad_start = pad_end - padded
    dest = (jnp.sum(onehot * pad_start[None, :], axis=1) + rank).astype(jnp.int32)
    n_slots = N_BLK * MOE_BLK
    tok = jnp.arange(N_ASG, dtype=jnp.int32) // 2
    slot_tok = jnp.zeros((n_slots,), jnp.int32).at[dest].set(tok, unique_indices=True)
    slot_w = jnp.zeros((n_slots,), f32).at[dest].set(p, unique_indices=True)
    n_valid = (pad_end[-1] // MOE_BLK).astype(jnp.int32)
    blk = jnp.minimum(jnp.arange(N_BLK, dtype=jnp.int32), n_valid - 1)
    block_e = jnp.minimum(jnp.searchsorted(pad_end, blk * MOE_BLK, side="right"), NE - 1).astype(jnp.int32)
    return dest, slot_tok, slot_w.reshape(n_slots, 1), block_e, n_valid.reshape(1)


def kernel(x_prompt, x_sample, c_prompt, c_sample, state_gdn, state_gdn_conv, state_conf_conv, w_ada, b_ada, g_norm1, g_norm2, w_in, w_gdn_conv, a_log, dt_bias, g_gdn_out, w_conf_dw, b_conf_dw, g_conf_ln, b_conf_ln, w_out, w_ff_gate, w_ff_up, w_ff_down, w_router, b_router, w_exp_gate, w_exp_up, w_exp_down, g_final):
    xs_pad = jnp.pad(x_sample, ((0, 0), (0, TPAD - L_S), (0, 0)))
    x = jnp.concatenate([x_prompt.reshape(T_P, D), xs_pad.reshape(T_S, D)], axis=0)
    c_all = jnp.concatenate([c_prompt, c_sample], axis=0)
    o1, o2, o4 = QKV + GW, QKV + GW + 2 * NH, QKV + GW + 2 * NH + 2 * CW
    w_cat = jnp.concatenate([w_in[:, :, :o1], w_in[:, :, o2:o4], w_in[:, :, o1:o2],
                             jnp.zeros((DEPTH, D, PROJ_W - o4), f32)], axis=-1)
    lane_pad = ((0, 0), (NH, LANES - 2 * NH))
    gprm = jnp.stack([jnp.pad(a_log, lane_pad), jnp.pad(dt_bias, lane_pad)], axis=1)
    gprm = jnp.pad(gprm, ((0, 0), (0, SUBLANES - 2), (0, 0)))
    wr_pad = jnp.pad(w_router, ((0, 0), (0, 0), (0, LANES - NE)))
    br_pad = jnp.pad(b_router, ((0, 0), (0, LANES - NE)), constant_values=-1e30).reshape(-1, 1, LANES)
    w_shift = jnp.stack([jnp.pad(w_conf_dw, ((0, 0), (t + HIST_C - (CK - 1), TPAD - 1 - t), (0, 0)))
                         for t in range(L_S)], axis=1)
    gconv_in = jnp.pad(state_gdn_conv, ((0, 0), (0, 0), (TPAD - (GCONV - 1), 0), (0, 0))).reshape(DEPTH, T_S, QKV)
    conf_in = jnp.pad(state_conf_conv, ((0, 0), (0, 0), (HIST_C - (CK - 1), 0), (0, 0))).reshape(DEPTH, NB_S * HIST_C, CW)
    w_out = w_out.astype(bf16)
    g1 = g_norm1.reshape(DEPTH, 1, D)
    g2 = g_norm2.reshape(DEPTH, 1, D)
    gout = g_gdn_out.reshape(DEPTH, 1, DH)
    b_dw = b_conf_dw.reshape(DEPTH, 1, CW)
    g_ln = g_conf_ln.reshape(DEPTH, 1, CW)
    b_ln = b_conf_ln.reshape(DEPTH, 1, CW)

    mod = _ada(c_all, w_ada, b_ada)

    sp_l, cp_l, fp_l, ss_l, cs_l, fs_l = [], [], [], [], [], []
    for layer in range(DEPTH):
        modp = mod[layer, :NB_P].reshape(NB_P, 1, N_MOD * D)
        mods = mod[layer, NB_P:]
        proj = _in_proj(layer, x, g1, modp, mods, w_cat)
        oa_p, s_p, c_p = _gdn_prompt(layer, proj, w_gdn_conv, gprm, gout)
        oa_s, s_s, c_s = _gdn_sample(layer, proj, gconv_in[layer], state_gdn[layer], w_gdn_conv, gprm, gout)
        ob_p, f_p = _conf_prompt(layer, proj, w_conf_dw, b_dw, g_ln, b_ln)
        ob_s, f_s = _conf_sample(layer, proj, conf_in[layer], w_shift, b_dw, g_ln, b_ln)
        sp_l.append(s_p)
        cp_l.append(c_p[:, TPAD - (GCONV - 1):, :])
        fp_l.append(f_p[:, HIST_C - (CK - 1):, :])
        ss_l.append(s_s)
        cs_l.append(c_s.reshape(NB_S, TPAD, QKV)[:, TPAD - (GCONV - 1):, :])
        fs_l.append(f_s.reshape(NB_S, HIST_C, CW)[:, HIST_C - (CK - 1):, :])
        if layer % 2 == 0:
            x, h2 = _out_proj(layer, False, x, oa_p, ob_p, oa_s, ob_s, w_out, g2, modp, mods)
            x = _ffn_dense(layer, h2, x, w_ff_gate, w_ff_up, w_ff_down, modp, mods)
        else:
            x, h2, rt = _out_proj(layer, True, x, oa_p, ob_p, oa_s, ob_s, w_out, g2, modp, mods, wr_pad, br_pad)
            dest, slot_tok, slot_w, block_e, n_valid = _route(rt)
            yb = _experts(layer // 2, h2, block_e, slot_tok, n_valid, slot_w, w_exp_gate, w_exp_up, w_exp_down)
            final = layer == DEPTH - 1
            x = _combine(final, dest, yb, x, modp, mods, g_final.reshape(1, D))

    y_prompt = x[:T_P].reshape(NB_P, L_P, D)
    y_sample = x[T_P:].reshape(NB_S, TPAD, D)[:, :L_S, :]
    return (y_prompt, y_sample, jnp.stack(sp_l), jnp.stack(cp_l), jnp.stack(fp_l),
            jnp.stack(ss_l), jnp.stack(cs_l), jnp.stack(fs_l))
```

```python
import functools
import math

import jax
import jax.numpy as jnp
from jax import lax
from jax.experimental import pallas as pl
from jax.experimental.pallas import tpu as pltpu

f32 = jnp.float32
bf16 = jnp.bfloat16
HIGHEST = lax.Precision.HIGHEST

D = 1024
NB_P, L_P = 8, 2048
NB_S, L_S = 128, 4
DEPTH = 4
NH, DH = 4, 128
GW = NH * DH
QKV = 3 * GW
GCONV = 4
GCHUNK = 64
CW = D - GW
CGROUPS = 4
CK = 31
FF_D = 2816
NE = 8
FF_E = 3584
N_MOD = 6
RMS_EPS = 1e-6
LN_EPS = 1e-5

SUBLANES = 8
LANES = 128
VMEM_LIMIT = 56 * 1024 * 1024

TPAD = SUBLANES
T_P = NB_P * L_P
T_S = NB_S * TPAD
T_ALL = T_P + T_S
TM = 1024
NT = T_ALL // TM
NT_P = T_P // TM
TILES_PER_SEQ = L_P // TM
PROJ_W = 3200
COL_GLU = 2048
COL_BA = 3072
TN_IN = 640
HIST_C = 32
SH1, SC1, G1, SH2, SC2, G2 = range(N_MOD)

LB_G = 512
LB_C = 512
TILE = 128
TILES_PER_ITER = 4
TF_D = 256
TF_E = 512
MOE_BLK = 1024
N_ASG = 2 * T_ALL
N_BLK = N_ASG // MOE_BLK + NE


def _cparams(sem):
    return pltpu.CompilerParams(dimension_semantics=sem, vmem_limit_bytes=VMEM_LIMIT)


def _dot(a, b):
    return jnp.dot(a.astype(bf16), b.astype(bf16), preferred_element_type=f32)


def _dot_hi(a, b):
    return jnp.dot(a, b, precision=HIGHEST, preferred_element_type=f32)


def _dot_nt(a, b):
    return lax.dot_general(a.astype(bf16), b.astype(bf16), (((1,), (1,)), ((), ())),
                           preferred_element_type=f32)


def _dot_nt_hi(a, b):
    return lax.dot_general(a, b, (((1,), (1,)), ((), ())), precision=HIGHEST,
                           preferred_element_type=f32)


def _dot_tn(a, b):
    return lax.dot_general(a, b, (((0,), (0,)), ((), ())), preferred_element_type=f32)


def _silu(x):
    return x * jax.nn.sigmoid(x)


def _rms(x, g):
    return x * lax.rsqrt(jnp.mean(x * x, axis=-1, keepdims=True) + RMS_EPS) * g


def _ada_kernel(c_ref, w_ref, b_ref, o_ref):
    a = _silu(c_ref[...])
    o_ref[0] = _dot(a, w_ref[0]) + b_ref[0]


def _ada(c_all, w_ada, b_ada):
    n = c_all.shape[0]
    tn = 1536
    return pl.pallas_call(
        _ada_kernel,
        grid=(DEPTH, N_MOD * D // tn),
        in_specs=[
            pl.BlockSpec((n, D), lambda l, j: (0, 0)),
            pl.BlockSpec((1, D, tn), lambda l, j: (l, 0, j)),
            pl.BlockSpec((1, 1, tn), lambda l, j: (l, 0, j)),
        ],
        out_specs=pl.BlockSpec((1, n, tn), lambda l, j: (l, 0, j)),
        out_shape=jax.ShapeDtypeStruct((DEPTH, n, N_MOD * D), f32),
        compiler_params=_cparams(("parallel", "parallel")),
        name="ada",
    )(c_all, w_ada, b_ada.reshape(DEPTH, 1, N_MOD * D))


def _mod_specs(k, ngrid):
    if ngrid == 1:
        return [pl.BlockSpec((1, 1, D), lambda i: (jnp.minimum(i // TILES_PER_SEQ, NB_P - 1), 0, k)),
                pl.BlockSpec((NB_S, D), lambda i: (0, k))]
    return [pl.BlockSpec((1, 1, D), lambda i, j: (jnp.minimum(i // TILES_PER_SEQ, NB_P - 1), 0, k)),
            pl.BlockSpec((NB_S, D), lambda i, j: (0, k))]


def _sample_rows(body):
    def step(b, carry):
        body(pl.ds(pl.multiple_of(b * TPAD, TPAD), TPAD), b)
        return carry
    lax.fori_loop(0, NB_S, step, 0)


def _in_kernel(x_ref, g_ref, shp, scp, shs, scs, w_ref, o_ref, h_scr, hf_scr):
    i = pl.program_id(0)
    j = pl.program_id(1)

    @pl.when(jnp.logical_and(j == 0, i < NT_P))
    def _():
        h = _rms(x_ref[...], g_ref[0]) * (1.0 + scp[0]) + shp[0]
        h_scr[...] = h.astype(bf16)

    @pl.when(jnp.logical_and(j == 0, i >= NT_P))
    def _():
        def body(rows, b):
            hf_scr[rows, :] = (_rms(x_ref[rows, :], g_ref[0]) * (1.0 + scs[pl.ds(b, 1), :])
                               + shs[pl.ds(b, 1), :])
        _sample_rows(body)
        h_scr[...] = hf_scr[...].astype(bf16)

    o_ref[...] = _dot(h_scr[...], w_ref[0])


def _in_proj(layer, x, g1, modp, mods, w_cat):
    return pl.pallas_call(
        _in_kernel,
        grid=(NT, PROJ_W // TN_IN),
        in_specs=[
            pl.BlockSpec((TM, D), lambda i, j: (i, 0)),
            pl.BlockSpec((1, 1, D), lambda i, j: (layer, 0, 0)),
            *_mod_specs(SH1, 2)[:1], *_mod_specs(SC1, 2)[:1],
            *_mod_specs(SH1, 2)[1:], *_mod_specs(SC1, 2)[1:],
            pl.BlockSpec((1, D, TN_IN), lambda i, j: (layer, 0, j)),
        ],
        out_specs=pl.BlockSpec((TM, TN_IN), lambda i, j: (i, j)),
        out_shape=jax.ShapeDtypeStruct((T_ALL, PROJ_W), f32),
        scratch_shapes=[pltpu.VMEM((TM, D), bf16), pltpu.VMEM((TM, D), f32)],
        compiler_params=_cparams(("parallel", "arbitrary")),
        name=f"in_proj_{layer}",
    )(x, g1, modp, modp, mods, mods, w_cat)


def _tile_masks(blk):
    ri = lax.broadcasted_iota(jnp.int32, (TILE, TILE), 0)
    ci = lax.broadcasted_iota(jnp.int32, (TILE, TILE), 1)
    same = (ri // blk) == (ci // blk)
    incl = jnp.logical_and(same, ri >= ci)
    strict = jnp.logical_and(same, ri > ci)
    return same, incl, strict


def _gate_cumsums(g_all, blk):
    same, incl, _ = _tile_masks(blk)
    gc = _dot_hi(incl.astype(f32), g_all)
    gt = _dot_hi(same.astype(f32), g_all)
    return gc, gt


def _unit_lower_inverse(a_mats, blk, base, nsq):
    ri = lax.broadcasted_iota(jnp.int32, (TILE, TILE), 0)
    ci = lax.broadcasted_iota(jnp.int32, (TILE, TILE), 1)
    ps = [jnp.where((ri // base) == (ci // base), -a, 0.0) for a in a_mats]
    eye = (ri == ci).astype(f32)
    ts = [eye + p for p in ps]
    for _ in range(nsq):
        ps = [_dot(p, p) for p in ps]
        ts = [t + _dot(p, t) for p, t in zip(ps, ts)]
    b = base
    while b < blk:
        off = jnp.logical_and((ri // (2 * b)) == (ci // (2 * b)), (ri // b) != (ci // b))
        ms = [_dot(jnp.where(off, a, 0.0), t) for a, t in zip(a_mats, ts)]
        ts = [t - _dot(t, m) for t, m in zip(ts, ms)]
        b *= 2
    return ts


def _delta_prepare(qs, ks, vs, betas, gcs, grows, gts, blk, nsq):
    _, incl, strict = _tile_masks(blk)
    n = range(len(qs))
    dmats = [gc - gr for gc, gr in zip(gcs, grows)]
    decays = [jnp.where(incl, jnp.exp(jnp.where(incl, d, 0.0)), 0.0) for d in dmats]
    egcs = [jnp.exp(gc) for gc in gcs]
    kbs = [ks[i] * betas[i] for i in n]
    a_mats = [jnp.where(strict, _dot_nt(kbs[i], ks[i]) * decays[i], 0.0) for i in n]
    qks = [_dot_nt(qs[i], ks[i]) * decays[i] for i in n]
    xs = [jnp.concatenate([vs[i] * betas[i], kbs[i] * egcs[i]], axis=1) for i in n]
    tinvs = _unit_lower_inverse(a_mats, blk, SUBLANES, nsq)
    xs = [_dot(t, x) for t, x in zip(tinvs, xs)]
    us = [x[:, :DH] for x in xs]
    ws = [x[:, DH:] for x in xs]
    q_decs = [qs[i] * egcs[i] for i in n]
    k_decs = [ks[i] * jnp.exp(gts[i] - gcs[i]) for i in n]
    return us, ws, q_decs, k_decs, qks


def _gdn_features(xc, ba, prm_ref, valid):
    s = _silu(xc)
    if valid is not None:
        s = jnp.where(valid, s, 0.0)
    qs, ks, vs = [], [], []
    for h in range(NH):
        qh = s[:, h * DH:(h + 1) * DH]
        kh = s[:, GW + h * DH:GW + (h + 1) * DH]
        qs.append(qh * lax.rsqrt(jnp.sum(qh * qh, axis=-1, keepdims=True) + 1e-6) * (DH ** -0.5))
        ks.append(kh * lax.rsqrt(jnp.sum(kh * kh, axis=-1, keepdims=True) + 1e-6))
        vs.append(s[:, 2 * GW + h * DH:2 * GW + (h + 1) * DH])
    beta = jax.nn.sigmoid(ba)
    xs = ba + prm_ref[pl.ds(1, 1), :]
    softplus = jnp.maximum(xs, 0.0) + jnp.log1p(jnp.exp(-jnp.abs(xs)))
    g = -jnp.exp(prm_ref[pl.ds(0, 1), :]) * softplus
    if valid is not None:
        beta = jnp.where(valid, beta, 0.0)
        g = jnp.where(valid, g, 0.0)
    return qs, ks, vs, beta, g


def _gated_out(o, z, gout):
    return _rms(o, gout) * _silu(z)


def _gdn_p_kernel(x_ref, halo_ref, ba_ref, wc_ref, prm_ref, gout_ref,
                  o_ref, s_ref, cst_ref,
                  q_scr, k_scr, v_scr, bg_scr, st_scr):
    j = pl.program_id(1)
    nj = pl.num_programs(1)

    @pl.when(j == 0)
    def _():
        st_scr[...] = jnp.zeros_like(st_scr)

    x = x_ref[:, :QKV]
    halo = jnp.where(j > 0, halo_ref[:, :QKV], 0.0)
    xp = jnp.concatenate([halo, x], axis=0)
    xc = x * wc_ref[0, pl.ds(GCONV - 1, 1), :]
    for s in range(1, GCONV):
        xc = xc + pltpu.roll(xp, s, 0)[SUBLANES:, :] * wc_ref[0, pl.ds(GCONV - 1 - s, 1), :]
    qs, ks, vs, beta, g = _gdn_features(xc, ba_ref[...], prm_ref.at[0], None)
    for h in range(NH):
        q_scr[:, h * DH:(h + 1) * DH] = qs[h]
        k_scr[:, h * DH:(h + 1) * DH] = ks[h]
        v_scr[:, h * DH:(h + 1) * DH] = vs[h]
    bg_scr[0] = beta
    bg_scr[1] = g

    nsq = int(math.log2(SUBLANES)) - 1
    nblk = TILE // GCHUNK

    def tile_body(t, carry):
        hs = range(NH)
        cols = [slice(h * DH, (h + 1) * DH) for h in hs]
        rows, qs, ks, vs, betas, gcs, gts, grows = [], [], [], [], [], [], [], []
        for ti in range(TILES_PER_ITER):
            r = pl.ds(pl.multiple_of((t * TILES_PER_ITER + ti) * TILE, TILE), TILE)
            rows.append(r)
            beta_t = bg_scr[0, r, :]
            gc_t, gt_t = _gate_cumsums(bg_scr[1, r, :], GCHUNK)
            gc_tt = gc_t.T
            for h in hs:
                qs.append(q_scr[r, cols[h]])
                ks.append(k_scr[r, cols[h]])
                vs.append(v_scr[r, cols[h]])
                betas.append(beta_t[:, h:h + 1])
                gcs.append(gc_t[:, NH + h:NH + h + 1])
                gts.append(gt_t[:, NH + h:NH + h + 1])
                grows.append(gc_tt[NH + h:NH + h + 1, :])
        us, ws, q_decs, k_decs, qks = _delta_prepare(qs, ks, vs, betas, gcs, grows, gts, GCHUNK, nsq)
        s_cur = [st_scr[h] for h in hs]
        for ti in range(TILES_PER_ITER):
            ch = [ti * NH + h for h in hs]
            egts = [jnp.exp(gts[i]) for i in ch]
            vn = [[] for _ in hs]
            qs_ = [[] for _ in hs]
            for c in range(nblk):
                rr = slice(c * GCHUNK, (c + 1) * GCHUNK)
                wss = [_dot(jnp.concatenate([ws[ch[h]][rr], q_decs[ch[h]][rr]], axis=0), s_cur[h]) for h in hs]
                for h in hs:
                    vn[h].append(us[ch[h]][rr] - wss[h][:GCHUNK])
                    qs_[h].append(wss[h][GCHUNK:])
                s_cur = [s_cur[h] * egts[h][c * GCHUNK:c * GCHUNK + 1, :] + _dot_tn(k_decs[ch[h]][rr], vn[h][c])
                         for h in hs]
            os_ = [jnp.concatenate(qs_[h], axis=0) + _dot(qks[ch[h]], jnp.concatenate(vn[h], axis=0)) for h in hs]
            for h in hs:
                z = x_ref[rows[ti], pl.ds(QKV + h * DH, DH)]
                o_ref[rows[ti], cols[h]] = _gated_out(os_[h], z, gout_ref[0]).astype(o_ref.dtype)
        for h in hs:
            st_scr[h] = s_cur[h]
        return carry

    lax.fori_loop(0, LB_G // (TILE * TILES_PER_ITER), tile_body, 0)

    @pl.when(j == nj - 1)
    def _():
        s_ref[0] = st_scr[...]
        cst_ref[0] = x_ref[pl.ds(LB_G - SUBLANES, SUBLANES), :QKV]


def _gdn_prompt(layer, proj, w_gconv, gprm, g_gout):
    nj = L_P // LB_G
    return pl.pallas_call(
        _gdn_p_kernel,
        grid=(NB_P, nj),
        in_specs=[
            pl.BlockSpec((LB_G, COL_GLU), lambda b, j: (b * nj + j, 0)),
            pl.BlockSpec((SUBLANES, COL_GLU),
                         lambda b, j: (jnp.maximum((b * nj + j) * (LB_G // SUBLANES) - 1, 0), 0)),
            pl.BlockSpec((LB_G, LANES), lambda b, j: (b * nj + j, COL_BA // LANES)),
            pl.BlockSpec((1, GCONV, QKV), lambda b, j: (layer, 0, 0)),
            pl.BlockSpec((1, SUBLANES, LANES), lambda b, j: (layer, 0, 0)),
            pl.BlockSpec((1, 1, DH), lambda b, j: (layer, 0, 0)),
        ],
        out_specs=[
            pl.BlockSpec((LB_G, GW), lambda b, j: (b * nj + j, 0)),
            pl.BlockSpec((1, NH, DH, DH), lambda b, j: (b, 0, 0, 0)),
            pl.BlockSpec((1, SUBLANES, QKV), lambda b, j: (b, 0, 0)),
        ],
        out_shape=[
            jax.ShapeDtypeStruct((T_P, GW), bf16),
            jax.ShapeDtypeStruct((NB_P, NH, DH, DH), f32),
            jax.ShapeDtypeStruct((NB_P, SUBLANES, QKV), f32),
        ],
        scratch_shapes=[pltpu.VMEM((LB_G, GW), f32), pltpu.VMEM((LB_G, GW), f32),
                        pltpu.VMEM((LB_G, GW), f32), pltpu.VMEM((2, LB_G, LANES), f32),
                        pltpu.VMEM((NH, DH, DH), f32)],
        compiler_params=_cparams(("parallel", "arbitrary")),
        name=f"gdn_prompt_{layer}",
    )(proj, proj, proj, w_gconv, gprm, g_gout)


BT_S = TILE // TPAD


def _gdn_s_kernel(x_ref, ba_ref, cin_ref, s0_ref, wc_ref, prm_ref, gout_ref,
                  o_ref, s_ref, cst_ref):
    x = x_ref[:, :QKV]
    st = cin_ref[...]
    trow = lax.broadcasted_iota(jnp.int32, (TILE, 1), 0) % TPAD
    xc = x * wc_ref[0, pl.ds(GCONV - 1, 1), :]
    for s in range(1, GCONV):
        xs = jnp.where(trow >= s, pltpu.roll(x, s, 0), pltpu.roll(st, TILE - TPAD + s, 0))
        xc = xc + xs * wc_ref[0, pl.ds(GCONV - 1 - s, 1), :]
    valid = trow < L_S
    qs, ks, vs, beta, g = _gdn_features(xc, ba_ref[...], prm_ref.at[0], valid)
    cst_ref[...] = pltpu.roll(x, TPAD - (GCONV - 1) - (L_S - (GCONV - 1)), 0)

    nsq = int(math.ceil(math.log2(L_S))) - 1
    gc_t, gt_t = _gate_cumsums(g, TPAD)
    hs = range(NH)
    gcs = [gc_t[:, NH + h:NH + h + 1] for h in hs]
    gts = [gt_t[:, NH + h:NH + h + 1] for h in hs]
    gc_tt = gc_t.T
    grows = [gc_tt[NH + h:NH + h + 1, :] for h in hs]
    us, ws, q_decs, k_decs, qks = _delta_prepare(qs, ks, vs, [beta[:, h:h + 1] for h in hs], gcs, grows, gts,
                                                 TPAD, nsq)
    for h in hs:
        egt = jnp.exp(gts[h])
        vn, qs_ = [], []
        for b in range(BT_S):
            rr = slice(b * TPAD, (b + 1) * TPAD)
            s0 = s0_ref[b, h]
            wsb = _dot(jnp.concatenate([ws[h][rr], q_decs[h][rr]], axis=0), s0)
            vn_b = us[h][rr] - wsb[:TPAD]
            vn.append(vn_b)
            qs_.append(wsb[TPAD:])
            s_ref[b, h] = s0 * egt[b * TPAD:b * TPAD + 1, :] + _dot_tn(k_decs[h][rr], vn_b)
        o = jnp.concatenate(qs_, axis=0) + _dot(qks[h], jnp.concatenate(vn, axis=0))
        z = x_ref[:, pl.ds(QKV + h * DH, DH)]
        o_ref[:, h * DH:(h + 1) * DH] = _gated_out(o, z, gout_ref[0]).astype(o_ref.dtype)


def _gdn_sample(layer, proj, conv_in, s0, w_gconv, gprm, g_gout):
    row0 = T_P // TILE
    return pl.pallas_call(
        _gdn_s_kernel,
        grid=(T_S // TILE,),
        in_specs=[
            pl.BlockSpec((TILE, COL_GLU), lambda i: (row0 + i, 0)),
            pl.BlockSpec((TILE, LANES), lambda i: (row0 + i, COL_BA // LANES)),
            pl.BlockSpec((TILE, QKV), lambda i: (i, 0)),
            pl.BlockSpec((BT_S, NH, DH, DH), lambda i: (i, 0, 0, 0)),
            pl.BlockSpec((1, GCONV, QKV), lambda i: (layer, 0, 0)),
            pl.BlockSpec((1, SUBLANES, LANES), lambda i: (layer, 0, 0)),
            pl.BlockSpec((1, 1, DH), lambda i: (layer, 0, 0)),
        ],
        out_specs=[
            pl.BlockSpec((TILE, GW), lambda i: (i, 0)),
            pl.BlockSpec((BT_S, NH, DH, DH), lambda i: (i, 0, 0, 0)),
            pl.BlockSpec((TILE, QKV), lambda i: (i, 0)),
        ],
        out_shape=[
            jax.ShapeDtypeStruct((T_S, GW), bf16),
            jax.ShapeDtypeStruct((NB_S, NH, DH, DH), f32),
            jax.ShapeDtypeStruct((T_S, QKV), f32),
        ],
        compiler_params=_cparams(("parallel",)),
        name=f"gdn_sample_{layer}",
    )(proj, proj, conv_in, s0, w_gconv, gprm, g_gout)


def _group_ln_silu(y, gl_ref, bl_ref):
    outs = []
    for gi in range(CGROUPS):
        w = CW // CGROUPS
        yg = y[:, gi * w:(gi + 1) * w]
        mu = jnp.mean(yg, axis=-1, keepdims=True)
        d = yg - mu
        var = jnp.mean(d * d, axis=-1, keepdims=True)
        outs.append(d * lax.rsqrt(var + LN_EPS))
    yn = jnp.concatenate(outs, axis=1) * gl_ref[0] + bl_ref[0]
    return _silu(yn)


def _conf_p_kernel(x_ref, halo_ref, w_ref, b_ref, gl_ref, bl_ref, o_ref, hst_ref, xp_scr):
    j = pl.program_id(1)
    nj = pl.num_programs(1)
    h = x_ref[:, :CW] * jax.nn.sigmoid(x_ref[:, CW:])
    hh = halo_ref[:, :CW] * jax.nn.sigmoid(halo_ref[:, CW:])
    xp_scr[pl.ds(0, HIST_C), :] = jnp.where(j > 0, hh, 0.0)
    xp_scr[pl.ds(HIST_C, LB_C), :] = h
    off = HIST_C - (CK - 1)
    y = xp_scr[pl.ds(off, LB_C), :] * w_ref[0, pl.ds(0, 1), :]
    for t in range(1, CK):
        y = y + xp_scr[pl.ds(off + t, LB_C), :] * w_ref[0, pl.ds(t, 1), :]
    y = y + b_ref[0]
    o_ref[...] = _group_ln_silu(y, gl_ref, bl_ref).astype(o_ref.dtype)

    @pl.when(j == nj - 1)
    def _():
        hst_ref[0] = h[LB_C - HIST_C:, :]


def _conf_prompt(layer, proj, w_dw, b_dw, g_ln, b_ln):
    nj = L_P // LB_C
    cb = COL_GLU // (2 * CW)
    return pl.pallas_call(
        _conf_p_kernel,
        grid=(NB_P, nj),
        in_specs=[
            pl.BlockSpec((LB_C, 2 * CW), lambda b, j: (b * nj + j, cb)),
            pl.BlockSpec((HIST_C, 2 * CW),
                         lambda b, j: (jnp.maximum((b * nj + j) * (LB_C // HIST_C) - 1, 0), cb)),
            pl.BlockSpec((1, CK, CW), lambda b, j: (layer, 0, 0)),
            pl.BlockSpec((1, 1, CW), lambda b, j: (layer, 0, 0)),
            pl.BlockSpec((1, 1, CW), lambda b, j: (layer, 0, 0)),
            pl.BlockSpec((1, 1, CW), lambda b, j: (layer, 0, 0)),
        ],
        out_specs=[
            pl.BlockSpec((LB_C, CW), lambda b, j: (b * nj + j, 0)),
            pl.BlockSpec((1, HIST_C, CW), lambda b, j: (b, 0, 0)),
        ],
        out_shape=[
            jax.ShapeDtypeStruct((T_P, CW), bf16),
            jax.ShapeDtypeStruct((NB_P, HIST_C, CW), f32),
        ],
        scratch_shapes=[pltpu.VMEM((HIST_C + LB_C, CW), f32)],
        compiler_params=_cparams(("parallel", "arbitrary")),
        name=f"conf_prompt_{layer}",
    )(proj, proj, w_dw, b_dw, g_ln, b_ln)


BT_C = 16


def _conf_s_kernel(x_ref, hin_ref, wsh_ref, b_ref, gl_ref, bl_ref, o_ref, hst_ref, y_scr):
    h = x_ref[:, :CW] * jax.nn.sigmoid(x_ref[:, CW:])
    trow = lax.broadcasted_iota(jnp.int32, (TPAD, 1), 0)
    for b in range(BT_C):
        xp = jnp.concatenate([hin_ref[pl.ds(b * HIST_C, HIST_C), :], h[b * TPAD:(b + 1) * TPAD, :]], axis=0)
        y8 = jnp.zeros((TPAD, CW), f32)
        for t in range(L_S):
            yt = jnp.sum(xp * wsh_ref[0, t], axis=0, keepdims=True)
            y8 = jnp.where(trow == t, yt, y8)
        y_scr[pl.ds(b * TPAD, TPAD), :] = y8
        hst_ref[pl.ds(b * HIST_C, HIST_C), :] = pltpu.roll(xp, HIST_C + TPAD - L_S, 0)[:HIST_C, :]
    y = y_scr[...] + b_ref[0]
    o_ref[...] = _group_ln_silu(y, gl_ref, bl_ref).astype(o_ref.dtype)


def _conf_sample(layer, proj, hist_in, w_shift, b_dw, g_ln, b_ln):
    row0 = T_P // (BT_C * TPAD)
    cb = COL_GLU // (2 * CW)
    return pl.pallas_call(
        _conf_s_kernel,
        grid=(NB_S // BT_C,),
        in_specs=[
            pl.BlockSpec((BT_C * TPAD, 2 * CW), lambda i: (row0 + i, cb)),
            pl.BlockSpec((BT_C * HIST_C, CW), lambda i: (i, 0)),
            pl.BlockSpec((1, L_S, HIST_C + TPAD, CW), lambda i: (layer, 0, 0, 0)),
            pl.BlockSpec((1, 1, CW), lambda i: (layer, 0, 0)),
            pl.BlockSpec((1, 1, CW), lambda i: (layer, 0, 0)),
            pl.BlockSpec((1, 1, CW), lambda i: (layer, 0, 0)),
        ],
        out_specs=[
            pl.BlockSpec((BT_C * TPAD, CW), lambda i: (i, 0)),
            pl.BlockSpec((BT_C * HIST_C, CW), lambda i: (i, 0)),
        ],
        out_shape=[
            jax.ShapeDtypeStruct((T_S, CW), bf16),
            jax.ShapeDtypeStruct((NB_S * HIST_C, CW), f32),
        ],
        scratch_shapes=[pltpu.VMEM((BT_C * TPAD, CW), f32)],
        compiler_params=_cparams(("parallel",)),
        name=f"conf_sample_{layer}",
    )(proj, hist_in, w_shift, b_dw, g_ln, b_ln)


def _out_kernel(moe, x_ref, oap, obp, oas, obs, w_ref, g_ref,
                g1p, scp, shp, g1s, scs, shs, *rest):
    if moe:
        wr_ref, br_ref, xo_ref, h2_ref, rt_ref = rest
        hf_scr = h2_ref
    else:
        xo_ref, h2_ref, hf_scr = rest
    i = pl.program_id(0)

    def attn(oa, ob):
        return (jnp.dot(oa[...], w_ref[0, :GW, :], preferred_element_type=f32)
                + jnp.dot(ob[...], w_ref[0, GW:, :], preferred_element_type=f32))

    @pl.when(i < NT_P)
    def _():
        xn = x_ref[...] + g1p[0] * attn(oap, obp)
        xo_ref[...] = xn
        hf_scr[...] = _rms(xn, g_ref[0]) * (1.0 + scp[0]) + shp[0]

    @pl.when(i >= NT_P)
    def _():
        hf_scr[...] = attn(oas, obs)

        def body(rows, b):
            m = pl.ds(b, 1)
            xn = x_ref[rows, :] + g1s[m, :] * hf_scr[rows, :]
            xo_ref[rows, :] = xn
            hf_scr[rows, :] = _rms(xn, g_ref[0]) * (1.0 + scs[m, :]) + shs[m, :]
        _sample_rows(body)

    hf = hf_scr[...]
    if not moe:
        h2_ref[...] = hf.astype(h2_ref.dtype)
    if moe:
        logits = _dot_hi(hf, wr_ref[0]) + br_ref[0]
        lane = lax.broadcasted_iota(jnp.int32, logits.shape, 1)
        ex = jnp.exp(logits - jnp.max(logits, axis=-1, keepdims=True))
        probs = ex / jnp.sum(ex, axis=-1, keepdims=True)
        m1 = jnp.max(probs, axis=-1, keepdims=True)
        i1 = jnp.min(jnp.where(probs == m1, lane, LANES), axis=-1, keepdims=True)
        rest_p = jnp.where(lane == i1, -1.0, probs)
        m2 = jnp.max(rest_p, axis=-1, keepdims=True)
        i2 = jnp.min(jnp.where(rest_p == m2, lane, LANES), axis=-1, keepdims=True)
        den = m1 + m2
        rt_ref[...] = jnp.where(lane == 0, i1.astype(f32),
                                jnp.where(lane == 1, i2.astype(f32),
                                          jnp.where(lane == 2, m1 / den,
                                                    jnp.where(lane == 3, m2 / den, 0.0))))


def _out_proj(layer, moe, x, oa_p, ob_p, oa_s, ob_s, w_out, g2, modp, mods, wr=None, br=None):
    idx = layer // 2
    in_specs = [
        pl.BlockSpec((TM, D), lambda i: (i, 0)),
        pl.BlockSpec((TM, GW), lambda i: (jnp.minimum(i, NT_P - 1), 0)),
        pl.BlockSpec((TM, CW), lambda i: (jnp.minimum(i, NT_P - 1), 0)),
        pl.BlockSpec((TM, GW), lambda i: (0, 0)),
        pl.BlockSpec((TM, CW), lambda i: (0, 0)),
        pl.BlockSpec((1, D, D), lambda i: (layer, 0, 0)),
        pl.BlockSpec((1, 1, D), lambda i: (layer, 0, 0)),
        _mod_specs(G1, 1)[0], _mod_specs(SC2, 1)[0], _mod_specs(SH2, 1)[0],
        _mod_specs(G1, 1)[1], _mod_specs(SC2, 1)[1], _mod_specs(SH2, 1)[1],
    ]
    args = [x, oa_p, ob_p, oa_s, ob_s, w_out, g2, modp, modp, modp, mods, mods, mods]
    out_specs = [pl.BlockSpec((TM, D), lambda i: (i, 0)), pl.BlockSpec((TM, D), lambda i: (i, 0))]
    out_shape = [jax.ShapeDtypeStruct((T_ALL, D), f32),
                 jax.ShapeDtypeStruct((T_ALL, D), f32 if moe else bf16)]
    if moe:
        in_specs += [pl.BlockSpec((1, D, LANES), lambda i: (idx, 0, 0)),
                     pl.BlockSpec((1, 1, LANES), lambda i: (idx, 0, 0))]
        args += [wr, br]
        out_specs.append(pl.BlockSpec((TM, LANES), lambda i: (i, 0)))
        out_shape.append(jax.ShapeDtypeStruct((T_ALL, LANES), f32))
    return pl.pallas_call(
        functools.partial(_out_kernel, moe),
        grid=(NT,),
        in_specs=in_specs,
        out_specs=out_specs,
        out_shape=out_shape,
        scratch_shapes=[] if moe else [pltpu.VMEM((TM, D), f32)],
        compiler_params=_cparams(("parallel",)),
        name=f"out_proj_{layer}",
    )(*args)


def _residual(i, x_ref, f_ref, g2p, g2s, o_ref, gf_ref):
    def fin(v):
        return v if gf_ref is None else _rms(v, gf_ref[...])

    @pl.when(i < NT_P)
    def _():
        o_ref[...] = fin(x_ref[...] + g2p[0] * f_ref[...])

    @pl.when(i >= NT_P)
    def _():
        def body(rows, b):
            o_ref[rows, :] = fin(x_ref[rows, :] + g2s[pl.ds(b, 1), :] * f_ref[rows, :])
        _sample_rows(body)


def _ffn_kernel(h_ref, wg_ref, wu_ref, wd_ref, x_ref, g2p, g2s, o_ref, acc_scr):
    i = pl.program_id(0)
    f = pl.program_id(1)
    nf = pl.num_programs(1)
    h = h_ref[...]
    a = _silu(_dot(h, wg_ref[0])) * _dot(h, wu_ref[0])
    part = _dot(a, wd_ref[0])

    @pl.when(f == 0)
    def _():
        acc_scr[...] = part

    @pl.when(f > 0)
    def _():
        acc_scr[...] += part

    @pl.when(f == nf - 1)
    def _():
        _residual(i, x_ref, acc_scr, g2p, g2s, o_ref, None)


def _ffn_dense(layer, h2, x, wg, wu, wd, modp, mods):
    idx = layer // 2
    mp, ms = _mod_specs(G2, 2)
    return pl.pallas_call(
        _ffn_kernel,
        grid=(NT, FF_D // TF_D),
        in_specs=[
            pl.BlockSpec((TM, D), lambda i, f: (i, 0)),
            pl.BlockSpec((1, D, TF_D), lambda i, f: (idx, 0, f)),
            pl.BlockSpec((1, D, TF_D), lambda i, f: (idx, 0, f)),
            pl.BlockSpec((1, TF_D, D), lambda i, f: (idx, f, 0)),
            pl.BlockSpec((TM, D), lambda i, f: (i, 0)),
            mp, ms,
        ],
        out_specs=pl.BlockSpec((TM, D), lambda i, f: (i, 0)),
        out_shape=jax.ShapeDtypeStruct((T_ALL, D), f32),
        scratch_shapes=[pltpu.VMEM((TM, D), f32)],
        compiler_params=_cparams(("parallel", "arbitrary")),
        name=f"ffn_dense_{layer}",
    )(h2, wg, wu, wd, x, modp, mods)


def _gather_rows(idx_ref, base, stride, src_hbm, dst, sem):
    def issue(r, carry):
        row = idx_ref[base + stride * r]
        pltpu.make_async_copy(src_hbm.at[pl.ds(row, 1), :], dst.at[pl.ds(r, 1), :], sem).start()
        return carry
    lax.fori_loop(0, MOE_BLK, issue, 0, unroll=8)


def _wait_rows(src_hbm, dst, sem):
    pltpu.make_async_copy(src_hbm.at[pl.ds(0, MOE_BLK), :], dst, sem).wait()


def _expert_kernel(be_ref, tok_ref, nv_ref, h_hbm, wg_ref, wu_ref, wd_ref,
                   y_ref, xg_scr, xb_scr, sem):
    m = pl.program_id(0)
    f = pl.program_id(1)
    nv = nv_ref[0]
    active = m < nv
    slot = m % 2

    @pl.when(jnp.logical_and(f == 0, m == 0))
    def _():
        _gather_rows(tok_ref, 0, 1, h_hbm, xg_scr.at[0], sem.at[0])

    @pl.when(jnp.logical_and(f == 0, active))
    def _():
        _wait_rows(h_hbm, xg_scr.at[slot], sem.at[slot])
        xb_scr[...] = xg_scr[slot].astype(bf16)

        @pl.when(m + 1 < nv)
        def _():
            _gather_rows(tok_ref, (m + 1) * MOE_BLK, 1, h_hbm, xg_scr.at[1 - slot], sem.at[1 - slot])

    @pl.when(f == 0)
    def _():
        y_ref[...] = jnp.zeros_like(y_ref)

    @pl.when(active)
    def _():
        xb = xb_scr[...]
        a = _silu(_dot(xb, wg_ref[0, 0])) * _dot(xb, wu_ref[0, 0])
        y_ref[...] += _dot(a, wd_ref[0, 0])


def _experts(idx, h2, block_e, slot_tok, n_valid, wg, wu, wd):
    nf = FF_E // TF_E

    def wmap_up(m, f, be, tok, nv):
        return (idx, be[m], 0, jnp.where(m < nv[0], f, nf - 1))

    def wmap_down(m, f, be, tok, nv):
        return (idx, be[m], jnp.where(m < nv[0], f, nf - 1), 0)

    grid_spec = pltpu.PrefetchScalarGridSpec(
        num_scalar_prefetch=3,
        grid=(N_BLK, nf),
        in_specs=[
            pl.BlockSpec(memory_space=pl.ANY),
            pl.BlockSpec((1, 1, D, TF_E), wmap_up),
            pl.BlockSpec((1, 1, D, TF_E), wmap_up),
            pl.BlockSpec((1, 1, TF_E, D), wmap_down),
        ],
        out_specs=pl.BlockSpec((MOE_BLK, D), lambda m, f, be, tok, nv: (m, 0)),
        scratch_shapes=[pltpu.VMEM((2, MOE_BLK, D), f32), pltpu.VMEM((MOE_BLK, D), bf16),
                        pltpu.SemaphoreType.DMA((2,))],
    )
    return pl.pallas_call(
        _expert_kernel,
        grid_spec=grid_spec,
        out_shape=jax.ShapeDtypeStruct((N_BLK * MOE_BLK, D), f32),
        compiler_params=_cparams(("arbitrary", "arbitrary")),
        name=f"experts_{idx}",
    )(block_e, slot_tok, n_valid, h2, wg, wu, wd)


def _combine_kernel(final, dest_ref, y_hbm, x_ref, rt_ref, g2p, g2s, *rest):
    if final:
        gf_ref, o_ref, g_scr, f_scr, sem = rest
    else:
        o_ref, g_scr, f_scr, sem = rest
        gf_ref = None
    i = pl.program_id(0)
    slot = i % 2

    def gather(tile, sl):
        for k in range(2):
            _gather_rows(dest_ref, 2 * tile * TM + k, 2, y_hbm, g_scr.at[sl, k], sem.at[sl, k])

    @pl.when(i == 0)
    def _():
        gather(0, 0)

    @pl.when(i + 1 < pl.num_programs(0))
    def _():
        gather(i + 1, 1 - slot)

    for k in range(2):
        _wait_rows(y_hbm, g_scr.at[slot, k], sem.at[slot, k])
    f_scr[...] = rt_ref[:, 2:3] * g_scr[slot, 0] + rt_ref[:, 3:4] * g_scr[slot, 1]
    _residual(i, x_ref, f_scr, g2p, g2s, o_ref, gf_ref)


def _combine(final, dest, yb, x, rt, modp, mods, g_final):
    mp, ms = _mod_specs(G2, 1)
    wrap = lambda spec: pl.BlockSpec(spec.block_shape, lambda i, d, _f=spec.index_map: _f(i))
    in_specs = [pl.BlockSpec(memory_space=pl.ANY),
                pl.BlockSpec((TM, D), lambda i, d: (i, 0)),
                pl.BlockSpec((TM, LANES), lambda i, d: (i, 0)), wrap(mp), wrap(ms)]
    args = [yb, x, rt, modp, mods]
    if final:
        in_specs.append(pl.BlockSpec((1, D), lambda i, d: (0, 0)))
        args.append(g_final)
    grid_spec = pltpu.PrefetchScalarGridSpec(
        num_scalar_prefetch=1,
        grid=(NT,),
        in_specs=in_specs,
        out_specs=pl.BlockSpec((TM, D), lambda i, d: (i, 0)),
        scratch_shapes=[pltpu.VMEM((2, 2, TM, D), f32), pltpu.VMEM((TM, D), f32),
                        pltpu.SemaphoreType.DMA((2, 2))],
    )
    return pl.pallas_call(
        functools.partial(_combine_kernel, final),
        grid_spec=grid_spec,
        out_shape=jax.ShapeDtypeStruct((T_ALL, D), f32),
        compiler_params=_cparams(("arbitrary",)),
        name="combine_final" if final else "combine",
    )(dest, *args)


def _route(rt):
    e = rt[:, :2].astype(jnp.int32).reshape(N_ASG)
    onehot = (e[:, None] == jnp.arange(NE, dtype=jnp.int32)[None, :]).astype(jnp.int32)
    csum = jnp.cumsum(onehot, axis=0)
    counts = csum[-1]
    rank = jnp.sum(onehot * (csum - 1), axis=1)
    padded = (counts + MOE_BLK - 1) // MOE_BLK * MOE_BLK
    pad_end = jnp.cumsum(padded)
    pad_start = pad_end - padded
    dest = (jnp.sum(onehot * pad_start[None, :], axis=1) + rank).astype(jnp.int32)
    n_slots = N_BLK * MOE_BLK
    tok = jnp.arange(N_ASG, dtype=jnp.int32) // 2
    slot_tok = jnp.zeros((n_slots,), jnp.int32).at[dest].set(tok, unique_indices=True)
    n_valid = (pad_end[-1] // MOE_BLK).astype(jnp.int32)
    blk = jnp.minimum(jnp.arange(N_BLK, dtype=jnp.int32), n_valid - 1)
    block_e = jnp.minimum(jnp.searchsorted(pad_end, blk * MOE_BLK, side="right"), NE - 1).astype(jnp.int32)
    return dest, slot_tok, block_e, n_valid.reshape(1)


def kernel(x_prompt, x_sample, c_prompt, c_sample, state_gdn, state_gdn_conv, state_conf_conv, w_ada, b_ada, g_norm1, g_norm2, w_in, w_gdn_conv, a_log, dt_bias, g_gdn_out, w_conf_dw, b_conf_dw, g_conf_ln, b_conf_ln, w_out, w_ff_gate, w_ff_up, w_ff_down, w_router, b_router, w_exp_gate, w_exp_up, w_exp_down, g_final):
    xs_pad = jnp.pad(x_sample, ((0, 0), (0, TPAD - L_S), (0, 0)))
    x = jnp.concatenate([x_prompt.reshape(T_P, D), xs_pad.reshape(T_S, D)], axis=0)
    c_all = jnp.concatenate([c_prompt, c_sample], axis=0)
    o1, o2, o4 = QKV + GW, QKV + GW + 2 * NH, QKV + GW + 2 * NH + 2 * CW
    w_cat = jnp.concatenate([w_in[:, :, :o1], w_in[:, :, o2:o4], w_in[:, :, o1:o2],
                             jnp.zeros((DEPTH, D, PROJ_W - o4), f32)], axis=-1)
    lane_pad = ((0, 0), (NH, LANES - 2 * NH))
    gprm = jnp.stack([jnp.pad(a_log, lane_pad), jnp.pad(dt_bias, lane_pad)], axis=1)
    gprm = jnp.pad(gprm, ((0, 0), (0, SUBLANES - 2), (0, 0)))
    wr_pad = jnp.pad(w_router, ((0, 0), (0, 0), (0, LANES - NE)))
    br_pad = jnp.pad(b_router, ((0, 0), (0, LANES - NE)), constant_values=-1e30).reshape(-1, 1, LANES)
    w_shift = jnp.stack([jnp.pad(w_conf_dw, ((0, 0), (t + HIST_C - (CK - 1), TPAD - 1 - t), (0, 0)))
                         for t in range(L_S)], axis=1)
    gconv_in = jnp.pad(state_gdn_conv, ((0, 0), (0, 0), (TPAD - (GCONV - 1), 0), (0, 0))).reshape(DEPTH, T_S, QKV)
    conf_in = jnp.pad(state_conf_conv, ((0, 0), (0, 0), (HIST_C - (CK - 1), 0), (0, 0))).reshape(DEPTH, NB_S * HIST_C, CW)
    w_out = w_out.astype(bf16)
    g1 = g_norm1.reshape(DEPTH, 1, D)
    g2 = g_norm2.reshape(DEPTH, 1, D)
    gout = g_gdn_out.reshape(DEPTH, 1, DH)
    b_dw = b_conf_dw.reshape(DEPTH, 1, CW)
    g_ln = g_conf_ln.reshape(DEPTH, 1, CW)
    b_ln = b_conf_ln.reshape(DEPTH, 1, CW)

    mod = _ada(c_all, w_ada, b_ada)

    sp_l, cp_l, fp_l, ss_l, cs_l, fs_l = [], [], [], [], [], []
    for layer in range(DEPTH):
        modp = mod[layer, :NB_P].reshape(NB_P, 1, N_MOD * D)
        mods = mod[layer, NB_P:]
        proj = _in_proj(layer, x, g1, modp, mods, w_cat)
        oa_p, s_p, c_p = _gdn_prompt(layer, proj, w_gdn_conv, gprm, gout)
        oa_s, s_s, c_s = _gdn_sample(layer, proj, gconv_in[layer], state_gdn[layer], w_gdn_conv, gprm, gout)
        ob_p, f_p = _conf_prompt(layer, proj, w_conf_dw, b_dw, g_ln, b_ln)
        ob_s, f_s = _conf_sample(layer, proj, conf_in[layer], w_shift, b_dw, g_ln, b_ln)
        sp_l.append(s_p)
        cp_l.append(c_p[:, TPAD - (GCONV - 1):, :])
        fp_l.append(f_p[:, HIST_C - (CK - 1):, :])
        ss_l.append(s_s)
        cs_l.append(c_s.reshape(NB_S, TPAD, QKV)[:, TPAD - (GCONV - 1):, :])
        fs_l.append(f_s.reshape(NB_S, HIST_C, CW)[:, HIST_C - (CK - 1):, :])
        if layer % 2 == 0:
            x, h2 = _out_proj(layer, False, x, oa_p, ob_p, oa_s, ob_s, w_out, g2, modp, mods)
            x = _ffn_dense(layer, h2, x, w_ff_gate, w_ff_up, w_ff_down, modp, mods)
        else:
            x, h2, rt = _out_proj(layer, True, x, oa_p, ob_p, oa_s, ob_s, w_out, g2, modp, mods, wr_pad, br_pad)
            dest, slot_tok, block_e, n_valid = _route(rt)
            yb = _experts(layer // 2, h2, block_e, slot_tok, n_valid, w_exp_gate, w_exp_up, w_exp_down)
            final = layer == DEPTH - 1
            x = _combine(final, dest, yb, x, rt, modp, mods, g_final.reshape(1, D))

    y_prompt = x[:T_P].reshape(NB_P, L_P, D)
    y_sample = x[T_P:].reshape(NB_S, TPAD, D)[:, :L_S, :]
    return (y_prompt, y_sample, jnp.stack(sp_l), jnp.stack(cp_l), jnp.stack(fp_l),
            jnp.stack(ss_l), jnp.stack(cs_l), jnp.stack(fs_l))
```

```python
import functools
import math

import jax
import jax.numpy as jnp
from jax import lax
from jax.experimental import pallas as pl
from jax.experimental.pallas import tpu as pltpu

f32 = jnp.float32
bf16 = jnp.bfloat16
HIGHEST = lax.Precision.HIGHEST

D = 1024
NB_P, L_P = 8, 2048
NB_S, L_S = 128, 4
DEPTH = 4
NH, DH = 4, 128
GW = NH * DH
QKV = 3 * GW
GCONV = 4
GCHUNK = 64
CW = D - GW
CGROUPS = 4
CK = 31
FF_D = 2816
NE = 8
FF_E = 3584
N_MOD = 6
RMS_EPS = 1e-6
LN_EPS = 1e-5

SUBLANES = 8
LANES = 128
VMEM_LIMIT = 56 * 1024 * 1024

TPAD = SUBLANES
T_P = NB_P * L_P
T_S = NB_S * TPAD
T_ALL = T_P + T_S
TM = 1024
NT = T_ALL // TM
NT_P = T_P // TM
TILES_PER_SEQ = L_P // TM
PROJ_W = 3200
COL_GLU = 2048
COL_BA = 3072
TN_IN = 640
HIST_C = 32
SH1, SC1, G1, SH2, SC2, G2 = range(N_MOD)

LB_G = 512
LB_C = 512
TILE = 128
TILES_PER_ITER = 4
TF_D = 1408
TF_E = 512
MOE_BLK = 1024
N_ASG = 2 * T_ALL
N_BLK = N_ASG // MOE_BLK + NE


def _cparams(sem):
    return pltpu.CompilerParams(dimension_semantics=sem, vmem_limit_bytes=VMEM_LIMIT)


def _dot(a, b):
    return jnp.dot(a.astype(bf16), b.astype(bf16), preferred_element_type=f32)


def _dot_hi(a, b):
    return jnp.dot(a, b, precision=HIGHEST, preferred_element_type=f32)


def _dot_nt(a, b):
    return lax.dot_general(a.astype(bf16), b.astype(bf16), (((1,), (1,)), ((), ())),
                           preferred_element_type=f32)


def _dot_nt_hi(a, b):
    return lax.dot_general(a, b, (((1,), (1,)), ((), ())), precision=HIGHEST,
                           preferred_element_type=f32)


def _dot_tn(a, b):
    return lax.dot_general(a, b, (((0,), (0,)), ((), ())), preferred_element_type=f32)


def _silu(x):
    return x * jax.nn.sigmoid(x)


def _rms(x, g):
    return x * lax.rsqrt(jnp.mean(x * x, axis=-1, keepdims=True) + RMS_EPS) * g


def _ada_kernel(c_ref, w_ref, b_ref, o_ref):
    a = _silu(c_ref[...])
    o_ref[0] = _dot(a, w_ref[0]) + b_ref[0]


def _ada(c_all, w_ada, b_ada):
    n = c_all.shape[0]
    tn = 1536
    return pl.pallas_call(
        _ada_kernel,
        grid=(DEPTH, N_MOD * D // tn),
        in_specs=[
            pl.BlockSpec((n, D), lambda l, j: (0, 0)),
            pl.BlockSpec((1, D, tn), lambda l, j: (l, 0, j)),
            pl.BlockSpec((1, 1, tn), lambda l, j: (l, 0, j)),
        ],
        out_specs=pl.BlockSpec((1, n, tn), lambda l, j: (l, 0, j)),
        out_shape=jax.ShapeDtypeStruct((DEPTH, n, N_MOD * D), f32),
        compiler_params=_cparams(("parallel", "parallel")),
        name="ada",
    )(c_all, w_ada, b_ada.reshape(DEPTH, 1, N_MOD * D))


def _mod_specs(k, ngrid):
    if ngrid == 1:
        return [pl.BlockSpec((1, 1, D), lambda i: (jnp.minimum(i // TILES_PER_SEQ, NB_P - 1), 0, k)),
                pl.BlockSpec((NB_S, D), lambda i: (0, k))]
    return [pl.BlockSpec((1, 1, D), lambda i, j: (jnp.minimum(i // TILES_PER_SEQ, NB_P - 1), 0, k)),
            pl.BlockSpec((NB_S, D), lambda i, j: (0, k))]


def _sample_rows(body):
    def step(b, carry):
        body(pl.ds(pl.multiple_of(b * TPAD, TPAD), TPAD), b)
        return carry
    lax.fori_loop(0, NB_S, step, 0)


def _in_kernel(x_ref, g_ref, shp, scp, shs, scs, w_ref, o_ref, h_scr, hf_scr):
    i = pl.program_id(0)

    @pl.when(i < NT_P)
    def _():
        h = _rms(x_ref[...], g_ref[0]) * (1.0 + scp[0]) + shp[0]
        h_scr[...] = h.astype(bf16)

    @pl.when(i >= NT_P)
    def _():
        def body(rows, b):
            hf_scr[rows, :] = (_rms(x_ref[rows, :], g_ref[0]) * (1.0 + scs[pl.ds(b, 1), :])
                               + shs[pl.ds(b, 1), :])
        _sample_rows(body)
        h_scr[...] = hf_scr[...].astype(bf16)

    for c in range(PROJ_W // TN_IN):
        cols = slice(c * TN_IN, (c + 1) * TN_IN)
        o_ref[:, cols] = jnp.dot(h_scr[...], w_ref[0, :, cols], preferred_element_type=f32)


def _resident(block_shape, index_map):
    return pl.BlockSpec(block_shape, index_map, pipeline_mode=pl.Buffered(1))


def _in_proj(layer, x, g1, modp, mods, w_cat):
    return pl.pallas_call(
        _in_kernel,
        grid=(NT,),
        in_specs=[
            pl.BlockSpec((TM, D), lambda i: (i, 0)),
            pl.BlockSpec((1, 1, D), lambda i: (layer, 0, 0)),
            *_mod_specs(SH1, 1)[:1], *_mod_specs(SC1, 1)[:1],
            *_mod_specs(SH1, 1)[1:], *_mod_specs(SC1, 1)[1:],
            _resident((1, D, PROJ_W), lambda i: (layer, 0, 0)),
        ],
        out_specs=pl.BlockSpec((TM, PROJ_W), lambda i: (i, 0)),
        out_shape=jax.ShapeDtypeStruct((T_ALL, PROJ_W), f32),
        scratch_shapes=[pltpu.VMEM((TM, D), bf16), pltpu.VMEM((TM, D), f32)],
        compiler_params=_cparams(("parallel",)),
        name=f"in_proj_{layer}",
    )(x, g1, modp, modp, mods, mods, w_cat)


def _tile_masks(blk):
    ri = lax.broadcasted_iota(jnp.int32, (TILE, TILE), 0)
    ci = lax.broadcasted_iota(jnp.int32, (TILE, TILE), 1)
    same = (ri // blk) == (ci // blk)
    incl = jnp.logical_and(same, ri >= ci)
    strict = jnp.logical_and(same, ri > ci)
    return same, incl, strict


def _gate_cumsums(g_all, blk):
    same, incl, _ = _tile_masks(blk)
    gc = _dot_hi(incl.astype(f32), g_all)
    gt = _dot_hi(same.astype(f32), g_all)
    return gc, gt


def _unit_lower_inverse(a_mats, blk, base, nsq):
    ri = lax.broadcasted_iota(jnp.int32, (TILE, TILE), 0)
    ci = lax.broadcasted_iota(jnp.int32, (TILE, TILE), 1)
    ps = [jnp.where((ri // base) == (ci // base), -a, 0.0) for a in a_mats]
    eye = (ri == ci).astype(f32)
    ts = [eye + p for p in ps]
    for _ in range(nsq):
        ps = [_dot(p, p) for p in ps]
        ts = [t + _dot(p, t) for p, t in zip(ps, ts)]
    b = base
    while b < blk:
        off = jnp.logical_and((ri // (2 * b)) == (ci // (2 * b)), (ri // b) != (ci // b))
        ms = [_dot(jnp.where(off, a, 0.0), t) for a, t in zip(a_mats, ts)]
        ts = [t - _dot(t, m) for t, m in zip(ts, ms)]
        b *= 2
    return ts


def _delta_prepare(qs, ks, vs, betas, gcs, grows, gts, blk, nsq):
    _, incl, strict = _tile_masks(blk)
    n = range(len(qs))
    dmats = [gc - gr for gc, gr in zip(gcs, grows)]
    decays = [jnp.where(incl, jnp.exp(jnp.where(incl, d, 0.0)), 0.0) for d in dmats]
    egcs = [jnp.exp(gc) for gc in gcs]
    kbs = [ks[i] * betas[i] for i in n]
    a_mats = [jnp.where(strict, _dot_nt(kbs[i], ks[i]) * decays[i], 0.0) for i in n]
    qks = [_dot_nt(qs[i], ks[i]) * decays[i] for i in n]
    xs = [jnp.concatenate([vs[i] * betas[i], kbs[i] * egcs[i]], axis=1) for i in n]
    tinvs = _unit_lower_inverse(a_mats, blk, SUBLANES, nsq)
    xs = [_dot(t, x) for t, x in zip(tinvs, xs)]
    us = [x[:, :DH] for x in xs]
    ws = [x[:, DH:] for x in xs]
    q_decs = [qs[i] * egcs[i] for i in n]
    k_decs = [ks[i] * jnp.exp(gts[i] - gcs[i]) for i in n]
    return us, ws, q_decs, k_decs, qks


def _gdn_features(xc, ba, prm_ref, valid):
    s = _silu(xc)
    if valid is not None:
        s = jnp.where(valid, s, 0.0)
    qs, ks, vs = [], [], []
    for h in range(NH):
        qh = s[:, h * DH:(h + 1) * DH]
        kh = s[:, GW + h * DH:GW + (h + 1) * DH]
        qs.append(qh * lax.rsqrt(jnp.sum(qh * qh, axis=-1, keepdims=True) + 1e-6) * (DH ** -0.5))
        ks.append(kh * lax.rsqrt(jnp.sum(kh * kh, axis=-1, keepdims=True) + 1e-6))
        vs.append(s[:, 2 * GW + h * DH:2 * GW + (h + 1) * DH])
    beta = jax.nn.sigmoid(ba)
    xs = ba + prm_ref[pl.ds(1, 1), :]
    softplus = jnp.maximum(xs, 0.0) + jnp.log1p(jnp.exp(-jnp.abs(xs)))
    g = -jnp.exp(prm_ref[pl.ds(0, 1), :]) * softplus
    if valid is not None:
        beta = jnp.where(valid, beta, 0.0)
        g = jnp.where(valid, g, 0.0)
    return qs, ks, vs, beta, g


def _gated_out(o, z, gout):
    return _rms(o, gout) * _silu(z)


def _gdn_p_kernel(x_ref, halo_ref, ba_ref, wc_ref, prm_ref, gout_ref,
                  o_ref, s_ref, cst_ref,
                  q_scr, k_scr, v_scr, bg_scr, st_scr):
    j = pl.program_id(1)
    nj = pl.num_programs(1)

    @pl.when(j == 0)
    def _():
        st_scr[...] = jnp.zeros_like(st_scr)

    x = x_ref[:, :QKV]
    halo = jnp.where(j > 0, halo_ref[:, :QKV], 0.0)
    xp = jnp.concatenate([halo, x], axis=0)
    xc = x * wc_ref[0, pl.ds(GCONV - 1, 1), :]
    for s in range(1, GCONV):
        xc = xc + pltpu.roll(xp, s, 0)[SUBLANES:, :] * wc_ref[0, pl.ds(GCONV - 1 - s, 1), :]
    qs, ks, vs, beta, g = _gdn_features(xc, ba_ref[...], prm_ref.at[0], None)
    for h in range(NH):
        q_scr[:, h * DH:(h + 1) * DH] = qs[h]
        k_scr[:, h * DH:(h + 1) * DH] = ks[h]
        v_scr[:, h * DH:(h + 1) * DH] = vs[h]
    bg_scr[0] = beta
    bg_scr[1] = g

    nsq = int(math.log2(SUBLANES)) - 1
    nblk = TILE // GCHUNK

    def tile_body(t, carry):
        hs = range(NH)
        cols = [slice(h * DH, (h + 1) * DH) for h in hs]
        rows, qs, ks, vs, betas, gcs, gts, grows = [], [], [], [], [], [], [], []
        for ti in range(TILES_PER_ITER):
            r = pl.ds(pl.multiple_of((t * TILES_PER_ITER + ti) * TILE, TILE), TILE)
            rows.append(r)
            beta_t = bg_scr[0, r, :]
            gc_t, gt_t = _gate_cumsums(bg_scr[1, r, :], GCHUNK)
            gc_tt = gc_t.T
            for h in hs:
                qs.append(q_scr[r, cols[h]])
                ks.append(k_scr[r, cols[h]])
                vs.append(v_scr[r, cols[h]])
                betas.append(beta_t[:, h:h + 1])
                gcs.append(gc_t[:, NH + h:NH + h + 1])
                gts.append(gt_t[:, NH + h:NH + h + 1])
                grows.append(gc_tt[NH + h:NH + h + 1, :])
        us, ws, q_decs, k_decs, qks = _delta_prepare(qs, ks, vs, betas, gcs, grows, gts, GCHUNK, nsq)
        s_cur = [st_scr[h] for h in hs]
        for ti in range(TILES_PER_ITER):
            ch = [ti * NH + h for h in hs]
            egts = [jnp.exp(gts[i]) for i in ch]
            vn = [[] for _ in hs]
            qs_ = [[] for _ in hs]
            for c in range(nblk):
                rr = slice(c * GCHUNK, (c + 1) * GCHUNK)
                wss = [_dot(jnp.concatenate([ws[ch[h]][rr], q_decs[ch[h]][rr]], axis=0), s_cur[h]) for h in hs]
                for h in hs:
                    vn[h].append(us[ch[h]][rr] - wss[h][:GCHUNK])
                    qs_[h].append(wss[h][GCHUNK:])
                s_cur = [s_cur[h] * egts[h][c * GCHUNK:c * GCHUNK + 1, :] + _dot_tn(k_decs[ch[h]][rr], vn[h][c])
                         for h in hs]
            os_ = [jnp.concatenate(qs_[h], axis=0) + _dot(qks[ch[h]], jnp.concatenate(vn[h], axis=0)) for h in hs]
            for h in hs:
                z = x_ref[rows[ti], pl.ds(QKV + h * DH, DH)]
                o_ref[rows[ti], cols[h]] = _gated_out(os_[h], z, gout_ref[0]).astype(o_ref.dtype)
        for h in hs:
            st_scr[h] = s_cur[h]
        return carry

    lax.fori_loop(0, LB_G // (TILE * TILES_PER_ITER), tile_body, 0)

    @pl.when(j == nj - 1)
    def _():
        s_ref[0] = st_scr[...]
        cst_ref[0] = x_ref[pl.ds(LB_G - SUBLANES, SUBLANES), :QKV]


def _gdn_prompt(layer, proj, w_gconv, gprm, g_gout):
    nj = L_P // LB_G
    return pl.pallas_call(
        _gdn_p_kernel,
        grid=(NB_P, nj),
        in_specs=[
            pl.BlockSpec((LB_G, COL_GLU), lambda b, j: (b * nj + j, 0)),
            pl.BlockSpec((SUBLANES, COL_GLU),
                         lambda b, j: (jnp.maximum((b * nj + j) * (LB_G // SUBLANES) - 1, 0), 0)),
            pl.BlockSpec((LB_G, LANES), lambda b, j: (b * nj + j, COL_BA // LANES)),
            pl.BlockSpec((1, GCONV, QKV), lambda b, j: (layer, 0, 0)),
            pl.BlockSpec((1, SUBLANES, LANES), lambda b, j: (layer, 0, 0)),
            pl.BlockSpec((1, 1, DH), lambda b, j: (layer, 0, 0)),
        ],
        out_specs=[
            pl.BlockSpec((LB_G, GW), lambda b, j: (b * nj + j, 0)),
            pl.BlockSpec((1, NH, DH, DH), lambda b, j: (b, 0, 0, 0)),
            pl.BlockSpec((1, SUBLANES, QKV), lambda b, j: (b, 0, 0)),
        ],
        out_shape=[
            jax.ShapeDtypeStruct((T_P, GW), bf16),
            jax.ShapeDtypeStruct((NB_P, NH, DH, DH), f32),
            jax.ShapeDtypeStruct((NB_P, SUBLANES, QKV), f32),
        ],
        scratch_shapes=[pltpu.VMEM((LB_G, GW), f32), pltpu.VMEM((LB_G, GW), f32),
                        pltpu.VMEM((LB_G, GW), f32), pltpu.VMEM((2, LB_G, LANES), f32),
                        pltpu.VMEM((NH, DH, DH), f32)],
        compiler_params=_cparams(("parallel", "arbitrary")),
        name=f"gdn_prompt_{layer}",
    )(proj, proj, proj, w_gconv, gprm, g_gout)


BT_S = TILE // TPAD


def _gdn_s_kernel(x_ref, ba_ref, cin_ref, s0_ref, wc_ref, prm_ref, gout_ref,
                  o_ref, s_ref, cst_ref):
    x = x_ref[:, :QKV]
    st = cin_ref[...]
    trow = lax.broadcasted_iota(jnp.int32, (TILE, 1), 0) % TPAD
    xc = x * wc_ref[0, pl.ds(GCONV - 1, 1), :]
    for s in range(1, GCONV):
        xs = jnp.where(trow >= s, pltpu.roll(x, s, 0), pltpu.roll(st, TILE - TPAD + s, 0))
        xc = xc + xs * wc_ref[0, pl.ds(GCONV - 1 - s, 1), :]
    valid = trow < L_S
    qs, ks, vs, beta, g = _gdn_features(xc, ba_ref[...], prm_ref.at[0], valid)
    cst_ref[...] = pltpu.roll(x, TPAD - (GCONV - 1) - (L_S - (GCONV - 1)), 0)

    nsq = int(math.ceil(math.log2(L_S))) - 1
    gc_t, gt_t = _gate_cumsums(g, TPAD)
    hs = range(NH)
    gcs = [gc_t[:, NH + h:NH + h + 1] for h in hs]
    gts = [gt_t[:, NH + h:NH + h + 1] for h in hs]
    gc_tt = gc_t.T
    grows = [gc_tt[NH + h:NH + h + 1, :] for h in hs]
    us, ws, q_decs, k_decs, qks = _delta_prepare(qs, ks, vs, [beta[:, h:h + 1] for h in hs], gcs, grows, gts,
                                                 TPAD, nsq)
    for h in hs:
        egt = jnp.exp(gts[h])
        vn, qs_ = [], []
        for b in range(BT_S):
            rr = slice(b * TPAD, (b + 1) * TPAD)
            s0 = s0_ref[b, h]
            wsb = _dot(jnp.concatenate([ws[h][rr], q_decs[h][rr]], axis=0), s0)
            vn_b = us[h][rr] - wsb[:TPAD]
            vn.append(vn_b)
            qs_.append(wsb[TPAD:])
            s_ref[b, h] = s0 * egt[b * TPAD:b * TPAD + 1, :] + _dot_tn(k_decs[h][rr], vn_b)
        o = jnp.concatenate(qs_, axis=0) + _dot(qks[h], jnp.concatenate(vn, axis=0))
        z = x_ref[:, pl.ds(QKV + h * DH, DH)]
        o_ref[:, h * DH:(h + 1) * DH] = _gated_out(o, z, gout_ref[0]).astype(o_ref.dtype)


def _gdn_sample(layer, proj, conv_in, s0, w_gconv, gprm, g_gout):
    row0 = T_P // TILE
    return pl.pallas_call(
        _gdn_s_kernel,
        grid=(T_S // TILE,),
        in_specs=[
            pl.BlockSpec((TILE, COL_GLU), lambda i: (row0 + i, 0)),
            pl.BlockSpec((TILE, LANES), lambda i: (row0 + i, COL_BA // LANES)),
            pl.BlockSpec((TILE, QKV), lambda i: (i, 0)),
            pl.BlockSpec((BT_S, NH, DH, DH), lambda i: (i, 0, 0, 0)),
            pl.BlockSpec((1, GCONV, QKV), lambda i: (layer, 0, 0)),
            pl.BlockSpec((1, SUBLANES, LANES), lambda i: (layer, 0, 0)),
            pl.BlockSpec((1, 1, DH), lambda i: (layer, 0, 0)),
        ],
        out_specs=[
            pl.BlockSpec((TILE, GW), lambda i: (i, 0)),
            pl.BlockSpec((BT_S, NH, DH, DH), lambda i: (i, 0, 0, 0)),
            pl.BlockSpec((TILE, QKV), lambda i: (i, 0)),
        ],
        out_shape=[
            jax.ShapeDtypeStruct((T_S, GW), bf16),
            jax.ShapeDtypeStruct((NB_S, NH, DH, DH), f32),
            jax.ShapeDtypeStruct((T_S, QKV), f32),
        ],
        compiler_params=_cparams(("parallel",)),
        name=f"gdn_sample_{layer}",
    )(proj, proj, conv_in, s0, w_gconv, gprm, g_gout)


def _group_ln_silu(y, gl_ref, bl_ref):
    outs = []
    for gi in range(CGROUPS):
        w = CW // CGROUPS
        yg = y[:, gi * w:(gi + 1) * w]
        mu = jnp.mean(yg, axis=-1, keepdims=True)
        d = yg - mu
        var = jnp.mean(d * d, axis=-1, keepdims=True)
        outs.append(d * lax.rsqrt(var + LN_EPS))
    yn = jnp.concatenate(outs, axis=1) * gl_ref[0] + bl_ref[0]
    return _silu(yn)


def _conf_p_kernel(x_ref, halo_ref, w_ref, b_ref, gl_ref, bl_ref, o_ref, hst_ref, xp_scr, sh_scr):
    j = pl.program_id(1)
    nj = pl.num_programs(1)
    h = x_ref[:, :CW] * jax.nn.sigmoid(x_ref[:, CW:])
    hh = halo_ref[:, :CW] * jax.nn.sigmoid(halo_ref[:, CW:])
    xp_scr[pl.ds(0, HIST_C), :] = jnp.where(j > 0, hh, 0.0)
    xp_scr[pl.ds(HIST_C, LB_C), :] = h
    n_sh = HIST_C + LB_C - SUBLANES
    for r in range(1, SUBLANES):
        sh_scr[r - 1] = xp_scr[pl.ds(r, n_sh), :]
    off = HIST_C - (CK - 1)
    y = None
    for t in range(CK):
        a, r = divmod(off + t, SUBLANES)
        rows = pl.ds(a * SUBLANES, LB_C)
        win = xp_scr[rows, :] if r == 0 else sh_scr[r - 1, rows, :]
        term = win * w_ref[0, pl.ds(t, 1), :]
        y = term if y is None else y + term
    y = y + b_ref[0]
    o_ref[...] = _group_ln_silu(y, gl_ref, bl_ref).astype(o_ref.dtype)

    @pl.when(j == nj - 1)
    def _():
        hst_ref[0] = h[LB_C - HIST_C:, :]


def _conf_prompt(layer, proj, w_dw, b_dw, g_ln, b_ln):
    nj = L_P // LB_C
    cb = COL_GLU // (2 * CW)
    return pl.pallas_call(
        _conf_p_kernel,
        grid=(NB_P, nj),
        in_specs=[
            pl.BlockSpec((LB_C, 2 * CW), lambda b, j: (b * nj + j, cb)),
            pl.BlockSpec((HIST_C, 2 * CW),
                         lambda b, j: (jnp.maximum((b * nj + j) * (LB_C // HIST_C) - 1, 0), cb)),
            pl.BlockSpec((1, CK, CW), lambda b, j: (layer, 0, 0)),
            pl.BlockSpec((1, 1, CW), lambda b, j: (layer, 0, 0)),
            pl.BlockSpec((1, 1, CW), lambda b, j: (layer, 0, 0)),
            pl.BlockSpec((1, 1, CW), lambda b, j: (layer, 0, 0)),
        ],
        out_specs=[
            pl.BlockSpec((LB_C, CW), lambda b, j: (b * nj + j, 0)),
            pl.BlockSpec((1, HIST_C, CW), lambda b, j: (b, 0, 0)),
        ],
        out_shape=[
            jax.ShapeDtypeStruct((T_P, CW), bf16),
            jax.ShapeDtypeStruct((NB_P, HIST_C, CW), f32),
        ],
        scratch_shapes=[pltpu.VMEM((HIST_C + LB_C, CW), f32),
                        pltpu.VMEM((SUBLANES - 1, HIST_C + LB_C - SUBLANES, CW), f32)],
        compiler_params=_cparams(("parallel", "arbitrary")),
        name=f"conf_prompt_{layer}",
    )(proj, proj, w_dw, b_dw, g_ln, b_ln)


BT_C = 16


def _conf_s_kernel(x_ref, hin_ref, wsh_ref, b_ref, gl_ref, bl_ref, o_ref, hst_ref, y_scr):
    h = x_ref[:, :CW] * jax.nn.sigmoid(x_ref[:, CW:])
    trow = lax.broadcasted_iota(jnp.int32, (TPAD, 1), 0)
    for b in range(BT_C):
        xp = jnp.concatenate([hin_ref[pl.ds(b * HIST_C, HIST_C), :], h[b * TPAD:(b + 1) * TPAD, :]], axis=0)
        y8 = jnp.zeros((TPAD, CW), f32)
        for t in range(L_S):
            yt = jnp.sum(xp * wsh_ref[0, t], axis=0, keepdims=True)
            y8 = jnp.where(trow == t, yt, y8)
        y_scr[pl.ds(b * TPAD, TPAD), :] = y8
        hst_ref[pl.ds(b * HIST_C, HIST_C), :] = pltpu.roll(xp, HIST_C + TPAD - L_S, 0)[:HIST_C, :]
    y = y_scr[...] + b_ref[0]
    o_ref[...] = _group_ln_silu(y, gl_ref, bl_ref).astype(o_ref.dtype)


def _conf_sample(layer, proj, hist_in, w_shift, b_dw, g_ln, b_ln):
    row0 = T_P // (BT_C * TPAD)
    cb = COL_GLU // (2 * CW)
    return pl.pallas_call(
        _conf_s_kernel,
        grid=(NB_S // BT_C,),
        in_specs=[
            pl.BlockSpec((BT_C * TPAD, 2 * CW), lambda i: (row0 + i, cb)),
            pl.BlockSpec((BT_C * HIST_C, CW), lambda i: (i, 0)),
            pl.BlockSpec((1, L_S, HIST_C + TPAD, CW), lambda i: (layer, 0, 0, 0)),
            pl.BlockSpec((1, 1, CW), lambda i: (layer, 0, 0)),
            pl.BlockSpec((1, 1, CW), lambda i: (layer, 0, 0)),
            pl.BlockSpec((1, 1, CW), lambda i: (layer, 0, 0)),
        ],
        out_specs=[
            pl.BlockSpec((BT_C * TPAD, CW), lambda i: (i, 0)),
            pl.BlockSpec((BT_C * HIST_C, CW), lambda i: (i, 0)),
        ],
        out_shape=[
            jax.ShapeDtypeStruct((T_S, CW), bf16),
            jax.ShapeDtypeStruct((NB_S * HIST_C, CW), f32),
        ],
        scratch_shapes=[pltpu.VMEM((BT_C * TPAD, CW), f32)],
        compiler_params=_cparams(("parallel",)),
        name=f"conf_sample_{layer}",
    )(proj, hist_in, w_shift, b_dw, g_ln, b_ln)


def _out_kernel(moe, x_ref, oap, obp, oas, obs, w_ref, g_ref,
                g1p, scp, shp, g1s, scs, shs, *rest):
    if moe:
        wr_ref, br_ref, xo_ref, h2_ref, rt_ref = rest
        hf_scr = h2_ref
    else:
        xo_ref, h2_ref, hf_scr = rest
    i = pl.program_id(0)

    def attn(oa, ob):
        return (jnp.dot(oa[...], w_ref[0, :GW, :], preferred_element_type=f32)
                + jnp.dot(ob[...], w_ref[0, GW:, :], preferred_element_type=f32))

    @pl.when(i < NT_P)
    def _():
        xn = x_ref[...] + g1p[0] * attn(oap, obp)
        xo_ref[...] = xn
        hf_scr[...] = _rms(xn, g_ref[0]) * (1.0 + scp[0]) + shp[0]

    @pl.when(i >= NT_P)
    def _():
        hf_scr[...] = attn(oas, obs)

        def body(rows, b):
            m = pl.ds(b, 1)
            xn = x_ref[rows, :] + g1s[m, :] * hf_scr[rows, :]
            xo_ref[rows, :] = xn
            hf_scr[rows, :] = _rms(xn, g_ref[0]) * (1.0 + scs[m, :]) + shs[m, :]
        _sample_rows(body)

    hf = hf_scr[...]
    if not moe:
        h2_ref[...] = hf.astype(h2_ref.dtype)
    if moe:
        logits = _dot_hi(hf, wr_ref[0]) + br_ref[0]
        lane = lax.broadcasted_iota(jnp.int32, logits.shape, 1)
        ex = jnp.exp(logits - jnp.max(logits, axis=-1, keepdims=True))
        probs = ex / jnp.sum(ex, axis=-1, keepdims=True)
        m1 = jnp.max(probs, axis=-1, keepdims=True)
        i1 = jnp.min(jnp.where(probs == m1, lane, LANES), axis=-1, keepdims=True)
        rest_p = jnp.where(lane == i1, -1.0, probs)
        m2 = jnp.max(rest_p, axis=-1, keepdims=True)
        i2 = jnp.min(jnp.where(rest_p == m2, lane, LANES), axis=-1, keepdims=True)
        den = m1 + m2
        rt_ref[...] = jnp.where(lane == 0, i1.astype(f32),
                                jnp.where(lane == 1, i2.astype(f32),
                                          jnp.where(lane == 2, m1 / den,
                                                    jnp.where(lane == 3, m2 / den, 0.0))))


def _out_proj(layer, moe, x, oa_p, ob_p, oa_s, ob_s, w_out, g2, modp, mods, wr=None, br=None):
    idx = layer // 2
    in_specs = [
        pl.BlockSpec((TM, D), lambda i: (i, 0)),
        pl.BlockSpec((TM, GW), lambda i: (jnp.minimum(i, NT_P - 1), 0)),
        pl.BlockSpec((TM, CW), lambda i: (jnp.minimum(i, NT_P - 1), 0)),
        pl.BlockSpec((TM, GW), lambda i: (0, 0)),
        pl.BlockSpec((TM, CW), lambda i: (0, 0)),
        pl.BlockSpec((1, D, D), lambda i: (layer, 0, 0)),
        pl.BlockSpec((1, 1, D), lambda i: (layer, 0, 0)),
        _mod_specs(G1, 1)[0], _mod_specs(SC2, 1)[0], _mod_specs(SH2, 1)[0],
        _mod_specs(G1, 1)[1], _mod_specs(SC2, 1)[1], _mod_specs(SH2, 1)[1],
    ]
    args = [x, oa_p, ob_p, oa_s, ob_s, w_out, g2, modp, modp, modp, mods, mods, mods]
    out_specs = [pl.BlockSpec((TM, D), lambda i: (i, 0)), pl.BlockSpec((TM, D), lambda i: (i, 0))]
    out_shape = [jax.ShapeDtypeStruct((T_ALL, D), f32),
                 jax.ShapeDtypeStruct((T_ALL, D), f32 if moe else bf16)]
    if moe:
        in_specs += [pl.BlockSpec((1, D, LANES), lambda i: (idx, 0, 0)),
                     pl.BlockSpec((1, 1, LANES), lambda i: (idx, 0, 0))]
        args += [wr, br]
        out_specs.append(pl.BlockSpec((TM, LANES), lambda i: (i, 0)))
        out_shape.append(jax.ShapeDtypeStruct((T_ALL, LANES), f32))
    return pl.pallas_call(
        functools.partial(_out_kernel, moe),
        grid=(NT,),
        in_specs=in_specs,
        out_specs=out_specs,
        out_shape=out_shape,
        scratch_shapes=[] if moe else [pltpu.VMEM((TM, D), f32)],
        compiler_params=_cparams(("parallel",)),
        name=f"out_proj_{layer}",
    )(*args)


def _residual(i, x_ref, f_ref, g2p, g2s, o_ref, gf_ref, os_ref=None):
    os_ref = o_ref if os_ref is None else os_ref

    def fin(v):
        return v if gf_ref is None else _rms(v, gf_ref[...])

    @pl.when(i < NT_P)
    def _():
        o_ref[...] = fin(x_ref[...] + g2p[0] * f_ref[...])

    @pl.when(i >= NT_P)
    def _():
        def body(rows, b):
            os_ref[rows, :] = fin(x_ref[rows, :] + g2s[pl.ds(b, 1), :] * f_ref[rows, :])
        _sample_rows(body)


def _ffn_kernel(h_ref, wg_ref, wu_ref, wd_ref, x_ref, g2p, g2s, o_ref, acc_scr):
    i = pl.program_id(0)
    h = h_ref[...]
    for c in range(FF_D // TF_D):
        cols = slice(c * TF_D, (c + 1) * TF_D)
        gate = jnp.dot(h, wg_ref[0, :, cols], preferred_element_type=f32)
        up = jnp.dot(h, wu_ref[0, :, cols], preferred_element_type=f32)
        part = jnp.dot((_silu(gate) * up).astype(bf16), wd_ref[0, cols, :], preferred_element_type=f32)
        if c == 0:
            acc_scr[...] = part
        else:
            acc_scr[...] += part
    _residual(i, x_ref, acc_scr, g2p, g2s, o_ref, None)


def _ffn_dense(layer, h2, x, wg, wu, wd, modp, mods):
    idx = layer // 2
    mp, ms = _mod_specs(G2, 1)
    return pl.pallas_call(
        _ffn_kernel,
        grid=(NT,),
        in_specs=[
            pl.BlockSpec((TM, D), lambda i: (i, 0)),
            _resident((1, D, FF_D), lambda i: (idx, 0, 0)),
            _resident((1, D, FF_D), lambda i: (idx, 0, 0)),
            _resident((1, FF_D, D), lambda i: (idx, 0, 0)),
            pl.BlockSpec((TM, D), lambda i: (i, 0)),
            mp, ms,
        ],
        out_specs=pl.BlockSpec((TM, D), lambda i: (i, 0)),
        out_shape=jax.ShapeDtypeStruct((T_ALL, D), f32),
        scratch_shapes=[pltpu.VMEM((TM, D), f32)],
        compiler_params=_cparams(("parallel",)),
        name=f"ffn_dense_{layer}",
    )(h2, wg, wu, wd, x, modp, mods)


def _gather_rows(idx_ref, base, stride, src_hbm, dst, sem):
    def issue(r, carry):
        row = idx_ref[base + stride * r]
        pltpu.make_async_copy(src_hbm.at[pl.ds(row, 1), :], dst.at[pl.ds(r, 1), :], sem).start()
        return carry
    lax.fori_loop(0, MOE_BLK, issue, 0, unroll=8)


def _wait_rows(src_hbm, dst, sem):
    pltpu.make_async_copy(src_hbm.at[pl.ds(0, MOE_BLK), :], dst, sem).wait()


def _expert_kernel(be_ref, tok_ref, nv_ref, h_hbm, wg_ref, wu_ref, wd_ref,
                   y_ref, xg_scr, xb_scr, sem):
    m = pl.program_id(0)
    f = pl.program_id(1)
    nv = nv_ref[0]
    active = m < nv
    slot = m % 2

    @pl.when(jnp.logical_and(f == 0, m == 0))
    def _():
        _gather_rows(tok_ref, 0, 1, h_hbm, xg_scr.at[0], sem.at[0])

    @pl.when(jnp.logical_and(f == 0, active))
    def _():
        _wait_rows(h_hbm, xg_scr.at[slot], sem.at[slot])
        xb_scr[...] = xg_scr[slot].astype(bf16)

        @pl.when(m + 1 < nv)
        def _():
            _gather_rows(tok_ref, (m + 1) * MOE_BLK, 1, h_hbm, xg_scr.at[1 - slot], sem.at[1 - slot])

    @pl.when(f == 0)
    def _():
        y_ref[...] = jnp.zeros_like(y_ref)

    @pl.when(active)
    def _():
        xb = xb_scr[...]
        a = _silu(_dot(xb, wg_ref[0, 0])) * _dot(xb, wu_ref[0, 0])
        y_ref[...] += _dot(a, wd_ref[0, 0])


def _experts(idx, h2, block_e, slot_tok, n_valid, wg, wu, wd):
    nf = FF_E // TF_E

    def wmap_up(m, f, be, tok, nv):
        return (idx, be[m], 0, jnp.where(m < nv[0], f, nf - 1))

    def wmap_down(m, f, be, tok, nv):
        return (idx, be[m], jnp.where(m < nv[0], f, nf - 1), 0)

    grid_spec = pltpu.PrefetchScalarGridSpec(
        num_scalar_prefetch=3,
        grid=(N_BLK, nf),
        in_specs=[
            pl.BlockSpec(memory_space=pl.ANY),
            pl.BlockSpec((1, 1, D, TF_E), wmap_up),
            pl.BlockSpec((1, 1, D, TF_E), wmap_up),
            pl.BlockSpec((1, 1, TF_E, D), wmap_down),
        ],
        out_specs=pl.BlockSpec((MOE_BLK, D), lambda m, f, be, tok, nv: (m, 0)),
        scratch_shapes=[pltpu.VMEM((2, MOE_BLK, D), f32), pltpu.VMEM((MOE_BLK, D), bf16),
                        pltpu.SemaphoreType.DMA((2,))],
    )
    return pl.pallas_call(
        _expert_kernel,
        grid_spec=grid_spec,
        out_shape=jax.ShapeDtypeStruct((N_BLK * MOE_BLK, D), f32),
        compiler_params=_cparams(("arbitrary", "arbitrary")),
        name=f"experts_{idx}",
    )(block_e, slot_tok, n_valid, h2, wg, wu, wd)


def _combine_kernel(final, dest_ref, y_hbm, x_ref, rt_ref, g2p, g2s, *rest):
    if final:
        gf_ref, o_ref, os_ref, g_scr, f_scr, sem = rest
    else:
        o_ref, g_scr, f_scr, sem = rest
        gf_ref = os_ref = None
    i = pl.program_id(0)
    slot = i % 2

    def gather(tile, sl):
        for k in range(2):
            _gather_rows(dest_ref, 2 * tile * TM + k, 2, y_hbm, g_scr.at[sl, k], sem.at[sl, k])

    @pl.when(i == 0)
    def _():
        gather(0, 0)

    @pl.when(i + 1 < pl.num_programs(0))
    def _():
        gather(i + 1, 1 - slot)

    for k in range(2):
        _wait_rows(y_hbm, g_scr.at[slot, k], sem.at[slot, k])
    f_scr[...] = rt_ref[:, 2:3] * g_scr[slot, 0] + rt_ref[:, 3:4] * g_scr[slot, 1]
    _residual(i, x_ref, f_scr, g2p, g2s, o_ref, gf_ref, os_ref)


def _combine(final, dest, yb, x, rt, modp, mods, g_final):
    mp, ms = _mod_specs(G2, 1)
    wrap = lambda spec: pl.BlockSpec(spec.block_shape, lambda i, d, _f=spec.index_map: _f(i))
    in_specs = [pl.BlockSpec(memory_space=pl.ANY),
                pl.BlockSpec((TM, D), lambda i, d: (i, 0)),
                pl.BlockSpec((TM, LANES), lambda i, d: (i, 0)), wrap(mp), wrap(ms)]
    args = [yb, x, rt, modp, mods]
    if final:
        in_specs.append(pl.BlockSpec((1, D), lambda i, d: (0, 0)))
        args.append(g_final)
        out_specs = [pl.BlockSpec((TM, D), lambda i, d: (jnp.minimum(i, NT_P - 1), 0)),
                     pl.BlockSpec((T_S, D), lambda i, d: (0, 0))]
        out_shape = [jax.ShapeDtypeStruct((T_P, D), f32), jax.ShapeDtypeStruct((T_S, D), f32)]
    else:
        out_specs = pl.BlockSpec((TM, D), lambda i, d: (i, 0))
        out_shape = jax.ShapeDtypeStruct((T_ALL, D), f32)
    grid_spec = pltpu.PrefetchScalarGridSpec(
        num_scalar_prefetch=1,
        grid=(NT,),
        in_specs=in_specs,
        out_specs=out_specs,
        scratch_shapes=[pltpu.VMEM((2, 2, TM, D), f32), pltpu.VMEM((TM, D), f32),
                        pltpu.SemaphoreType.DMA((2, 2))],
    )
    return pl.pallas_call(
        functools.partial(_combine_kernel, final),
        grid_spec=grid_spec,
        out_shape=out_shape,
        compiler_params=_cparams(("arbitrary",)),
        name="combine_final" if final else "combine",
    )(dest, *args)


def _route(rt):
    e = rt[:, :2].astype(jnp.int32).reshape(N_ASG)
    onehot = (e[:, None] == jnp.arange(NE, dtype=jnp.int32)[None, :]).astype(jnp.int32)
    csum = jnp.cumsum(onehot, axis=0)
    counts = csum[-1]
    rank = jnp.sum(onehot * (csum - 1), axis=1)
    padded = (counts + MOE_BLK - 1) // MOE_BLK * MOE_BLK
    pad_end = jnp.cumsum(padded)
    pad_start = pad_end - padded
    dest = (jnp.sum(onehot * pad_start[None, :], axis=1) + rank).astype(jnp.int32)
    n_slots = N_BLK * MOE_BLK
    tok = jnp.arange(N_ASG, dtype=jnp.int32) // 2
    slot_tok = jnp.zeros((n_slots,), jnp.int32).at[dest].set(tok, unique_indices=True)
    n_valid = (pad_end[-1] // MOE_BLK).astype(jnp.int32)
    blk = jnp.minimum(jnp.arange(N_BLK, dtype=jnp.int32), n_valid - 1)
    block_e = jnp.minimum(jnp.searchsorted(pad_end, blk * MOE_BLK, side="right"), NE - 1).astype(jnp.int32)
    return dest, slot_tok, block_e, n_valid.reshape(1)


def kernel(x_prompt, x_sample, c_prompt, c_sample, state_gdn, state_gdn_conv, state_conf_conv, w_ada, b_ada, g_norm1, g_norm2, w_in, w_gdn_conv, a_log, dt_bias, g_gdn_out, w_conf_dw, b_conf_dw, g_conf_ln, b_conf_ln, w_out, w_ff_gate, w_ff_up, w_ff_down, w_router, b_router, w_exp_gate, w_exp_up, w_exp_down, g_final):
    xs_pad = jnp.pad(x_sample, ((0, 0), (0, TPAD - L_S), (0, 0)))
    x = jnp.concatenate([x_prompt.reshape(T_P, D), xs_pad.reshape(T_S, D)], axis=0)
    c_all = jnp.concatenate([c_prompt, c_sample], axis=0)
    o1, o2, o4 = QKV + GW, QKV + GW + 2 * NH, QKV + GW + 2 * NH + 2 * CW
    w_cat = jnp.concatenate([w_in[:, :, :o1], w_in[:, :, o2:o4], w_in[:, :, o1:o2],
                             jnp.zeros((DEPTH, D, PROJ_W - o4), f32)], axis=-1).astype(bf16)
    lane_pad = ((0, 0), (NH, LANES - 2 * NH))
    gprm = jnp.stack([jnp.pad(a_log, lane_pad), jnp.pad(dt_bias, lane_pad)], axis=1)
    gprm = jnp.pad(gprm, ((0, 0), (0, SUBLANES - 2), (0, 0)))
    wr_pad = jnp.pad(w_router, ((0, 0), (0, 0), (0, LANES - NE)))
    br_pad = jnp.pad(b_router, ((0, 0), (0, LANES - NE)), constant_values=-1e30).reshape(-1, 1, LANES)
    w_shift = jnp.stack([jnp.pad(w_conf_dw, ((0, 0), (t + HIST_C - (CK - 1), TPAD - 1 - t), (0, 0)))
                         for t in range(L_S)], axis=1)
    gconv_in = jnp.pad(state_gdn_conv, ((0, 0), (0, 0), (TPAD - (GCONV - 1), 0), (0, 0))).reshape(DEPTH, T_S, QKV)
    conf_in = jnp.pad(state_conf_conv, ((0, 0), (0, 0), (HIST_C - (CK - 1), 0), (0, 0))).reshape(DEPTH, NB_S * HIST_C, CW)
    w_out = w_out.astype(bf16)
    w_ff_gate, w_ff_up, w_ff_down = (w.astype(bf16) for w in (w_ff_gate, w_ff_up, w_ff_down))
    g1 = g_norm1.reshape(DEPTH, 1, D)
    g2 = g_norm2.reshape(DEPTH, 1, D)
    gout = g_gdn_out.reshape(DEPTH, 1, DH)
    b_dw = b_conf_dw.reshape(DEPTH, 1, CW)
    g_ln = g_conf_ln.reshape(DEPTH, 1, CW)
    b_ln = b_conf_ln.reshape(DEPTH, 1, CW)

    mod = _ada(c_all, w_ada, b_ada)

    sp_l, cp_l, fp_l, ss_l, cs_l, fs_l = [], [], [], [], [], []
    for layer in range(DEPTH):
        modp = mod[layer, :NB_P].reshape(NB_P, 1, N_MOD * D)
        mods = mod[layer, NB_P:]
        proj = _in_proj(layer, x, g1, modp, mods, w_cat)
        oa_p, s_p, c_p = _gdn_prompt(layer, proj, w_gdn_conv, gprm, gout)
        oa_s, s_s, c_s = _gdn_sample(layer, proj, gconv_in[layer], state_gdn[layer], w_gdn_conv, gprm, gout)
        ob_p, f_p = _conf_prompt(layer, proj, w_conf_dw, b_dw, g_ln, b_ln)
        ob_s, f_s = _conf_sample(layer, proj, conf_in[layer], w_shift, b_dw, g_ln, b_ln)
        sp_l.append(s_p)
        cp_l.append(c_p[:, TPAD - (GCONV - 1):, :])
        fp_l.append(f_p[:, HIST_C - (CK - 1):, :])
        ss_l.append(s_s)
        cs_l.append(c_s.reshape(NB_S, TPAD, QKV)[:, TPAD - (GCONV - 1):, :])
        fs_l.append(f_s.reshape(NB_S, HIST_C, CW)[:, HIST_C - (CK - 1):, :])
        if layer % 2 == 0:
            x, h2 = _out_proj(layer, False, x, oa_p, ob_p, oa_s, ob_s, w_out, g2, modp, mods)
            x = _ffn_dense(layer, h2, x, w_ff_gate, w_ff_up, w_ff_down, modp, mods)
        else:
            x, h2, rt = _out_proj(layer, True, x, oa_p, ob_p, oa_s, ob_s, w_out, g2, modp, mods, wr_pad, br_pad)
            dest, slot_tok, block_e, n_valid = _route(rt)
            yb = _experts(layer // 2, h2, block_e, slot_tok, n_valid, w_exp_gate, w_exp_up, w_exp_down)
            final = layer == DEPTH - 1
            x = _combine(final, dest, yb, x, rt, modp, mods, g_final.reshape(1, D))

    y_p, y_s = x
    y_prompt = y_p.reshape(NB_P, L_P, D)
    y_sample = y_s.reshape(NB_S, TPAD, D)[:, :L_S, :]
    return (y_prompt, y_sample, jnp.stack(sp_l), jnp.stack(cp_l), jnp.stack(fp_l),
            jnp.stack(ss_l), jnp.stack(cs_l), jnp.stack(fs_l))
```

```python
import functools
import math

import jax
import jax.numpy as jnp
from jax import lax
from jax.experimental import pallas as pl
from jax.experimental.pallas import tpu as pltpu

f32 = jnp.float32
bf16 = jnp.bfloat16
HIGHEST = lax.Precision.HIGHEST

D = 1024
NB_P, L_P = 8, 2048
NB_S, L_S = 128, 4
DEPTH = 4
NH, DH = 4, 128
GW = NH * DH
QKV = 3 * GW
GCONV = 4
GCHUNK = 64
CW = D - GW
CGROUPS = 4
CK = 31
FF_D = 2816
NE = 8
FF_E = 3584
N_MOD = 6
RMS_EPS = 1e-6
LN_EPS = 1e-5

SUBLANES = 8
LANES = 128
VMEM_LIMIT = 56 * 1024 * 1024

TPAD = SUBLANES
T_P = NB_P * L_P
T_S = NB_S * TPAD
T_ALL = T_P + T_S
TM = 1024
NT = T_ALL // TM
NT_P = T_P // TM
TILES_PER_SEQ = L_P // TM
PROJ_W = 3200
COL_GLU = 2048
COL_BA = 3072
TN_IN = 640
HIST_C = 32
SH1, SC1, G1, SH2, SC2, G2 = range(N_MOD)

LB_G = 512
LB_C = 512
TILE = 128
TILES_PER_ITER = 4
TF_D = 1408
TF_E = 512
MOE_BLK = 1024
N_ASG = 2 * T_ALL
N_BLK = N_ASG // MOE_BLK + NE
GATHER_CHUNK = MOE_BLK // (FF_E // TF_E)
GATHER_TAIL = MOE_BLK - GATHER_CHUNK * (FF_E // TF_E)


def _cparams(sem):
    return pltpu.CompilerParams(dimension_semantics=sem, vmem_limit_bytes=VMEM_LIMIT)


def _dot(a, b):
    return jnp.dot(a.astype(bf16), b.astype(bf16), preferred_element_type=f32)


def _dot_hi(a, b):
    return jnp.dot(a, b, precision=HIGHEST, preferred_element_type=f32)


def _dot_nt(a, b):
    return lax.dot_general(a.astype(bf16), b.astype(bf16), (((1,), (1,)), ((), ())),
                           preferred_element_type=f32)


def _dot_nt_hi(a, b):
    return lax.dot_general(a, b, (((1,), (1,)), ((), ())), precision=HIGHEST,
                           preferred_element_type=f32)


def _dot_tn(a, b):
    return lax.dot_general(a, b, (((0,), (0,)), ((), ())), preferred_element_type=f32)


def _silu(x):
    return x * jax.nn.sigmoid(x)


def _rms(x, g):
    return x * lax.rsqrt(jnp.mean(x * x, axis=-1, keepdims=True) + RMS_EPS) * g


def _ada_kernel(c_ref, w_ref, b_ref, o_ref):
    a = _silu(c_ref[...])
    o_ref[0] = _dot(a, w_ref[0]) + b_ref[0]


def _ada(c_all, w_ada, b_ada):
    n = c_all.shape[0]
    tn = 1536
    return pl.pallas_call(
        _ada_kernel,
        grid=(DEPTH, N_MOD * D // tn),
        in_specs=[
            pl.BlockSpec((n, D), lambda l, j: (0, 0)),
            pl.BlockSpec((1, D, tn), lambda l, j: (l, 0, j)),
            pl.BlockSpec((1, 1, tn), lambda l, j: (l, 0, j)),
        ],
        out_specs=pl.BlockSpec((1, n, tn), lambda l, j: (l, 0, j)),
        out_shape=jax.ShapeDtypeStruct((DEPTH, n, N_MOD * D), f32),
        compiler_params=_cparams(("parallel", "parallel")),
        name="ada",
    )(c_all, w_ada, b_ada.reshape(DEPTH, 1, N_MOD * D))


def _mod_specs(k, ngrid):
    if ngrid == 1:
        return [pl.BlockSpec((1, 1, D), lambda i: (jnp.minimum(i // TILES_PER_SEQ, NB_P - 1), 0, k)),
                pl.BlockSpec((NB_S, D), lambda i: (0, k))]
    return [pl.BlockSpec((1, 1, D), lambda i, j: (jnp.minimum(i // TILES_PER_SEQ, NB_P - 1), 0, k)),
            pl.BlockSpec((NB_S, D), lambda i, j: (0, k))]


def _sample_rows(body):
    def step(b, carry):
        body(pl.ds(pl.multiple_of(b * TPAD, TPAD), TPAD), b)
        return carry
    lax.fori_loop(0, NB_S, step, 0)


def _in_kernel(x_ref, g_ref, shp, scp, shs, scs, w_ref, o_ref, h_scr, hf_scr):
    i = pl.program_id(0)

    @pl.when(i < NT_P)
    def _():
        h = _rms(x_ref[...], g_ref[0]) * (1.0 + scp[0]) + shp[0]
        h_scr[...] = h.astype(bf16)

    @pl.when(i >= NT_P)
    def _():
        def body(rows, b):
            hf_scr[rows, :] = (_rms(x_ref[rows, :], g_ref[0]) * (1.0 + scs[pl.ds(b, 1), :])
                               + shs[pl.ds(b, 1), :])
        _sample_rows(body)
        h_scr[...] = hf_scr[...].astype(bf16)

    for c in range(PROJ_W // TN_IN):
        cols = slice(c * TN_IN, (c + 1) * TN_IN)
        o_ref[:, cols] = jnp.dot(h_scr[...], w_ref[0, :, cols], preferred_element_type=f32)


def _resident(block_shape, index_map):
    return pl.BlockSpec(block_shape, index_map, pipeline_mode=pl.Buffered(1))


def _in_proj(layer, x, g1, modp, mods, w_cat):
    return pl.pallas_call(
        _in_kernel,
        grid=(NT,),
        in_specs=[
            pl.BlockSpec((TM, D), lambda i: (i, 0)),
            pl.BlockSpec((1, 1, D), lambda i: (layer, 0, 0)),
            *_mod_specs(SH1, 1)[:1], *_mod_specs(SC1, 1)[:1],
            *_mod_specs(SH1, 1)[1:], *_mod_specs(SC1, 1)[1:],
            _resident((1, D, PROJ_W), lambda i: (layer, 0, 0)),
        ],
        out_specs=pl.BlockSpec((TM, PROJ_W), lambda i: (i, 0)),
        out_shape=jax.ShapeDtypeStruct((T_ALL, PROJ_W), f32),
        scratch_shapes=[pltpu.VMEM((TM, D), bf16), pltpu.VMEM((TM, D), f32)],
        compiler_params=_cparams(("parallel",)),
        name=f"in_proj_{layer}",
    )(x, g1, modp, modp, mods, mods, w_cat)


def _tile_masks(blk):
    ri = lax.broadcasted_iota(jnp.int32, (TILE, TILE), 0)
    ci = lax.broadcasted_iota(jnp.int32, (TILE, TILE), 1)
    same = (ri // blk) == (ci // blk)
    incl = jnp.logical_and(same, ri >= ci)
    strict = jnp.logical_and(same, ri > ci)
    return same, incl, strict


def _gate_cumsums(g_all, blk):
    same, incl, _ = _tile_masks(blk)
    gc = _dot_hi(incl.astype(f32), g_all)
    gt = _dot_hi(same.astype(f32), g_all)
    return gc, gt


def _unit_lower_inverse(a_mats, blk, base, nsq):
    ri = lax.broadcasted_iota(jnp.int32, (TILE, TILE), 0)
    ci = lax.broadcasted_iota(jnp.int32, (TILE, TILE), 1)
    ps = [jnp.where((ri // base) == (ci // base), -a, 0.0) for a in a_mats]
    eye = (ri == ci).astype(f32)
    ts = [eye + p for p in ps]
    for _ in range(nsq):
        ps = [_dot(p, p) for p in ps]
        ts = [t + _dot(p, t) for p, t in zip(ps, ts)]
    b = base
    while b < blk:
        off = jnp.logical_and((ri // (2 * b)) == (ci // (2 * b)), (ri // b) != (ci // b))
        ms = [_dot(jnp.where(off, a, 0.0), t) for a, t in zip(a_mats, ts)]
        ts = [t - _dot(t, m) for t, m in zip(ts, ms)]
        b *= 2
    return ts


def _delta_prepare(qs, ks, vs, betas, gcs, grows, gts, blk, nsq):
    _, incl, strict = _tile_masks(blk)
    n = range(len(qs))
    dmats = [gc - gr for gc, gr in zip(gcs, grows)]
    decays = [jnp.where(incl, jnp.exp(jnp.where(incl, d, 0.0)), 0.0) for d in dmats]
    egcs = [jnp.exp(gc) for gc in gcs]
    kbs = [ks[i] * betas[i] for i in n]
    a_mats = [jnp.where(strict, _dot_nt(kbs[i], ks[i]) * decays[i], 0.0) for i in n]
    qks = [_dot_nt(qs[i], ks[i]) * decays[i] for i in n]
    xs = [jnp.concatenate([vs[i] * betas[i], kbs[i] * egcs[i]], axis=1) for i in n]
    tinvs = _unit_lower_inverse(a_mats, blk, SUBLANES, nsq)
    xs = [_dot(t, x) for t, x in zip(tinvs, xs)]
    us = [x[:, :DH] for x in xs]
    ws = [x[:, DH:] for x in xs]
    q_decs = [qs[i] * egcs[i] for i in n]
    k_decs = [ks[i] * jnp.exp(gts[i] - gcs[i]) for i in n]
    return us, ws, q_decs, k_decs, qks


def _gdn_features(xc, ba, prm_ref, valid):
    s = _silu(xc)
    if valid is not None:
        s = jnp.where(valid, s, 0.0)
    qs, ks, vs = [], [], []
    for h in range(NH):
        qh = s[:, h * DH:(h + 1) * DH]
        kh = s[:, GW + h * DH:GW + (h + 1) * DH]
        qs.append(qh * lax.rsqrt(jnp.sum(qh * qh, axis=-1, keepdims=True) + 1e-6) * (DH ** -0.5))
        ks.append(kh * lax.rsqrt(jnp.sum(kh * kh, axis=-1, keepdims=True) + 1e-6))
        vs.append(s[:, 2 * GW + h * DH:2 * GW + (h + 1) * DH])
    beta = jax.nn.sigmoid(ba)
    xs = ba + prm_ref[pl.ds(1, 1), :]
    softplus = jnp.maximum(xs, 0.0) + jnp.log1p(jnp.exp(-jnp.abs(xs)))
    g = -jnp.exp(prm_ref[pl.ds(0, 1), :]) * softplus
    if valid is not None:
        beta = jnp.where(valid, beta, 0.0)
        g = jnp.where(valid, g, 0.0)
    return qs, ks, vs, beta, g


def _gated_out(o, z, gout):
    return _rms(o, gout) * _silu(z)


def _gdn_p_kernel(x_ref, halo_ref, ba_ref, wc_ref, prm_ref, gout_ref,
                  o_ref, s_ref, cst_ref,
                  q_scr, k_scr, v_scr, bg_scr, st_scr):
    j = pl.program_id(1)
    nj = pl.num_programs(1)

    @pl.when(j == 0)
    def _():
        st_scr[...] = jnp.zeros_like(st_scr)

    x = x_ref[:, :QKV]
    halo = jnp.where(j > 0, halo_ref[:, :QKV], 0.0)
    xp = jnp.concatenate([halo, x], axis=0)
    xc = x * wc_ref[0, pl.ds(GCONV - 1, 1), :]
    for s in range(1, GCONV):
        xc = xc + pltpu.roll(xp, s, 0)[SUBLANES:, :] * wc_ref[0, pl.ds(GCONV - 1 - s, 1), :]
    qs, ks, vs, beta, g = _gdn_features(xc, ba_ref[...], prm_ref.at[0], None)
    for h in range(NH):
        q_scr[:, h * DH:(h + 1) * DH] = qs[h]
        k_scr[:, h * DH:(h + 1) * DH] = ks[h]
        v_scr[:, h * DH:(h + 1) * DH] = vs[h]
    bg_scr[0] = beta
    bg_scr[1] = g

    nsq = int(math.log2(SUBLANES)) - 1
    nblk = TILE // GCHUNK

    def tile_body(t, carry):
        hs = range(NH)
        cols = [slice(h * DH, (h + 1) * DH) for h in hs]
        rows, qs, ks, vs, betas, gcs, gts, grows = [], [], [], [], [], [], [], []
        for ti in range(TILES_PER_ITER):
            r = pl.ds(pl.multiple_of((t * TILES_PER_ITER + ti) * TILE, TILE), TILE)
            rows.append(r)
            beta_t = bg_scr[0, r, :]
            gc_t, gt_t = _gate_cumsums(bg_scr[1, r, :], GCHUNK)
            gc_tt = gc_t.T
            for h in hs:
                qs.append(q_scr[r, cols[h]])
                ks.append(k_scr[r, cols[h]])
                vs.append(v_scr[r, cols[h]])
                betas.append(beta_t[:, h:h + 1])
                gcs.append(gc_t[:, NH + h:NH + h + 1])
                gts.append(gt_t[:, NH + h:NH + h + 1])
                grows.append(gc_tt[NH + h:NH + h + 1, :])
        us, ws, q_decs, k_decs, qks = _delta_prepare(qs, ks, vs, betas, gcs, grows, gts, GCHUNK, nsq)
        s_cur = [st_scr[h] for h in hs]
        for ti in range(TILES_PER_ITER):
            ch = [ti * NH + h for h in hs]
            egts = [jnp.exp(gts[i]) for i in ch]
            vn = [[] for _ in hs]
            qs_ = [[] for _ in hs]
            for c in range(nblk):
                rr = slice(c * GCHUNK, (c + 1) * GCHUNK)
                wss = [_dot(jnp.concatenate([ws[ch[h]][rr], q_decs[ch[h]][rr]], axis=0), s_cur[h]) for h in hs]
                for h in hs:
                    vn[h].append(us[ch[h]][rr] - wss[h][:GCHUNK])
                    qs_[h].append(wss[h][GCHUNK:])
                s_cur = [s_cur[h] * egts[h][c * GCHUNK:c * GCHUNK + 1, :] + _dot_tn(k_decs[ch[h]][rr], vn[h][c])
                         for h in hs]
            os_ = [jnp.concatenate(qs_[h], axis=0) + _dot(qks[ch[h]], jnp.concatenate(vn[h], axis=0)) for h in hs]
            for h in hs:
                z = x_ref[rows[ti], pl.ds(QKV + h * DH, DH)]
                o_ref[rows[ti], cols[h]] = _gated_out(os_[h], z, gout_ref[0]).astype(o_ref.dtype)
        for h in hs:
            st_scr[h] = s_cur[h]
        return carry

    lax.fori_loop(0, LB_G // (TILE * TILES_PER_ITER), tile_body, 0)

    @pl.when(j == nj - 1)
    def _():
        s_ref[0] = st_scr[...]
        cst_ref[0] = x_ref[pl.ds(LB_G - SUBLANES, SUBLANES), :QKV]


def _gdn_prompt(layer, proj, w_gconv, gprm, g_gout):
    nj = L_P // LB_G
    return pl.pallas_call(
        _gdn_p_kernel,
        grid=(NB_P, nj),
        in_specs=[
            pl.BlockSpec((LB_G, COL_GLU), lambda b, j: (b * nj + j, 0)),
            pl.BlockSpec((SUBLANES, COL_GLU),
                         lambda b, j: (jnp.maximum((b * nj + j) * (LB_G // SUBLANES) - 1, 0), 0)),
            pl.BlockSpec((LB_G, LANES), lambda b, j: (b * nj + j, COL_BA // LANES)),
            pl.BlockSpec((1, GCONV, QKV), lambda b, j: (layer, 0, 0)),
            pl.BlockSpec((1, SUBLANES, LANES), lambda b, j: (layer, 0, 0)),
            pl.BlockSpec((1, 1, DH), lambda b, j: (layer, 0, 0)),
        ],
        out_specs=[
            pl.BlockSpec((LB_G, GW), lambda b, j: (b * nj + j, 0)),
            pl.BlockSpec((1, NH, DH, DH), lambda b, j: (b, 0, 0, 0)),
            pl.BlockSpec((1, SUBLANES, QKV), lambda b, j: (b, 0, 0)),
        ],
        out_shape=[
            jax.ShapeDtypeStruct((T_P, GW), bf16),
            jax.ShapeDtypeStruct((NB_P, NH, DH, DH), f32),
            jax.ShapeDtypeStruct((NB_P, SUBLANES, QKV), f32),
        ],
        scratch_shapes=[pltpu.VMEM((LB_G, GW), f32), pltpu.VMEM((LB_G, GW), f32),
                        pltpu.VMEM((LB_G, GW), f32), pltpu.VMEM((2, LB_G, LANES), f32),
                        pltpu.VMEM((NH, DH, DH), f32)],
        compiler_params=_cparams(("parallel", "arbitrary")),
        name=f"gdn_prompt_{layer}",
    )(proj, proj, proj, w_gconv, gprm, g_gout)


BT_S = TILE // TPAD


def _gdn_s_kernel(x_ref, ba_ref, cin_ref, s0_ref, wc_ref, prm_ref, gout_ref,
                  o_ref, s_ref, cst_ref, st_scr):
    x = x_ref[:, :QKV]
    st_scr[...] = jnp.zeros_like(st_scr)
    for b in range(BT_S):
        st_scr[pl.ds((b + 1) * TPAD - (GCONV - 1), GCONV - 1), :] = cin_ref[0, b]
        cst_ref[b] = x_ref[pl.ds(b * TPAD + L_S - (GCONV - 1), GCONV - 1), :QKV]
    st = st_scr[...]
    trow = lax.broadcasted_iota(jnp.int32, (TILE, 1), 0) % TPAD
    xc = x * wc_ref[0, pl.ds(GCONV - 1, 1), :]
    for s in range(1, GCONV):
        xs = jnp.where(trow >= s, pltpu.roll(x, s, 0), pltpu.roll(st, TILE - TPAD + s, 0))
        xc = xc + xs * wc_ref[0, pl.ds(GCONV - 1 - s, 1), :]
    valid = trow < L_S
    qs, ks, vs, beta, g = _gdn_features(xc, ba_ref[...], prm_ref.at[0], valid)

    nsq = int(math.ceil(math.log2(L_S))) - 1
    gc_t, gt_t = _gate_cumsums(g, TPAD)
    hs = range(NH)
    gcs = [gc_t[:, NH + h:NH + h + 1] for h in hs]
    gts = [gt_t[:, NH + h:NH + h + 1] for h in hs]
    gc_tt = gc_t.T
    grows = [gc_tt[NH + h:NH + h + 1, :] for h in hs]
    us, ws, q_decs, k_decs, qks = _delta_prepare(qs, ks, vs, [beta[:, h:h + 1] for h in hs], gcs, grows, gts,
                                                 TPAD, nsq)
    for h in hs:
        egt = jnp.exp(gts[h])
        vn, qs_ = [], []
        for b in range(BT_S):
            rr = slice(b * TPAD, (b + 1) * TPAD)
            s0 = s0_ref[0, b, h]
            wsb = _dot(jnp.concatenate([ws[h][rr], q_decs[h][rr]], axis=0), s0)
            vn_b = us[h][rr] - wsb[:TPAD]
            vn.append(vn_b)
            qs_.append(wsb[TPAD:])
            s_ref[b, h] = s0 * egt[b * TPAD:b * TPAD + 1, :] + _dot_tn(k_decs[h][rr], vn_b)
        o = jnp.concatenate(qs_, axis=0) + _dot(qks[h], jnp.concatenate(vn, axis=0))
        z = x_ref[:, pl.ds(QKV + h * DH, DH)]
        o_ref[:, h * DH:(h + 1) * DH] = _gated_out(o, z, gout_ref[0]).astype(o_ref.dtype)


def _gdn_sample(layer, proj, conv_in, s0, w_gconv, gprm, g_gout):
    row0 = T_P // TILE
    return pl.pallas_call(
        _gdn_s_kernel,
        grid=(T_S // TILE,),
        in_specs=[
            pl.BlockSpec((TILE, COL_GLU), lambda i: (row0 + i, 0)),
            pl.BlockSpec((TILE, LANES), lambda i: (row0 + i, COL_BA // LANES)),
            pl.BlockSpec((1, BT_S, GCONV - 1, QKV), lambda i: (layer, i, 0, 0)),
            pl.BlockSpec((1, BT_S, NH, DH, DH), lambda i: (layer, i, 0, 0, 0)),
            pl.BlockSpec((1, GCONV, QKV), lambda i: (layer, 0, 0)),
            pl.BlockSpec((1, SUBLANES, LANES), lambda i: (layer, 0, 0)),
            pl.BlockSpec((1, 1, DH), lambda i: (layer, 0, 0)),
        ],
        out_specs=[
            pl.BlockSpec((TILE, GW), lambda i: (i, 0)),
            pl.BlockSpec((BT_S, NH, DH, DH), lambda i: (i, 0, 0, 0)),
            pl.BlockSpec((BT_S, GCONV - 1, QKV), lambda i: (i, 0, 0)),
        ],
        out_shape=[
            jax.ShapeDtypeStruct((T_S, GW), bf16),
            jax.ShapeDtypeStruct((NB_S, NH, DH, DH), f32),
            jax.ShapeDtypeStruct((NB_S, GCONV - 1, QKV), f32),
        ],
        scratch_shapes=[pltpu.VMEM((TILE, QKV), f32)],
        compiler_params=_cparams(("parallel",)),
        name=f"gdn_sample_{layer}",
    )(proj, proj, conv_in, s0, w_gconv, gprm, g_gout)


def _group_ln_silu(y, gl_ref, bl_ref):
    outs = []
    for gi in range(CGROUPS):
        w = CW // CGROUPS
        yg = y[:, gi * w:(gi + 1) * w]
        mu = jnp.mean(yg, axis=-1, keepdims=True)
        d = yg - mu
        var = jnp.mean(d * d, axis=-1, keepdims=True)
        outs.append(d * lax.rsqrt(var + LN_EPS))
    yn = jnp.concatenate(outs, axis=1) * gl_ref[0] + bl_ref[0]
    return _silu(yn)


def _conf_p_kernel(x_ref, halo_ref, w_ref, b_ref, gl_ref, bl_ref, o_ref, hst_ref, xp_scr, sh_scr):
    j = pl.program_id(1)
    nj = pl.num_programs(1)
    h = x_ref[:, :CW] * jax.nn.sigmoid(x_ref[:, CW:])
    hh = halo_ref[:, :CW] * jax.nn.sigmoid(halo_ref[:, CW:])
    xp_scr[pl.ds(0, HIST_C), :] = jnp.where(j > 0, hh, 0.0)
    xp_scr[pl.ds(HIST_C, LB_C), :] = h
    n_sh = HIST_C + LB_C - SUBLANES
    for r in range(1, SUBLANES):
        sh_scr[r - 1] = xp_scr[pl.ds(r, n_sh), :]
    off = HIST_C - (CK - 1)
    y = None
    for t in range(CK):
        a, r = divmod(off + t, SUBLANES)
        rows = pl.ds(a * SUBLANES, LB_C)
        win = xp_scr[rows, :] if r == 0 else sh_scr[r - 1, rows, :]
        term = win * w_ref[0, pl.ds(t, 1), :]
        y = term if y is None else y + term
    y = y + b_ref[0]
    o_ref[...] = _group_ln_silu(y, gl_ref, bl_ref).astype(o_ref.dtype)

    @pl.when(j == nj - 1)
    def _():
        hst_ref[0] = h[LB_C - HIST_C:, :]


def _conf_prompt(layer, proj, w_dw, b_dw, g_ln, b_ln):
    nj = L_P // LB_C
    cb = COL_GLU // (2 * CW)
    return pl.pallas_call(
        _conf_p_kernel,
        grid=(NB_P, nj),
        in_specs=[
            pl.BlockSpec((LB_C, 2 * CW), lambda b, j: (b * nj + j, cb)),
            pl.BlockSpec((HIST_C, 2 * CW),
                         lambda b, j: (jnp.maximum((b * nj + j) * (LB_C // HIST_C) - 1, 0), cb)),
            pl.BlockSpec((1, CK, CW), lambda b, j: (layer, 0, 0)),
            pl.BlockSpec((1, 1, CW), lambda b, j: (layer, 0, 0)),
            pl.BlockSpec((1, 1, CW), lambda b, j: (layer, 0, 0)),
            pl.BlockSpec((1, 1, CW), lambda b, j: (layer, 0, 0)),
        ],
        out_specs=[
            pl.BlockSpec((LB_C, CW), lambda b, j: (b * nj + j, 0)),
            pl.BlockSpec((1, HIST_C, CW), lambda b, j: (b, 0, 0)),
        ],
        out_shape=[
            jax.ShapeDtypeStruct((T_P, CW), bf16),
            jax.ShapeDtypeStruct((NB_P, HIST_C, CW), f32),
        ],
        scratch_shapes=[pltpu.VMEM((HIST_C + LB_C, CW), f32),
                        pltpu.VMEM((SUBLANES - 1, HIST_C + LB_C - SUBLANES, CW), f32)],
        compiler_params=_cparams(("parallel", "arbitrary")),
        name=f"conf_prompt_{layer}",
    )(proj, proj, w_dw, b_dw, g_ln, b_ln)


BT_C = 16


def _conf_s_kernel(x_ref, hin_ref, wh_ref, wn_ref, b_ref, gl_ref, bl_ref, o_ref, hst_ref, y_scr):
    h = x_ref[:, :CW] * jax.nn.sigmoid(x_ref[:, CW:])
    trow = lax.broadcasted_iota(jnp.int32, (TPAD, 1), 0)
    nh = CK - 1
    for b in range(BT_C):
        hist = hin_ref[0, b]
        h8 = h[b * TPAD:(b + 1) * TPAD, :]
        y8 = jnp.zeros((TPAD, CW), f32)
        for t in range(L_S):
            yt = (jnp.sum(hist * wh_ref[0, t], axis=0, keepdims=True)
                  + jnp.sum(h8 * wn_ref[0, t], axis=0, keepdims=True))
            y8 = jnp.where(trow == t, yt, y8)
        y_scr[pl.ds(b * TPAD, TPAD), :] = y8
        hst_ref[b, pl.ds(0, nh - L_S), :] = hin_ref[0, b, pl.ds(L_S, nh - L_S), :]
        hst_ref[b, pl.ds(nh - L_S, L_S), :] = h8[:L_S, :]
    y = y_scr[...] + b_ref[0]
    o_ref[...] = _group_ln_silu(y, gl_ref, bl_ref).astype(o_ref.dtype)


def _conf_sample(layer, proj, hist_in, w_hist, w_new, b_dw, g_ln, b_ln):
    row0 = T_P // (BT_C * TPAD)
    cb = COL_GLU // (2 * CW)
    return pl.pallas_call(
        _conf_s_kernel,
        grid=(NB_S // BT_C,),
        in_specs=[
            pl.BlockSpec((BT_C * TPAD, 2 * CW), lambda i: (row0 + i, cb)),
            pl.BlockSpec((1, BT_C, CK - 1, CW), lambda i: (layer, i, 0, 0)),
            pl.BlockSpec((1, L_S, CK - 1, CW), lambda i: (layer, 0, 0, 0)),
            pl.BlockSpec((1, L_S, TPAD, CW), lambda i: (layer, 0, 0, 0)),
            pl.BlockSpec((1, 1, CW), lambda i: (layer, 0, 0)),
            pl.BlockSpec((1, 1, CW), lambda i: (layer, 0, 0)),
            pl.BlockSpec((1, 1, CW), lambda i: (layer, 0, 0)),
        ],
        out_specs=[
            pl.BlockSpec((BT_C * TPAD, CW), lambda i: (i, 0)),
            pl.BlockSpec((BT_C, CK - 1, CW), lambda i: (i, 0, 0)),
        ],
        out_shape=[
            jax.ShapeDtypeStruct((T_S, CW), bf16),
            jax.ShapeDtypeStruct((NB_S, CK - 1, CW), f32),
        ],
        scratch_shapes=[pltpu.VMEM((BT_C * TPAD, CW), f32)],
        compiler_params=_cparams(("parallel",)),
        name=f"conf_sample_{layer}",
    )(proj, hist_in, w_hist, w_new, b_dw, g_ln, b_ln)


def _out_kernel(moe, x_ref, oap, obp, oas, obs, w_ref, g_ref,
                g1p, scp, shp, g1s, scs, shs, *rest):
    if moe:
        wr_ref, br_ref, xo_ref, h2_ref, rt_ref = rest
        hf_scr = h2_ref
    else:
        xo_ref, h2_ref, hf_scr = rest
    i = pl.program_id(0)

    def attn(oa, ob):
        return (jnp.dot(oa[...], w_ref[0, :GW, :], preferred_element_type=f32)
                + jnp.dot(ob[...], w_ref[0, GW:, :], preferred_element_type=f32))

    @pl.when(i < NT_P)
    def _():
        xn = x_ref[...] + g1p[0] * attn(oap, obp)
        xo_ref[...] = xn
        hf_scr[...] = _rms(xn, g_ref[0]) * (1.0 + scp[0]) + shp[0]

    @pl.when(i >= NT_P)
    def _():
        hf_scr[...] = attn(oas, obs)

        def body(rows, b):
            m = pl.ds(b, 1)
            xn = x_ref[rows, :] + g1s[m, :] * hf_scr[rows, :]
            xo_ref[rows, :] = xn
            hf_scr[rows, :] = _rms(xn, g_ref[0]) * (1.0 + scs[m, :]) + shs[m, :]
        _sample_rows(body)

    hf = hf_scr[...]
    if not moe:
        h2_ref[...] = hf.astype(h2_ref.dtype)
    if moe:
        logits = _dot_hi(hf, wr_ref[0]) + br_ref[0]
        lane = lax.broadcasted_iota(jnp.int32, logits.shape, 1)
        ex = jnp.exp(logits - jnp.max(logits, axis=-1, keepdims=True))
        probs = ex / jnp.sum(ex, axis=-1, keepdims=True)
        m1 = jnp.max(probs, axis=-1, keepdims=True)
        i1 = jnp.min(jnp.where(probs == m1, lane, LANES), axis=-1, keepdims=True)
        rest_p = jnp.where(lane == i1, -1.0, probs)
        m2 = jnp.max(rest_p, axis=-1, keepdims=True)
        i2 = jnp.min(jnp.where(rest_p == m2, lane, LANES), axis=-1, keepdims=True)
        den = m1 + m2
        rt_ref[...] = jnp.where(lane == 0, i1.astype(f32),
                                jnp.where(lane == 1, i2.astype(f32),
                                          jnp.where(lane == 2, m1 / den,
                                                    jnp.where(lane == 3, m2 / den, 0.0))))


def _out_proj(layer, moe, x, oa_p, ob_p, oa_s, ob_s, w_out, g2, modp, mods, wr=None, br=None):
    idx = layer // 2
    in_specs = [
        pl.BlockSpec((TM, D), lambda i: (i, 0)),
        pl.BlockSpec((TM, GW), lambda i: (jnp.minimum(i, NT_P - 1), 0)),
        pl.BlockSpec((TM, CW), lambda i: (jnp.minimum(i, NT_P - 1), 0)),
        pl.BlockSpec((TM, GW), lambda i: (0, 0)),
        pl.BlockSpec((TM, CW), lambda i: (0, 0)),
        pl.BlockSpec((1, D, D), lambda i: (layer, 0, 0)),
        pl.BlockSpec((1, 1, D), lambda i: (layer, 0, 0)),
        _mod_specs(G1, 1)[0], _mod_specs(SC2, 1)[0], _mod_specs(SH2, 1)[0],
        _mod_specs(G1, 1)[1], _mod_specs(SC2, 1)[1], _mod_specs(SH2, 1)[1],
    ]
    args = [x, oa_p, ob_p, oa_s, ob_s, w_out, g2, modp, modp, modp, mods, mods, mods]
    out_specs = [pl.BlockSpec((TM, D), lambda i: (i, 0)), pl.BlockSpec((TM, D), lambda i: (i, 0))]
    out_shape = [jax.ShapeDtypeStruct((T_ALL, D), f32),
                 jax.ShapeDtypeStruct((T_ALL, D), f32 if moe else bf16)]
    if moe:
        in_specs += [pl.BlockSpec((1, D, LANES), lambda i: (idx, 0, 0)),
                     pl.BlockSpec((1, 1, LANES), lambda i: (idx, 0, 0))]
        args += [wr, br]
        out_specs.append(pl.BlockSpec((TM, LANES), lambda i: (i, 0)))
        out_shape.append(jax.ShapeDtypeStruct((T_ALL, LANES), f32))
    return pl.pallas_call(
        functools.partial(_out_kernel, moe),
        grid=(NT,),
        in_specs=in_specs,
        out_specs=out_specs,
        out_shape=out_shape,
        scratch_shapes=[] if moe else [pltpu.VMEM((TM, D), f32)],
        compiler_params=_cparams(("parallel",)),
        name=f"out_proj_{layer}",
    )(*args)


def _residual(i, x_ref, f_ref, g2p, g2s, o_ref, gf_ref, os_ref=None):
    os_ref = o_ref if os_ref is None else os_ref

    def fin(v):
        return v if gf_ref is None else _rms(v, gf_ref[...])

    @pl.when(i < NT_P)
    def _():
        o_ref[...] = fin(x_ref[...] + g2p[0] * f_ref[...])

    @pl.when(i >= NT_P)
    def _():
        def body(rows, b):
            os_ref[rows, :] = fin(x_ref[rows, :] + g2s[pl.ds(b, 1), :] * f_ref[rows, :])
        _sample_rows(body)


def _ffn_kernel(h_ref, wg_ref, wu_ref, wd_ref, x_ref, g2p, g2s, o_ref, acc_scr):
    i = pl.program_id(0)
    h = h_ref[...]
    for c in range(FF_D // TF_D):
        cols = slice(c * TF_D, (c + 1) * TF_D)
        gate = jnp.dot(h, wg_ref[0, :, cols], preferred_element_type=f32)
        up = jnp.dot(h, wu_ref[0, :, cols], preferred_element_type=f32)
        part = jnp.dot((_silu(gate) * up).astype(bf16), wd_ref[0, cols, :], preferred_element_type=f32)
        if c == 0:
            acc_scr[...] = part
        else:
            acc_scr[...] += part
    _residual(i, x_ref, acc_scr, g2p, g2s, o_ref, None)


def _ffn_dense(layer, h2, x, wg, wu, wd, modp, mods):
    idx = layer // 2
    mp, ms = _mod_specs(G2, 1)
    return pl.pallas_call(
        _ffn_kernel,
        grid=(NT,),
        in_specs=[
            pl.BlockSpec((TM, D), lambda i: (i, 0)),
            _resident((1, D, FF_D), lambda i: (idx, 0, 0)),
            _resident((1, D, FF_D), lambda i: (idx, 0, 0)),
            _resident((1, FF_D, D), lambda i: (idx, 0, 0)),
            pl.BlockSpec((TM, D), lambda i: (i, 0)),
            mp, ms,
        ],
        out_specs=pl.BlockSpec((TM, D), lambda i: (i, 0)),
        out_shape=jax.ShapeDtypeStruct((T_ALL, D), f32),
        scratch_shapes=[pltpu.VMEM((TM, D), f32)],
        compiler_params=_cparams(("parallel",)),
        name=f"ffn_dense_{layer}",
    )(h2, wg, wu, wd, x, modp, mods)


def _gather_rows(idx_ref, base, stride, src_hbm, dst, sem):
    def issue(r, carry):
        row = idx_ref[base + stride * r]
        pltpu.make_async_copy(src_hbm.at[pl.ds(row, 1), :], dst.at[pl.ds(r, 1), :], sem).start()
        return carry
    lax.fori_loop(0, MOE_BLK, issue, 0, unroll=8)


def _wait_rows(src_hbm, dst, sem):
    pltpu.make_async_copy(src_hbm.at[pl.ds(0, MOE_BLK), :], dst, sem).wait()


def _issue_rows(idx_ref, base, src_hbm, dst, sem, r0, n):
    for u in range(n):
        row = idx_ref[base + r0 + u]
        pltpu.make_async_copy(src_hbm.at[pl.ds(row, 1), :], dst.at[pl.ds(r0 + u, 1), :], sem).start()


def _expert_kernel(be_ref, tok_ref, nv_ref, h_hbm, wg_ref, wu_ref, wd_ref,
                   y_ref, xg_scr, xb_scr, sem):
    m = pl.program_id(0)
    f = pl.program_id(1)
    nv = nv_ref[0]
    active = m < nv
    slot = m % 2
    nxt = (m + 1) * MOE_BLK

    @pl.when(jnp.logical_and(f == 0, m == 0))
    def _():
        _gather_rows(tok_ref, 0, 1, h_hbm, xg_scr.at[0], sem.at[0])

    @pl.when(jnp.logical_and(f == 0, m <= nv))
    def _():
        _wait_rows(h_hbm, xg_scr.at[slot], sem.at[slot])

    @pl.when(jnp.logical_and(f == 0, active))
    def _():
        xb_scr[...] = xg_scr[slot].astype(bf16)
        _issue_rows(tok_ref, nxt, h_hbm, xg_scr.at[1 - slot], sem.at[1 - slot],
                    GATHER_CHUNK * (FF_E // TF_E), GATHER_TAIL)

    @pl.when(f == 0)
    def _():
        y_ref[...] = jnp.zeros_like(y_ref)

    @pl.when(active)
    def _():
        _issue_rows(tok_ref, nxt, h_hbm, xg_scr.at[1 - slot], sem.at[1 - slot], f * GATHER_CHUNK, GATHER_CHUNK)
        xb = xb_scr[...]
        a = _silu(_dot(xb, wg_ref[0, 0])) * _dot(xb, wu_ref[0, 0])
        y_ref[...] += _dot(a, wd_ref[0, 0])


def _experts(idx, h2, block_e, slot_tok, n_valid, wg, wu, wd):
    nf = FF_E // TF_E

    def wmap_up(m, f, be, tok, nv):
        return (idx, be[m], 0, jnp.where(m < nv[0], f, nf - 1))

    def wmap_down(m, f, be, tok, nv):
        return (idx, be[m], jnp.where(m < nv[0], f, nf - 1), 0)

    grid_spec = pltpu.PrefetchScalarGridSpec(
        num_scalar_prefetch=3,
        grid=(N_BLK + 1, nf),
        in_specs=[
            pl.BlockSpec(memory_space=pl.ANY),
            pl.BlockSpec((1, 1, D, TF_E), wmap_up),
            pl.BlockSpec((1, 1, D, TF_E), wmap_up),
            pl.BlockSpec((1, 1, TF_E, D), wmap_down),
        ],
        out_specs=pl.BlockSpec((MOE_BLK, D), lambda m, f, be, tok, nv: (m, 0)),
        scratch_shapes=[pltpu.VMEM((2, MOE_BLK, D), f32), pltpu.VMEM((MOE_BLK, D), bf16),
                        pltpu.SemaphoreType.DMA((2,))],
    )
    return pl.pallas_call(
        _expert_kernel,
        grid_spec=grid_spec,
        out_shape=jax.ShapeDtypeStruct(((N_BLK + 1) * MOE_BLK, D), f32),
        compiler_params=_cparams(("arbitrary", "arbitrary")),
        name=f"experts_{idx}",
    )(block_e, slot_tok, n_valid, h2, wg, wu, wd)


def _combine_kernel(final, dest_ref, y_hbm, x_ref, rt_ref, g2p, g2s, *rest):
    if final:
        gf_ref, o_ref, os_ref, g_scr, f_scr, sem = rest
    else:
        o_ref, g_scr, f_scr, sem = rest
        gf_ref = os_ref = None
    i = pl.program_id(0)
    slot = i % 2

    def gather(tile, sl):
        for k in range(2):
            _gather_rows(dest_ref, 2 * tile * TM + k, 2, y_hbm, g_scr.at[sl, k], sem.at[sl, k])

    @pl.when(i == 0)
    def _():
        gather(0, 0)

    @pl.when(i + 1 < pl.num_programs(0))
    def _():
        gather(i + 1, 1 - slot)

    for k in range(2):
        _wait_rows(y_hbm, g_scr.at[slot, k], sem.at[slot, k])
    f_scr[...] = rt_ref[:, 2:3] * g_scr[slot, 0] + rt_ref[:, 3:4] * g_scr[slot, 1]
    _residual(i, x_ref, f_scr, g2p, g2s, o_ref, gf_ref, os_ref)


def _combine(final, dest, yb, x, rt, modp, mods, g_final):
    mp, ms = _mod_specs(G2, 1)
    wrap = lambda spec: pl.BlockSpec(spec.block_shape, lambda i, d, _f=spec.index_map: _f(i))
    in_specs = [pl.BlockSpec(memory_space=pl.ANY),
                pl.BlockSpec((TM, D), lambda i, d: (i, 0)),
                pl.BlockSpec((TM, LANES), lambda i, d: (i, 0)), wrap(mp), wrap(ms)]
    args = [yb, x, rt, modp, mods]
    if final:
        in_specs.append(pl.BlockSpec((1, D), lambda i, d: (0, 0)))
        args.append(g_final)
        out_specs = [pl.BlockSpec((TM, D), lambda i, d: (jnp.minimum(i, NT_P - 1), 0)),
                     pl.BlockSpec((T_S, D), lambda i, d: (0, 0))]
        out_shape = [jax.ShapeDtypeStruct((T_P, D), f32), jax.ShapeDtypeStruct((T_S, D), f32)]
    else:
        out_specs = pl.BlockSpec((TM, D), lambda i, d: (i, 0))
        out_shape = jax.ShapeDtypeStruct((T_ALL, D), f32)
    grid_spec = pltpu.PrefetchScalarGridSpec(
        num_scalar_prefetch=1,
        grid=(NT,),
        in_specs=in_specs,
        out_specs=out_specs,
        scratch_shapes=[pltpu.VMEM((2, 2, TM, D), f32), pltpu.VMEM((TM, D), f32),
                        pltpu.SemaphoreType.DMA((2, 2))],
    )
    return pl.pallas_call(
        functools.partial(_combine_kernel, final),
        grid_spec=grid_spec,
        out_shape=out_shape,
        compiler_params=_cparams(("arbitrary",)),
        name="combine_final" if final else "combine",
    )(dest, *args)


def _route(rt):
    e = rt[:, :2].astype(jnp.int32).reshape(N_ASG)
    onehot = (e[:, None] == jnp.arange(NE, dtype=jnp.int32)[None, :]).astype(jnp.int32)
    csum = jnp.cumsum(onehot, axis=0)
    counts = csum[-1]
    rank = jnp.sum(onehot * (csum - 1), axis=1)
    padded = (counts + MOE_BLK - 1) // MOE_BLK * MOE_BLK
    pad_end = jnp.cumsum(padded)
    pad_start = pad_end - padded
    dest = (jnp.sum(onehot * pad_start[None, :], axis=1) + rank).astype(jnp.int32)
    n_slots = (N_BLK + 1) * MOE_BLK
    tok = jnp.arange(N_ASG, dtype=jnp.int32) // 2
    slot_tok = jnp.zeros((n_slots,), jnp.int32).at[dest].set(tok, unique_indices=True)
    n_valid = (pad_end[-1] // MOE_BLK).astype(jnp.int32)
    blk = jnp.minimum(jnp.arange(N_BLK + 1, dtype=jnp.int32), n_valid - 1)
    block_e = jnp.minimum(jnp.searchsorted(pad_end, blk * MOE_BLK, side="right"), NE - 1).astype(jnp.int32)
    return dest, slot_tok, block_e, n_valid.reshape(1)


def kernel(x_prompt, x_sample, c_prompt, c_sample, state_gdn, state_gdn_conv, state_conf_conv, w_ada, b_ada, g_norm1, g_norm2, w_in, w_gdn_conv, a_log, dt_bias, g_gdn_out, w_conf_dw, b_conf_dw, g_conf_ln, b_conf_ln, w_out, w_ff_gate, w_ff_up, w_ff_down, w_router, b_router, w_exp_gate, w_exp_up, w_exp_down, g_final):
    xs_pad = jnp.pad(x_sample, ((0, 0), (0, TPAD - L_S), (0, 0)))
    x = jnp.concatenate([x_prompt.reshape(T_P, D), xs_pad.reshape(T_S, D)], axis=0)
    c_all = jnp.concatenate([c_prompt, c_sample], axis=0)
    o1, o2, o4 = QKV + GW, QKV + GW + 2 * NH, QKV + GW + 2 * NH + 2 * CW
    w_cat = jnp.concatenate([w_in[:, :, :o1], w_in[:, :, o2:o4], w_in[:, :, o1:o2],
                             jnp.zeros((DEPTH, D, PROJ_W - o4), f32)], axis=-1).astype(bf16)
    lane_pad = ((0, 0), (NH, LANES - 2 * NH))
    gprm = jnp.stack([jnp.pad(a_log, lane_pad), jnp.pad(dt_bias, lane_pad)], axis=1)
    gprm = jnp.pad(gprm, ((0, 0), (0, SUBLANES - 2), (0, 0)))
    wr_pad = jnp.pad(w_router, ((0, 0), (0, 0), (0, LANES - NE)))
    br_pad = jnp.pad(b_router, ((0, 0), (0, LANES - NE)), constant_values=-1e30).reshape(-1, 1, LANES)
    w_shift = jnp.stack([jnp.pad(w_conf_dw, ((0, 0), (t, TPAD - 1 - t), (0, 0))) for t in range(L_S)], axis=1)
    w_hist, w_new = w_shift[:, :, :CK - 1], w_shift[:, :, CK - 1:]
    w_out = w_out.astype(bf16)
    w_ff_gate, w_ff_up, w_ff_down = (w.astype(bf16) for w in (w_ff_gate, w_ff_up, w_ff_down))
    g1 = g_norm1.reshape(DEPTH, 1, D)
    g2 = g_norm2.reshape(DEPTH, 1, D)
    gout = g_gdn_out.reshape(DEPTH, 1, DH)
    b_dw = b_conf_dw.reshape(DEPTH, 1, CW)
    g_ln = g_conf_ln.reshape(DEPTH, 1, CW)
    b_ln = b_conf_ln.reshape(DEPTH, 1, CW)

    mod = _ada(c_all, w_ada, b_ada)

    sp_l, cp_l, fp_l, ss_l, cs_l, fs_l = [], [], [], [], [], []
    for layer in range(DEPTH):
        modp = mod[layer, :NB_P].reshape(NB_P, 1, N_MOD * D)
        mods = mod[layer, NB_P:]
        proj = _in_proj(layer, x, g1, modp, mods, w_cat)
        oa_p, s_p, c_p = _gdn_prompt(layer, proj, w_gdn_conv, gprm, gout)
        oa_s, s_s, c_s = _gdn_sample(layer, proj, state_gdn_conv, state_gdn, w_gdn_conv, gprm, gout)
        ob_p, f_p = _conf_prompt(layer, proj, w_conf_dw, b_dw, g_ln, b_ln)
        ob_s, f_s = _conf_sample(layer, proj, state_conf_conv, w_hist, w_new, b_dw, g_ln, b_ln)
        sp_l.append(s_p)
        cp_l.append(c_p[:, TPAD - (GCONV - 1):, :])
        fp_l.append(f_p[:, HIST_C - (CK - 1):, :])
        ss_l.append(s_s)
        cs_l.append(c_s)
        fs_l.append(f_s)
        if layer % 2 == 0:
            x, h2 = _out_proj(layer, False, x, oa_p, ob_p, oa_s, ob_s, w_out, g2, modp, mods)
            x = _ffn_dense(layer, h2, x, w_ff_gate, w_ff_up, w_ff_down, modp, mods)
        else:
            x, h2, rt = _out_proj(layer, True, x, oa_p, ob_p, oa_s, ob_s, w_out, g2, modp, mods, wr_pad, br_pad)
            dest, slot_tok, block_e, n_valid = _route(rt)
            yb = _experts(layer // 2, h2, block_e, slot_tok, n_valid, w_exp_gate, w_exp_up, w_exp_down)
            final = layer == DEPTH - 1
            x = _combine(final, dest, yb, x, rt, modp, mods, g_final.reshape(1, D))

    y_p, y_s = x
    y_prompt = y_p.reshape(NB_P, L_P, D)
    y_sample = y_s.reshape(NB_S, TPAD, D)[:, :L_S, :]
    return (y_prompt, y_sample, jnp.stack(sp_l), jnp.stack(cp_l), jnp.stack(fp_l),
            jnp.stack(ss_l), jnp.stack(cs_l), jnp.stack(fs_l))
```

```python
import functools
import math

import jax
import jax.numpy as jnp
from jax import lax
from jax.experimental import pallas as pl
from jax.experimental.pallas import tpu as pltpu

f32 = jnp.float32
bf16 = jnp.bfloat16
HIGHEST = lax.Precision.HIGHEST

D = 1024
NB_P, L_P = 8, 2048
NB_S, L_S = 128, 4
DEPTH = 4
NH, DH = 4, 128
GW = NH * DH
QKV = 3 * GW
GCONV = 4
GCHUNK = 64
CW = D - GW
CGROUPS = 4
CK = 31
FF_D = 2816
NE = 8
FF_E = 3584
N_MOD = 6
RMS_EPS = 1e-6
LN_EPS = 1e-5

SUBLANES = 8
LANES = 128
VMEM_LIMIT = 56 * 1024 * 1024

TPAD = SUBLANES
T_P = NB_P * L_P
T_S = NB_S * TPAD
T_ALL = T_P + T_S
TM = 1024
NT = T_ALL // TM
NT_P = T_P // TM
TILES_PER_SEQ = L_P // TM
PROJ_W = 3200
COL_GLU = 2048
COL_BA = 3072
TN_IN = 640
HIST_C = 32
SH1, SC1, G1, SH2, SC2, G2 = range(N_MOD)

LB_G = 512
LB_C = 512
TILE = 128
TILES_PER_ITER = 4
TF_D = 1408
TF_E = 512
MOE_BLK = 1024
N_ASG = 2 * T_ALL
N_BLK = N_ASG // MOE_BLK + NE
GATHER_CHUNK = MOE_BLK // (FF_E // TF_E)
GATHER_TAIL = MOE_BLK - GATHER_CHUNK * (FF_E // TF_E)


def _cparams(sem):
    return pltpu.CompilerParams(dimension_semantics=sem, vmem_limit_bytes=VMEM_LIMIT)


def _dot(a, b):
    return jnp.dot(a.astype(bf16), b.astype(bf16), preferred_element_type=f32)


def _dot_hi(a, b):
    return jnp.dot(a, b, precision=HIGHEST, preferred_element_type=f32)


def _dot_nt(a, b):
    return lax.dot_general(a.astype(bf16), b.astype(bf16), (((1,), (1,)), ((), ())),
                           preferred_element_type=f32)


def _dot_nt_hi(a, b):
    return lax.dot_general(a, b, (((1,), (1,)), ((), ())), precision=HIGHEST,
                           preferred_element_type=f32)


def _dot_tn(a, b):
    return lax.dot_general(a, b, (((0,), (0,)), ((), ())), preferred_element_type=f32)


def _silu(x):
    return x * jax.nn.sigmoid(x)


def _rms(x, g):
    return x * lax.rsqrt(jnp.mean(x * x, axis=-1, keepdims=True) + RMS_EPS) * g


def _ada_kernel(c_ref, w_ref, b_ref, o_ref):
    a = _silu(c_ref[...])
    o_ref[0] = _dot(a, w_ref[0]) + b_ref[0]


def _ada(c_all, w_ada, b_ada):
    n = c_all.shape[0]
    tn = 1536
    return pl.pallas_call(
        _ada_kernel,
        grid=(DEPTH, N_MOD * D // tn),
        in_specs=[
            pl.BlockSpec((n, D), lambda l, j: (0, 0)),
            pl.BlockSpec((1, D, tn), lambda l, j: (l, 0, j)),
            pl.BlockSpec((1, 1, tn), lambda l, j: (l, 0, j)),
        ],
        out_specs=pl.BlockSpec((1, n, tn), lambda l, j: (l, 0, j)),
        out_shape=jax.ShapeDtypeStruct((DEPTH, n, N_MOD * D), f32),
        compiler_params=_cparams(("parallel", "parallel")),
        name="ada",
    )(c_all, w_ada, b_ada.reshape(DEPTH, 1, N_MOD * D))


def _mod_specs(k, ngrid):
    if ngrid == 1:
        return [pl.BlockSpec((1, 1, D), lambda i: (jnp.minimum(i // TILES_PER_SEQ, NB_P - 1), 0, k)),
                pl.BlockSpec((NB_S, D), lambda i: (0, k))]
    return [pl.BlockSpec((1, 1, D), lambda i, j: (jnp.minimum(i // TILES_PER_SEQ, NB_P - 1), 0, k)),
            pl.BlockSpec((NB_S, D), lambda i, j: (0, k))]


def _sample_rows(body):
    def step(b, carry):
        body(pl.ds(pl.multiple_of(b * TPAD, TPAD), TPAD), b)
        return carry
    lax.fori_loop(0, NB_S, step, 0)


def _in_kernel(x_ref, g_ref, shp, scp, shs, scs, w_ref, o_ref, h_scr, hf_scr):
    i = pl.program_id(0)

    @pl.when(i < NT_P)
    def _():
        h = _rms(x_ref[...], g_ref[0]) * (1.0 + scp[0]) + shp[0]
        h_scr[...] = h.astype(bf16)

    @pl.when(i >= NT_P)
    def _():
        def body(rows, b):
            hf_scr[rows, :] = (_rms(x_ref[rows, :], g_ref[0]) * (1.0 + scs[pl.ds(b, 1), :])
                               + shs[pl.ds(b, 1), :])
        _sample_rows(body)
        h_scr[...] = hf_scr[...].astype(bf16)

    for c in range(PROJ_W // TN_IN):
        cols = slice(c * TN_IN, (c + 1) * TN_IN)
        o_ref[:, cols] = jnp.dot(h_scr[...], w_ref[0, :, cols], preferred_element_type=f32)


def _resident(block_shape, index_map):
    return pl.BlockSpec(block_shape, index_map, pipeline_mode=pl.Buffered(1))


def _in_proj(layer, x, g1, modp, mods, w_cat):
    return pl.pallas_call(
        _in_kernel,
        grid=(NT,),
        in_specs=[
            pl.BlockSpec((TM, D), lambda i: (i, 0)),
            pl.BlockSpec((1, 1, D), lambda i: (layer, 0, 0)),
            *_mod_specs(SH1, 1)[:1], *_mod_specs(SC1, 1)[:1],
            *_mod_specs(SH1, 1)[1:], *_mod_specs(SC1, 1)[1:],
            _resident((1, D, PROJ_W), lambda i: (layer, 0, 0)),
        ],
        out_specs=pl.BlockSpec((TM, PROJ_W), lambda i: (i, 0)),
        out_shape=jax.ShapeDtypeStruct((T_ALL, PROJ_W), f32),
        scratch_shapes=[pltpu.VMEM((TM, D), bf16), pltpu.VMEM((TM, D), f32)],
        compiler_params=_cparams(("parallel",)),
        name=f"in_proj_{layer}",
    )(x, g1, modp, modp, mods, mods, w_cat)


def _tile_masks(blk):
    ri = lax.broadcasted_iota(jnp.int32, (TILE, TILE), 0)
    ci = lax.broadcasted_iota(jnp.int32, (TILE, TILE), 1)
    same = (ri // blk) == (ci // blk)
    incl = jnp.logical_and(same, ri >= ci)
    strict = jnp.logical_and(same, ri > ci)
    return same, incl, strict


def _gate_cumsums(g_all, blk):
    same, incl, _ = _tile_masks(blk)
    gc = _dot_hi(incl.astype(f32), g_all)
    if TILE == 2 * blk:
        row = lax.broadcasted_iota(jnp.int32, (TILE, 1), 0)
        gt = jnp.where(row < blk, gc[blk - 1:blk, :], gc[TILE - 1:TILE, :])
    else:
        gt = _dot_hi(same.astype(f32), g_all)
    return gc, gt


def _unit_lower_inverse(a_mats, blk, base, nsq):
    ri = lax.broadcasted_iota(jnp.int32, (TILE, TILE), 0)
    ci = lax.broadcasted_iota(jnp.int32, (TILE, TILE), 1)
    ps = [jnp.where((ri // base) == (ci // base), -a, 0.0) for a in a_mats]
    eye = (ri == ci).astype(f32)
    ts = [eye + p for p in ps]
    for _ in range(nsq):
        ps = [_dot(p, p) for p in ps]
        ts = [t + _dot(p, t) for p, t in zip(ps, ts)]
    b = base
    while b < blk:
        off = jnp.logical_and((ri // (2 * b)) == (ci // (2 * b)), (ri // b) != (ci // b))
        ms = [_dot(jnp.where(off, a, 0.0), t) for a, t in zip(a_mats, ts)]
        ts = [t - _dot(t, m) for t, m in zip(ts, ms)]
        b *= 2
    return ts


def _delta_prepare(qs, ks, vs, betas, gcs, grows, gts, blk, nsq):
    _, incl, strict = _tile_masks(blk)
    n = range(len(qs))
    dmats = [gc - gr for gc, gr in zip(gcs, grows)]
    decays = [jnp.where(incl, jnp.exp(jnp.where(incl, d, 0.0)), 0.0) for d in dmats]
    egcs = [jnp.exp(gc) for gc in gcs]
    kbs = [ks[i] * betas[i] for i in n]
    a_mats = [jnp.where(strict, _dot_nt(kbs[i], ks[i]) * decays[i], 0.0) for i in n]
    qks = [_dot_nt(qs[i], ks[i]) * decays[i] for i in n]
    xs = [jnp.concatenate([vs[i] * betas[i], kbs[i] * egcs[i]], axis=1) for i in n]
    tinvs = _unit_lower_inverse(a_mats, blk, SUBLANES, nsq)
    xs = [_dot(t, x) for t, x in zip(tinvs, xs)]
    us = [x[:, :DH] for x in xs]
    ws = [x[:, DH:] for x in xs]
    q_decs = [qs[i] * egcs[i] for i in n]
    k_decs = [ks[i] * jnp.exp(gts[i] - gcs[i]) for i in n]
    return us, ws, q_decs, k_decs, qks


def _gdn_features(xc, ba, prm_ref, valid):
    s = _silu(xc)
    if valid is not None:
        s = jnp.where(valid, s, 0.0)
    qs, ks, vs = [], [], []
    for h in range(NH):
        qh = s[:, h * DH:(h + 1) * DH]
        kh = s[:, GW + h * DH:GW + (h + 1) * DH]
        qs.append(qh * lax.rsqrt(jnp.sum(qh * qh, axis=-1, keepdims=True) + 1e-6) * (DH ** -0.5))
        ks.append(kh * lax.rsqrt(jnp.sum(kh * kh, axis=-1, keepdims=True) + 1e-6))
        vs.append(s[:, 2 * GW + h * DH:2 * GW + (h + 1) * DH])
    beta = jax.nn.sigmoid(ba)
    xs = ba + prm_ref[pl.ds(1, 1), :]
    softplus = jnp.maximum(xs, 0.0) + jnp.log1p(jnp.exp(-jnp.abs(xs)))
    g = -jnp.exp(prm_ref[pl.ds(0, 1), :]) * softplus
    if valid is not None:
        beta = jnp.where(valid, beta, 0.0)
        g = jnp.where(valid, g, 0.0)
    return qs, ks, vs, beta, g


def _gated_out(o, z, gout):
    return _rms(o, gout) * _silu(z)


def _gdn_p_kernel(x_ref, halo_ref, ba_ref, wc_ref, prm_ref, gout_ref,
                  o_ref, s_ref, cst_ref,
                  q_scr, k_scr, v_scr, bg_scr, st_scr):
    j = pl.program_id(1)
    nj = pl.num_programs(1)

    @pl.when(j == 0)
    def _():
        st_scr[...] = jnp.zeros_like(st_scr)

    x = x_ref[:, :QKV]
    halo = jnp.where(j > 0, halo_ref[:, :QKV], 0.0)
    xp = jnp.concatenate([halo, x], axis=0)
    xc = x * wc_ref[0, pl.ds(GCONV - 1, 1), :]
    for s in range(1, GCONV):
        xc = xc + pltpu.roll(xp, s, 0)[SUBLANES:, :] * wc_ref[0, pl.ds(GCONV - 1 - s, 1), :]
    qs, ks, vs, beta, g = _gdn_features(xc, ba_ref[...], prm_ref.at[0], None)
    for h in range(NH):
        q_scr[:, h * DH:(h + 1) * DH] = qs[h]
        k_scr[:, h * DH:(h + 1) * DH] = ks[h]
        v_scr[:, h * DH:(h + 1) * DH] = vs[h]
    bg_scr[0] = beta
    bg_scr[1] = g

    nsq = int(math.log2(SUBLANES)) - 1
    nblk = TILE // GCHUNK

    def tile_body(t, carry):
        hs = range(NH)
        cols = [slice(h * DH, (h + 1) * DH) for h in hs]
        rows, qs, ks, vs, betas, gcs, gts, grows = [], [], [], [], [], [], [], []
        for ti in range(TILES_PER_ITER):
            r = pl.ds(pl.multiple_of((t * TILES_PER_ITER + ti) * TILE, TILE), TILE)
            rows.append(r)
            beta_t = bg_scr[0, r, :]
            gc_t, gt_t = _gate_cumsums(bg_scr[1, r, :], GCHUNK)
            gc_tt = gc_t.T
            for h in hs:
                qs.append(q_scr[r, cols[h]])
                ks.append(k_scr[r, cols[h]])
                vs.append(v_scr[r, cols[h]])
                betas.append(beta_t[:, h:h + 1])
                gcs.append(gc_t[:, NH + h:NH + h + 1])
                gts.append(gt_t[:, NH + h:NH + h + 1])
                grows.append(gc_tt[NH + h:NH + h + 1, :])
        us, ws, q_decs, k_decs, qks = _delta_prepare(qs, ks, vs, betas, gcs, grows, gts, GCHUNK, nsq)
        s_cur = [st_scr[h] for h in hs]
        for ti in range(TILES_PER_ITER):
            ch = [ti * NH + h for h in hs]
            egts = [jnp.exp(gts[i]) for i in ch]
            vn = [[] for _ in hs]
            qs_ = [[] for _ in hs]
            for c in range(nblk):
                rr = slice(c * GCHUNK, (c + 1) * GCHUNK)
                wss = [_dot(jnp.concatenate([ws[ch[h]][rr], q_decs[ch[h]][rr]], axis=0), s_cur[h]) for h in hs]
                for h in hs:
                    vn[h].append(us[ch[h]][rr] - wss[h][:GCHUNK])
                    qs_[h].append(wss[h][GCHUNK:])
                s_cur = [s_cur[h] * egts[h][c * GCHUNK:c * GCHUNK + 1, :] + _dot_tn(k_decs[ch[h]][rr], vn[h][c])
                         for h in hs]
            os_ = [jnp.concatenate(qs_[h], axis=0) + _dot(qks[ch[h]], jnp.concatenate(vn[h], axis=0)) for h in hs]
            for h in hs:
                z = x_ref[rows[ti], pl.ds(QKV + h * DH, DH)]
                o_ref[rows[ti], cols[h]] = _gated_out(os_[h], z, gout_ref[0]).astype(o_ref.dtype)
        for h in hs:
            st_scr[h] = s_cur[h]
        return carry

    lax.fori_loop(0, LB_G // (TILE * TILES_PER_ITER), tile_body, 0)

    @pl.when(j == nj - 1)
    def _():
        s_ref[0] = st_scr[...]
        cst_ref[0] = x_ref[pl.ds(LB_G - SUBLANES, SUBLANES), :QKV]


def _gdn_prompt(layer, proj, w_gconv, gprm, g_gout):
    nj = L_P // LB_G
    return pl.pallas_call(
        _gdn_p_kernel,
        grid=(NB_P, nj),
        in_specs=[
            pl.BlockSpec((LB_G, COL_GLU), lambda b, j: (b * nj + j, 0)),
            pl.BlockSpec((SUBLANES, COL_GLU),
                         lambda b, j: (jnp.maximum((b * nj + j) * (LB_G // SUBLANES) - 1, 0), 0)),
            pl.BlockSpec((LB_G, LANES), lambda b, j: (b * nj + j, COL_BA // LANES)),
            pl.BlockSpec((1, GCONV, QKV), lambda b, j: (layer, 0, 0)),
            pl.BlockSpec((1, SUBLANES, LANES), lambda b, j: (layer, 0, 0)),
            pl.BlockSpec((1, 1, DH), lambda b, j: (layer, 0, 0)),
        ],
        out_specs=[
            pl.BlockSpec((LB_G, GW), lambda b, j: (b * nj + j, 0)),
            pl.BlockSpec((1, NH, DH, DH), lambda b, j: (b, 0, 0, 0)),
            pl.BlockSpec((1, SUBLANES, QKV), lambda b, j: (b, 0, 0)),
        ],
        out_shape=[
            jax.ShapeDtypeStruct((T_P, GW), bf16),
            jax.ShapeDtypeStruct((NB_P, NH, DH, DH), f32),
            jax.ShapeDtypeStruct((NB_P, SUBLANES, QKV), f32),
        ],
        scratch_shapes=[pltpu.VMEM((LB_G, GW), f32), pltpu.VMEM((LB_G, GW), f32),
                        pltpu.VMEM((LB_G, GW), f32), pltpu.VMEM((2, LB_G, LANES), f32),
                        pltpu.VMEM((NH, DH, DH), f32)],
        compiler_params=_cparams(("parallel", "arbitrary")),
        name=f"gdn_prompt_{layer}",
    )(proj, proj, proj, w_gconv, gprm, g_gout)


BT_S = TILE // TPAD


def _gdn_s_kernel(x_ref, ba_ref, cin_ref, s0_ref, wc_ref, prm_ref, gout_ref,
                  o_ref, s_ref, cst_ref, st_scr):
    x = x_ref[:, :QKV]
    st_scr[...] = jnp.zeros_like(st_scr)
    for b in range(BT_S):
        st_scr[pl.ds((b + 1) * TPAD - (GCONV - 1), GCONV - 1), :] = cin_ref[0, b]
        cst_ref[b] = x_ref[pl.ds(b * TPAD + L_S - (GCONV - 1), GCONV - 1), :QKV]
    st = st_scr[...]
    trow = lax.broadcasted_iota(jnp.int32, (TILE, 1), 0) % TPAD
    xc = x * wc_ref[0, pl.ds(GCONV - 1, 1), :]
    for s in range(1, GCONV):
        xs = jnp.where(trow >= s, pltpu.roll(x, s, 0), pltpu.roll(st, TILE - TPAD + s, 0))
        xc = xc + xs * wc_ref[0, pl.ds(GCONV - 1 - s, 1), :]
    valid = trow < L_S
    qs, ks, vs, beta, g = _gdn_features(xc, ba_ref[...], prm_ref.at[0], valid)

    nsq = int(math.ceil(math.log2(L_S))) - 1
    gc_t, gt_t = _gate_cumsums(g, TPAD)
    hs = range(NH)
    gcs = [gc_t[:, NH + h:NH + h + 1] for h in hs]
    gts = [gt_t[:, NH + h:NH + h + 1] for h in hs]
    gc_tt = gc_t.T
    grows = [gc_tt[NH + h:NH + h + 1, :] for h in hs]
    us, ws, q_decs, k_decs, qks = _delta_prepare(qs, ks, vs, [beta[:, h:h + 1] for h in hs], gcs, grows, gts,
                                                 TPAD, nsq)
    for h in hs:
        egt = jnp.exp(gts[h])
        vn, qs_ = [], []
        for b in range(BT_S):
            rr = slice(b * TPAD, (b + 1) * TPAD)
            s0 = s0_ref[0, b, h]
            wsb = _dot(jnp.concatenate([ws[h][rr], q_decs[h][rr]], axis=0), s0)
            vn_b = us[h][rr] - wsb[:TPAD]
            vn.append(vn_b)
            qs_.append(wsb[TPAD:])
            s_ref[b, h] = s0 * egt[b * TPAD:b * TPAD + 1, :] + _dot_tn(k_decs[h][rr], vn_b)
        o = jnp.concatenate(qs_, axis=0) + _dot(qks[h], jnp.concatenate(vn, axis=0))
        z = x_ref[:, pl.ds(QKV + h * DH, DH)]
        o_ref[:, h * DH:(h + 1) * DH] = _gated_out(o, z, gout_ref[0]).astype(o_ref.dtype)


def _gdn_sample(layer, proj, conv_in, s0, w_gconv, gprm, g_gout):
    row0 = T_P // TILE
    return pl.pallas_call(
        _gdn_s_kernel,
        grid=(T_S // TILE,),
        in_specs=[
            pl.BlockSpec((TILE, COL_GLU), lambda i: (row0 + i, 0)),
            pl.BlockSpec((TILE, LANES), lambda i: (row0 + i, COL_BA // LANES)),
            pl.BlockSpec((1, BT_S, GCONV - 1, QKV), lambda i: (layer, i, 0, 0)),
            pl.BlockSpec((1, BT_S, NH, DH, DH), lambda i: (layer, i, 0, 0, 0)),
            pl.BlockSpec((1, GCONV, QKV), lambda i: (layer, 0, 0)),
            pl.BlockSpec((1, SUBLANES, LANES), lambda i: (layer, 0, 0)),
            pl.BlockSpec((1, 1, DH), lambda i: (layer, 0, 0)),
        ],
        out_specs=[
            pl.BlockSpec((TILE, GW), lambda i: (i, 0)),
            pl.BlockSpec((BT_S, NH, DH, DH), lambda i: (i, 0, 0, 0)),
            pl.BlockSpec((BT_S, GCONV - 1, QKV), lambda i: (i, 0, 0)),
        ],
        out_shape=[
            jax.ShapeDtypeStruct((T_S, GW), bf16),
            jax.ShapeDtypeStruct((NB_S, NH, DH, DH), f32),
            jax.ShapeDtypeStruct((NB_S, GCONV - 1, QKV), f32),
        ],
        scratch_shapes=[pltpu.VMEM((TILE, QKV), f32)],
        compiler_params=_cparams(("parallel",)),
        name=f"gdn_sample_{layer}",
    )(proj, proj, conv_in, s0, w_gconv, gprm, g_gout)


def _group_ln_silu(y, gl_ref, bl_ref):
    outs = []
    for gi in range(CGROUPS):
        w = CW // CGROUPS
        yg = y[:, gi * w:(gi + 1) * w]
        mu = jnp.mean(yg, axis=-1, keepdims=True)
        d = yg - mu
        var = jnp.mean(d * d, axis=-1, keepdims=True)
        outs.append(d * lax.rsqrt(var + LN_EPS))
    yn = jnp.concatenate(outs, axis=1) * gl_ref[0] + bl_ref[0]
    return _silu(yn)


def _conf_p_kernel(x_ref, halo_ref, w_ref, b_ref, gl_ref, bl_ref, o_ref, hst_ref, xp_scr, sh_scr):
    j = pl.program_id(1)
    nj = pl.num_programs(1)
    h = x_ref[:, :CW] * jax.nn.sigmoid(x_ref[:, CW:])
    hh = halo_ref[:, :CW] * jax.nn.sigmoid(halo_ref[:, CW:])
    xp_scr[pl.ds(0, HIST_C), :] = jnp.where(j > 0, hh, 0.0)
    xp_scr[pl.ds(HIST_C, LB_C), :] = h
    n_sh = HIST_C + LB_C - SUBLANES
    for r in range(1, SUBLANES):
        sh_scr[r - 1] = xp_scr[pl.ds(r, n_sh), :]
    off = HIST_C - (CK - 1)
    y = None
    for t in range(CK):
        a, r = divmod(off + t, SUBLANES)
        rows = pl.ds(a * SUBLANES, LB_C)
        win = xp_scr[rows, :] if r == 0 else sh_scr[r - 1, rows, :]
        term = win * w_ref[0, pl.ds(t, 1), :]
        y = term if y is None else y + term
    y = y + b_ref[0]
    o_ref[...] = _group_ln_silu(y, gl_ref, bl_ref).astype(o_ref.dtype)

    @pl.when(j == nj - 1)
    def _():
        hst_ref[0] = h[LB_C - HIST_C:, :]


def _conf_prompt(layer, proj, w_dw, b_dw, g_ln, b_ln):
    nj = L_P // LB_C
    cb = COL_GLU // (2 * CW)
    return pl.pallas_call(
        _conf_p_kernel,
        grid=(NB_P, nj),
        in_specs=[
            pl.BlockSpec((LB_C, 2 * CW), lambda b, j: (b * nj + j, cb)),
            pl.BlockSpec((HIST_C, 2 * CW),
                         lambda b, j: (jnp.maximum((b * nj + j) * (LB_C // HIST_C) - 1, 0), cb)),
            pl.BlockSpec((1, CK, CW), lambda b, j: (layer, 0, 0)),
            pl.BlockSpec((1, 1, CW), lambda b, j: (layer, 0, 0)),
            pl.BlockSpec((1, 1, CW), lambda b, j: (layer, 0, 0)),
            pl.BlockSpec((1, 1, CW), lambda b, j: (layer, 0, 0)),
        ],
        out_specs=[
            pl.BlockSpec((LB_C, CW), lambda b, j: (b * nj + j, 0)),
            pl.BlockSpec((1, HIST_C, CW), lambda b, j: (b, 0, 0)),
        ],
        out_shape=[
            jax.ShapeDtypeStruct((T_P, CW), bf16),
            jax.ShapeDtypeStruct((NB_P, HIST_C, CW), f32),
        ],
        scratch_shapes=[pltpu.VMEM((HIST_C + LB_C, CW), f32),
                        pltpu.VMEM((SUBLANES - 1, HIST_C + LB_C - SUBLANES, CW), f32)],
        compiler_params=_cparams(("parallel", "arbitrary")),
        name=f"conf_prompt_{layer}",
    )(proj, proj, w_dw, b_dw, g_ln, b_ln)


BT_C = 16


def _conf_s_kernel(x_ref, hin_ref, wh_ref, wn_ref, b_ref, gl_ref, bl_ref, o_ref, hst_ref, y_scr):
    h = x_ref[:, :CW] * jax.nn.sigmoid(x_ref[:, CW:])
    trow = lax.broadcasted_iota(jnp.int32, (TPAD, 1), 0)
    nh = CK - 1
    for b in range(BT_C):
        hist = hin_ref[0, b]
        h8 = h[b * TPAD:(b + 1) * TPAD, :]
        y8 = jnp.zeros((TPAD, CW), f32)
        for t in range(L_S):
            yt = (jnp.sum(hist * wh_ref[0, t], axis=0, keepdims=True)
                  + jnp.sum(h8 * wn_ref[0, t], axis=0, keepdims=True))
            y8 = jnp.where(trow == t, yt, y8)
        y_scr[pl.ds(b * TPAD, TPAD), :] = y8
        hst_ref[b, pl.ds(0, nh - L_S), :] = hin_ref[0, b, pl.ds(L_S, nh - L_S), :]
        hst_ref[b, pl.ds(nh - L_S, L_S), :] = h8[:L_S, :]
    y = y_scr[...] + b_ref[0]
    o_ref[...] = _group_ln_silu(y, gl_ref, bl_ref).astype(o_ref.dtype)


def _conf_sample(layer, proj, hist_in, w_hist, w_new, b_dw, g_ln, b_ln):
    row0 = T_P // (BT_C * TPAD)
    cb = COL_GLU // (2 * CW)
    return pl.pallas_call(
        _conf_s_kernel,
        grid=(NB_S // BT_C,),
        in_specs=[
            pl.BlockSpec((BT_C * TPAD, 2 * CW), lambda i: (row0 + i, cb)),
            pl.BlockSpec((1, BT_C, CK - 1, CW), lambda i: (layer, i, 0, 0)),
            pl.BlockSpec((1, L_S, CK - 1, CW), lambda i: (layer, 0, 0, 0)),
            pl.BlockSpec((1, L_S, TPAD, CW), lambda i: (layer, 0, 0, 0)),
            pl.BlockSpec((1, 1, CW), lambda i: (layer, 0, 0)),
            pl.BlockSpec((1, 1, CW), lambda i: (layer, 0, 0)),
            pl.BlockSpec((1, 1, CW), lambda i: (layer, 0, 0)),
        ],
        out_specs=[
            pl.BlockSpec((BT_C * TPAD, CW), lambda i: (i, 0)),
            pl.BlockSpec((BT_C, CK - 1, CW), lambda i: (i, 0, 0)),
        ],
        out_shape=[
            jax.ShapeDtypeStruct((T_S, CW), bf16),
            jax.ShapeDtypeStruct((NB_S, CK - 1, CW), f32),
        ],
        scratch_shapes=[pltpu.VMEM((BT_C * TPAD, CW), f32)],
        compiler_params=_cparams(("parallel",)),
        name=f"conf_sample_{layer}",
    )(proj, hist_in, w_hist, w_new, b_dw, g_ln, b_ln)


def _out_kernel(moe, x_ref, oap, obp, oas, obs, w_ref, g_ref,
                g1p, scp, shp, g1s, scs, shs, *rest):
    if moe:
        wr_ref, br_ref, xo_ref, h2_ref, rt_ref = rest
        hf_scr = h2_ref
    else:
        xo_ref, h2_ref, hf_scr = rest
    i = pl.program_id(0)

    def attn(oa, ob):
        return (jnp.dot(oa[...], w_ref[0, :GW, :], preferred_element_type=f32)
                + jnp.dot(ob[...], w_ref[0, GW:, :], preferred_element_type=f32))

    @pl.when(i < NT_P)
    def _():
        xn = x_ref[...] + g1p[0] * attn(oap, obp)
        xo_ref[...] = xn
        hf_scr[...] = _rms(xn, g_ref[0]) * (1.0 + scp[0]) + shp[0]

    @pl.when(i >= NT_P)
    def _():
        hf_scr[...] = attn(oas, obs)

        def body(rows, b):
            m = pl.ds(b, 1)
            xn = x_ref[rows, :] + g1s[m, :] * hf_scr[rows, :]
            xo_ref[rows, :] = xn
            hf_scr[rows, :] = _rms(xn, g_ref[0]) * (1.0 + scs[m, :]) + shs[m, :]
        _sample_rows(body)

    hf = hf_scr[...]
    if not moe:
        h2_ref[...] = hf.astype(h2_ref.dtype)
    if moe:
        logits = _dot_hi(hf, wr_ref[0]) + br_ref[0]
        lane = lax.broadcasted_iota(jnp.int32, logits.shape, 1)
        ex = jnp.exp(logits - jnp.max(logits, axis=-1, keepdims=True))
        probs = ex / jnp.sum(ex, axis=-1, keepdims=True)
        m1 = jnp.max(probs, axis=-1, keepdims=True)
        i1 = jnp.min(jnp.where(probs == m1, lane, LANES), axis=-1, keepdims=True)
        rest_p = jnp.where(lane == i1, -1.0, probs)
        m2 = jnp.max(rest_p, axis=-1, keepdims=True)
        i2 = jnp.min(jnp.where(rest_p == m2, lane, LANES), axis=-1, keepdims=True)
        den = m1 + m2
        rt_ref[...] = jnp.where(lane == 0, i1.astype(f32),
                                jnp.where(lane == 1, i2.astype(f32),
                                          jnp.where(lane == 2, m1 / den,
                                                    jnp.where(lane == 3, m2 / den, 0.0))))


def _out_proj(layer, moe, x, oa_p, ob_p, oa_s, ob_s, w_out, g2, modp, mods, wr=None, br=None):
    idx = layer // 2
    in_specs = [
        pl.BlockSpec((TM, D), lambda i: (i, 0)),
        pl.BlockSpec((TM, GW), lambda i: (jnp.minimum(i, NT_P - 1), 0)),
        pl.BlockSpec((TM, CW), lambda i: (jnp.minimum(i, NT_P - 1), 0)),
        pl.BlockSpec((TM, GW), lambda i: (0, 0)),
        pl.BlockSpec((TM, CW), lambda i: (0, 0)),
        pl.BlockSpec((1, D, D), lambda i: (layer, 0, 0)),
        pl.BlockSpec((1, 1, D), lambda i: (layer, 0, 0)),
        _mod_specs(G1, 1)[0], _mod_specs(SC2, 1)[0], _mod_specs(SH2, 1)[0],
        _mod_specs(G1, 1)[1], _mod_specs(SC2, 1)[1], _mod_specs(SH2, 1)[1],
    ]
    args = [x, oa_p, ob_p, oa_s, ob_s, w_out, g2, modp, modp, modp, mods, mods, mods]
    out_specs = [pl.BlockSpec((TM, D), lambda i: (i, 0)), pl.BlockSpec((TM, D), lambda i: (i, 0))]
    out_shape = [jax.ShapeDtypeStruct((T_ALL, D), f32),
                 jax.ShapeDtypeStruct((T_ALL, D), f32 if moe else bf16)]
    if moe:
        in_specs += [pl.BlockSpec((1, D, LANES), lambda i: (idx, 0, 0)),
                     pl.BlockSpec((1, 1, LANES), lambda i: (idx, 0, 0))]
        args += [wr, br]
        out_specs.append(pl.BlockSpec((TM, LANES), lambda i: (i, 0)))
        out_shape.append(jax.ShapeDtypeStruct((T_ALL, LANES), f32))
    return pl.pallas_call(
        functools.partial(_out_kernel, moe),
        grid=(NT,),
        in_specs=in_specs,
        out_specs=out_specs,
        out_shape=out_shape,
        scratch_shapes=[] if moe else [pltpu.VMEM((TM, D), f32)],
        compiler_params=_cparams(("parallel",)),
        name=f"out_proj_{layer}",
    )(*args)


def _residual(i, x_ref, f_ref, g2p, g2s, o_ref, gf_ref, os_ref=None):
    os_ref = o_ref if os_ref is None else os_ref

    def fin(v):
        return v if gf_ref is None else _rms(v, gf_ref[...])

    @pl.when(i < NT_P)
    def _():
        o_ref[...] = fin(x_ref[...] + g2p[0] * f_ref[...])

    @pl.when(i >= NT_P)
    def _():
        def body(rows, b):
            os_ref[rows, :] = fin(x_ref[rows, :] + g2s[pl.ds(b, 1), :] * f_ref[rows, :])
        _sample_rows(body)


def _ffn_kernel(h_ref, wg_ref, wu_ref, wd_ref, x_ref, g2p, g2s, o_ref, acc_scr):
    i = pl.program_id(0)
    h = h_ref[...]
    for c in range(FF_D // TF_D):
        cols = slice(c * TF_D, (c + 1) * TF_D)
        gate = jnp.dot(h, wg_ref[0, :, cols], preferred_element_type=f32)
        up = jnp.dot(h, wu_ref[0, :, cols], preferred_element_type=f32)
        part = jnp.dot((_silu(gate) * up).astype(bf16), wd_ref[0, cols, :], preferred_element_type=f32)
        if c == 0:
            acc_scr[...] = part
        else:
            acc_scr[...] += part
    _residual(i, x_ref, acc_scr, g2p, g2s, o_ref, None)


def _ffn_dense(layer, h2, x, wg, wu, wd, modp, mods):
    idx = layer // 2
    mp, ms = _mod_specs(G2, 1)
    return pl.pallas_call(
        _ffn_kernel,
        grid=(NT,),
        in_specs=[
            pl.BlockSpec((TM, D), lambda i: (i, 0)),
            _resident((1, D, FF_D), lambda i: (idx, 0, 0)),
            _resident((1, D, FF_D), lambda i: (idx, 0, 0)),
            _resident((1, FF_D, D), lambda i: (idx, 0, 0)),
            pl.BlockSpec((TM, D), lambda i: (i, 0)),
            mp, ms,
        ],
        out_specs=pl.BlockSpec((TM, D), lambda i: (i, 0)),
        out_shape=jax.ShapeDtypeStruct((T_ALL, D), f32),
        scratch_shapes=[pltpu.VMEM((TM, D), f32)],
        compiler_params=_cparams(("parallel",)),
        name=f"ffn_dense_{layer}",
    )(h2, wg, wu, wd, x, modp, mods)


def _gather_rows(idx_ref, base, stride, src_hbm, dst, sem):
    def issue(r, carry):
        row = idx_ref[base + stride * r]
        if len(src_hbm.shape) == 3:
            pltpu.make_async_copy(src_hbm.at[row], dst.at[r], sem).start()
        else:
            pltpu.make_async_copy(src_hbm.at[pl.ds(row, 1), :], dst.at[pl.ds(r, 1), :], sem).start()
        return carry
    lax.fori_loop(0, MOE_BLK, issue, 0, unroll=8)


def _wait_rows(src_hbm, dst, sem):
    pltpu.make_async_copy(src_hbm.at[pl.ds(0, MOE_BLK)], dst, sem).wait()


def _issue_rows(idx_ref, base, src_hbm, dst, sem, r0, n):
    for u in range(n):
        row = idx_ref[base + r0 + u]
        pltpu.make_async_copy(src_hbm.at[pl.ds(row, 1), :], dst.at[pl.ds(r0 + u, 1), :], sem).start()


def _expert_kernel(be_ref, tok_ref, nv_ref, h_hbm, wg_ref, wu_ref, wd_ref,
                   y_ref, xg_scr, xb_scr, acc_scr, sem):
    m = pl.program_id(0)
    f = pl.program_id(1)
    nv = nv_ref[0]
    active = m < nv
    slot = m % 2
    nxt = (m + 1) * MOE_BLK

    @pl.when(jnp.logical_and(f == 0, m == 0))
    def _():
        _gather_rows(tok_ref, 0, 1, h_hbm, xg_scr.at[0], sem.at[0])

    @pl.when(jnp.logical_and(f == 0, m <= nv))
    def _():
        _wait_rows(h_hbm, xg_scr.at[slot], sem.at[slot])

    @pl.when(jnp.logical_and(f == 0, active))
    def _():
        xb_scr[...] = xg_scr[slot].astype(bf16)
        _issue_rows(tok_ref, nxt, h_hbm, xg_scr.at[1 - slot], sem.at[1 - slot],
                    GATHER_CHUNK * (FF_E // TF_E), GATHER_TAIL)

    @pl.when(jnp.logical_and(f == 0, jnp.logical_not(active)))
    def _():
        y_ref[...] = jnp.zeros_like(y_ref)

    @pl.when(jnp.logical_and(f == 0, active))
    def _():
        acc_scr[...] = jnp.zeros_like(acc_scr)

    @pl.when(active)
    def _():
        _issue_rows(tok_ref, nxt, h_hbm, xg_scr.at[1 - slot], sem.at[1 - slot], f * GATHER_CHUNK, GATHER_CHUNK)
        xb = xb_scr[...]
        a = _silu(_dot(xb, wg_ref[0, 0])) * _dot(xb, wu_ref[0, 0])
        acc_scr[...] += _dot(a, wd_ref[0, 0])

    @pl.when(jnp.logical_and(f == pl.num_programs(1) - 1, active))
    def _():
        y_ref[...] = acc_scr[...].reshape(MOE_BLK, D // LANES, LANES)


def _experts(idx, h2, block_e, slot_tok, n_valid, wg, wu, wd):
    nf = FF_E // TF_E

    def wmap_up(m, f, be, tok, nv):
        return (idx, be[m], 0, jnp.where(m < nv[0], f, nf - 1))

    def wmap_down(m, f, be, tok, nv):
        return (idx, be[m], jnp.where(m < nv[0], f, nf - 1), 0)

    grid_spec = pltpu.PrefetchScalarGridSpec(
        num_scalar_prefetch=3,
        grid=(N_BLK + 1, nf),
        in_specs=[
            pl.BlockSpec(memory_space=pl.ANY),
            pl.BlockSpec((1, 1, D, TF_E), wmap_up),
            pl.BlockSpec((1, 1, D, TF_E), wmap_up),
            pl.BlockSpec((1, 1, TF_E, D), wmap_down),
        ],
        out_specs=pl.BlockSpec((MOE_BLK, D // LANES, LANES), lambda m, f, be, tok, nv: (m, 0, 0)),
        scratch_shapes=[pltpu.VMEM((2, MOE_BLK, D), f32), pltpu.VMEM((MOE_BLK, D), bf16),
                        pltpu.VMEM((MOE_BLK, D), f32), pltpu.SemaphoreType.DMA((2,))],
    )
    return pl.pallas_call(
        _expert_kernel,
        grid_spec=grid_spec,
        out_shape=jax.ShapeDtypeStruct(((N_BLK + 1) * MOE_BLK, D // LANES, LANES), f32),
        compiler_params=_cparams(("arbitrary", "arbitrary")),
        name=f"experts_{idx}",
    )(block_e, slot_tok, n_valid, h2, wg, wu, wd)


def _combine_kernel(final, dest_ref, y_hbm, x_ref, rt_ref, g2p, g2s, *rest):
    if final:
        gf_ref, o_ref, os_ref, g_scr, f_scr, sem = rest
    else:
        o_ref, g_scr, f_scr, sem = rest
        gf_ref = os_ref = None
    i = pl.program_id(0)
    slot = i % 2

    def gather(tile, sl):
        for k in range(2):
            _gather_rows(dest_ref, 2 * tile * TM + k, 2, y_hbm, g_scr.at[sl, k], sem.at[sl, k])

    @pl.when(i == 0)
    def _():
        gather(0, 0)

    @pl.when(i + 1 < pl.num_programs(0))
    def _():
        gather(i + 1, 1 - slot)

    for k in range(2):
        _wait_rows(y_hbm, g_scr.at[slot, k], sem.at[slot, k])
    f_scr[...] = (rt_ref[:, 2:3] * g_scr[slot, 0].reshape(TM, D) + rt_ref[:, 3:4] * g_scr[slot, 1].reshape(TM, D))
    _residual(i, x_ref, f_scr, g2p, g2s, o_ref, gf_ref, os_ref)


def _combine(final, dest, yb, x, rt, modp, mods, g_final):
    mp, ms = _mod_specs(G2, 1)
    wrap = lambda spec: pl.BlockSpec(spec.block_shape, lambda i, d, _f=spec.index_map: _f(i))
    in_specs = [pl.BlockSpec(memory_space=pl.ANY),
                pl.BlockSpec((TM, D), lambda i, d: (i, 0)),
                pl.BlockSpec((TM, LANES), lambda i, d: (i, 0)), wrap(mp), wrap(ms)]
    args = [yb, x, rt, modp, mods]
    if final:
        in_specs.append(pl.BlockSpec((1, D), lambda i, d: (0, 0)))
        args.append(g_final)
        out_specs = [pl.BlockSpec((TM, D), lambda i, d: (jnp.minimum(i, NT_P - 1), 0)),
                     pl.BlockSpec((T_S, D), lambda i, d: (0, 0))]
        out_shape = [jax.ShapeDtypeStruct((T_P, D), f32), jax.ShapeDtypeStruct((T_S, D), f32)]
    else:
        out_specs = pl.BlockSpec((TM, D), lambda i, d: (i, 0))
        out_shape = jax.ShapeDtypeStruct((T_ALL, D), f32)
    grid_spec = pltpu.PrefetchScalarGridSpec(
        num_scalar_prefetch=1,
        grid=(NT,),
        in_specs=in_specs,
        out_specs=out_specs,
        scratch_shapes=[pltpu.VMEM((2, 2, TM, D // LANES, LANES), f32), pltpu.VMEM((TM, D), f32),
                        pltpu.SemaphoreType.DMA((2, 2))],
    )
    return pl.pallas_call(
        functools.partial(_combine_kernel, final),
        grid_spec=grid_spec,
        out_shape=out_shape,
        compiler_params=_cparams(("arbitrary",)),
        name="combine_final" if final else "combine",
    )(dest, *args)


def _route(rt):
    e = rt[:, :2].astype(jnp.int32).reshape(N_ASG)
    onehot = (e[:, None] == jnp.arange(NE, dtype=jnp.int32)[None, :]).astype(jnp.int32)
    csum = jnp.cumsum(onehot, axis=0)
    counts = csum[-1]
    rank = jnp.sum(onehot * (csum - 1), axis=1)
    padded = (counts + MOE_BLK - 1) // MOE_BLK * MOE_BLK
    pad_end = jnp.cumsum(padded)
    pad_start = pad_end - padded
    dest = (jnp.sum(onehot * pad_start[None, :], axis=1) + rank).astype(jnp.int32)
    n_slots = (N_BLK + 1) * MOE_BLK
    tok = jnp.arange(N_ASG, dtype=jnp.int32) // 2
    slot_tok = jnp.zeros((n_slots,), jnp.int32).at[dest].set(tok, unique_indices=True)
    n_valid = (pad_end[-1] // MOE_BLK).astype(jnp.int32)
    blk = jnp.minimum(jnp.arange(N_BLK + 1, dtype=jnp.int32), n_valid - 1)
    first_slot = (blk * MOE_BLK)[:, None]
    block_e = jnp.minimum(jnp.sum((pad_end[None, :] <= first_slot).astype(jnp.int32), axis=1), NE - 1)
    return dest, slot_tok, block_e, n_valid.reshape(1)


def kernel(x_prompt, x_sample, c_prompt, c_sample, state_gdn, state_gdn_conv, state_conf_conv, w_ada, b_ada, g_norm1, g_norm2, w_in, w_gdn_conv, a_log, dt_bias, g_gdn_out, w_conf_dw, b_conf_dw, g_conf_ln, b_conf_ln, w_out, w_ff_gate, w_ff_up, w_ff_down, w_router, b_router, w_exp_gate, w_exp_up, w_exp_down, g_final):
    xs_pad = jnp.pad(x_sample, ((0, 0), (0, TPAD - L_S), (0, 0)))
    x = jnp.concatenate([x_prompt.reshape(T_P, D), xs_pad.reshape(T_S, D)], axis=0)
    c_all = jnp.concatenate([c_prompt, c_sample], axis=0)
    o1, o2, o4 = QKV + GW, QKV + GW + 2 * NH, QKV + GW + 2 * NH + 2 * CW
    w_cat = jnp.concatenate([w_in[:, :, :o1], w_in[:, :, o2:o4], w_in[:, :, o1:o2],
                             jnp.zeros((DEPTH, D, PROJ_W - o4), f32)], axis=-1).astype(bf16)
    lane_pad = ((0, 0), (NH, LANES - 2 * NH))
    gprm = jnp.stack([jnp.pad(a_log, lane_pad), jnp.pad(dt_bias, lane_pad)], axis=1)
    gprm = jnp.pad(gprm, ((0, 0), (0, SUBLANES - 2), (0, 0)))
    wr_pad = jnp.pad(w_router, ((0, 0), (0, 0), (0, LANES - NE)))
    br_pad = jnp.pad(b_router, ((0, 0), (0, LANES - NE)), constant_values=-1e30).reshape(-1, 1, LANES)
    w_shift = jnp.stack([jnp.pad(w_conf_dw, ((0, 0), (t, TPAD - 1 - t), (0, 0))) for t in range(L_S)], axis=1)
    w_hist, w_new = w_shift[:, :, :CK - 1], w_shift[:, :, CK - 1:]
    w_out = w_out.astype(bf16)
    w_ff_gate, w_ff_up, w_ff_down = (w.astype(bf16) for w in (w_ff_gate, w_ff_up, w_ff_down))
    g1 = g_norm1.reshape(DEPTH, 1, D)
    g2 = g_norm2.reshape(DEPTH, 1, D)
    gout = g_gdn_out.reshape(DEPTH, 1, DH)
    b_dw = b_conf_dw.reshape(DEPTH, 1, CW)
    g_ln = g_conf_ln.reshape(DEPTH, 1, CW)
    b_ln = b_conf_ln.reshape(DEPTH, 1, CW)

    mod = _ada(c_all, w_ada, b_ada)

    sp_l, cp_l, fp_l, ss_l, cs_l, fs_l = [], [], [], [], [], []
    for layer in range(DEPTH):
        modp = mod[layer, :NB_P].reshape(NB_P, 1, N_MOD * D)
        mods = mod[layer, NB_P:]
        proj = _in_proj(layer, x, g1, modp, mods, w_cat)
        oa_p, s_p, c_p = _gdn_prompt(layer, proj, w_gdn_conv, gprm, gout)
        oa_s, s_s, c_s = _gdn_sample(layer, proj, state_gdn_conv, state_gdn, w_gdn_conv, gprm, gout)
        ss_l.append(s_s)
        ob_p, f_p = _conf_prompt(layer, proj, w_conf_dw, b_dw, g_ln, b_ln)
        ob_s, f_s = _conf_sample(layer, proj, state_conf_conv, w_hist, w_new, b_dw, g_ln, b_ln)
        sp_l.append(s_p)
        cp_l.append(c_p[:, TPAD - (GCONV - 1):, :])
        fp_l.append(f_p[:, HIST_C - (CK - 1):, :])
        cs_l.append(c_s)
        fs_l.append(f_s)
        if layer % 2 == 0:
            x, h2 = _out_proj(layer, False, x, oa_p, ob_p, oa_s, ob_s, w_out, g2, modp, mods)
            x = _ffn_dense(layer, h2, x, w_ff_gate, w_ff_up, w_ff_down, modp, mods)
        else:
            x, h2, rt = _out_proj(layer, True, x, oa_p, ob_p, oa_s, ob_s, w_out, g2, modp, mods, wr_pad, br_pad)
            dest, slot_tok, block_e, n_valid = _route(rt)
            yb = _experts(layer // 2, h2, block_e, slot_tok, n_valid, w_exp_gate, w_exp_up, w_exp_down)
            final = layer == DEPTH - 1
            x = _combine(final, dest, yb, x, rt, modp, mods, g_final.reshape(1, D))

    y_p, y_s = x
    y_prompt = y_p.reshape(NB_P, L_P, D)
    y_sample = y_s.reshape(NB_S, TPAD, D)[:, :L_S, :]
    return (y_prompt, y_sample, jnp.stack(sp_l), jnp.stack(cp_l), jnp.stack(fp_l),
            jnp.stack(ss_l), jnp.stack(cs_l), jnp.stack(fs_l))
```

```python
import functools
import math

import jax
import jax.numpy as jnp
from jax import lax
from jax.experimental import pallas as pl
from jax.experimental.pallas import tpu as pltpu

f32 = jnp.float32
bf16 = jnp.bfloat16
HIGHEST = lax.Precision.HIGHEST

D = 1024
NB_P, L_P = 8, 2048
NB_S, L_S = 128, 4
DEPTH = 4
NH, DH = 4, 128
GW = NH * DH
QKV = 3 * GW
GCONV = 4
GCHUNK = 64
CW = D - GW
CGROUPS = 4
CK = 31
FF_D = 2816
NE = 8
FF_E = 3584
N_MOD = 6
RMS_EPS = 1e-6
LN_EPS = 1e-5

SUBLANES = 8
LANES = 128
VMEM_LIMIT = 56 * 1024 * 1024

TPAD = SUBLANES
T_P = NB_P * L_P
T_S = NB_S * TPAD
T_ALL = T_P + T_S
TM = 1024
NT = T_ALL // TM
NT_P = T_P // TM
TILES_PER_SEQ = L_P // TM
PROJ_W = 3200
COL_GLU = 2048
COL_BA = 3072
TN_IN = 640
HIST_C = 32
SH1, SC1, G1, SH2, SC2, G2 = range(N_MOD)

LB_G = 512
LB_C = 512
TILE = 128
TILES_PER_ITER = 4
TF_D = 1408
TF_E = 512
MOE_BLK = 1024
N_ASG = 2 * T_ALL
N_BLK = N_ASG // MOE_BLK + NE
GATHER_CHUNK = MOE_BLK // (FF_E // TF_E)
GATHER_TAIL = MOE_BLK - GATHER_CHUNK * (FF_E // TF_E)


def _cparams(sem):
    return pltpu.CompilerParams(dimension_semantics=sem, vmem_limit_bytes=VMEM_LIMIT)


def _dot(a, b):
    return jnp.dot(a.astype(bf16), b.astype(bf16), preferred_element_type=f32)


def _dot_hi(a, b):
    return jnp.dot(a, b, precision=HIGHEST, preferred_element_type=f32)


def _dot_nt(a, b):
    return lax.dot_general(a.astype(bf16), b.astype(bf16), (((1,), (1,)), ((), ())),
                           preferred_element_type=f32)


def _dot_nt_hi(a, b):
    return lax.dot_general(a, b, (((1,), (1,)), ((), ())), precision=HIGHEST,
                           preferred_element_type=f32)


def _dot_tn(a, b):
    return lax.dot_general(a, b, (((0,), (0,)), ((), ())), preferred_element_type=f32)


def _silu(x):
    return x * jax.nn.sigmoid(x)


def _rms(x, g):
    return x * lax.rsqrt(jnp.mean(x * x, axis=-1, keepdims=True) + RMS_EPS) * g


def _ada_kernel(c_ref, w_ref, b_ref, o_ref):
    a = _silu(c_ref[...])
    o_ref[0] = _dot(a, w_ref[0]) + b_ref[0]


def _ada(c_all, w_ada, b_ada):
    n = c_all.shape[0]
    tn = 1536
    return pl.pallas_call(
        _ada_kernel,
        grid=(DEPTH, N_MOD * D // tn),
        in_specs=[
            pl.BlockSpec((n, D), lambda l, j: (0, 0)),
            pl.BlockSpec((1, D, tn), lambda l, j: (l, 0, j)),
            pl.BlockSpec((1, 1, tn), lambda l, j: (l, 0, j)),
        ],
        out_specs=pl.BlockSpec((1, n, tn), lambda l, j: (l, 0, j)),
        out_shape=jax.ShapeDtypeStruct((DEPTH, n, N_MOD * D), f32),
        compiler_params=_cparams(("parallel", "parallel")),
        name="ada",
    )(c_all, w_ada, b_ada.reshape(DEPTH, 1, N_MOD * D))


def _mod_specs(k, ngrid):
    if ngrid == 1:
        return [pl.BlockSpec((1, 1, D), lambda i: (jnp.minimum(i // TILES_PER_SEQ, NB_P - 1), 0, k)),
                pl.BlockSpec((NB_S, D), lambda i: (0, k))]
    return [pl.BlockSpec((1, 1, D), lambda i, j: (jnp.minimum(i // TILES_PER_SEQ, NB_P - 1), 0, k)),
            pl.BlockSpec((NB_S, D), lambda i, j: (0, k))]


def _sample_rows(body):
    def step(b, carry):
        body(pl.ds(pl.multiple_of(b * TPAD, TPAD), TPAD), b)
        return carry
    lax.fori_loop(0, NB_S, step, 0)


def _in_kernel(x_ref, g_ref, shp, scp, shs, scs, w_ref, o_ref, h_scr, hf_scr):
    i = pl.program_id(0)

    @pl.when(i < NT_P)
    def _():
        h = _rms(x_ref[...], g_ref[0]) * (1.0 + scp[0]) + shp[0]
        h_scr[...] = h.astype(bf16)

    @pl.when(i >= NT_P)
    def _():
        def body(rows, b):
            hf_scr[rows, :] = (_rms(x_ref[rows, :], g_ref[0]) * (1.0 + scs[pl.ds(b, 1), :])
                               + shs[pl.ds(b, 1), :])
        _sample_rows(body)
        h_scr[...] = hf_scr[...].astype(bf16)

    for c in range(PROJ_W // TN_IN):
        cols = slice(c * TN_IN, (c + 1) * TN_IN)
        o_ref[:, cols] = jnp.dot(h_scr[...], w_ref[0, :, cols], preferred_element_type=f32)


def _resident(block_shape, index_map):
    return pl.BlockSpec(block_shape, index_map, pipeline_mode=pl.Buffered(1))


def _in_proj(layer, x, g1, modp, mods, w_cat):
    return pl.pallas_call(
        _in_kernel,
        grid=(NT,),
        in_specs=[
            pl.BlockSpec((TM, D), lambda i: (i, 0)),
            pl.BlockSpec((1, 1, D), lambda i: (layer, 0, 0)),
            *_mod_specs(SH1, 1)[:1], *_mod_specs(SC1, 1)[:1],
            *_mod_specs(SH1, 1)[1:], *_mod_specs(SC1, 1)[1:],
            _resident((1, D, PROJ_W), lambda i: (layer, 0, 0)),
        ],
        out_specs=pl.BlockSpec((TM, PROJ_W), lambda i: (i, 0)),
        out_shape=jax.ShapeDtypeStruct((T_ALL, PROJ_W), f32),
        scratch_shapes=[pltpu.VMEM((TM, D), bf16), pltpu.VMEM((TM, D), f32)],
        compiler_params=_cparams(("parallel",)),
        name=f"in_proj_{layer}",
    )(x, g1, modp, modp, mods, mods, w_cat)


def _tile_masks(blk):
    ri = lax.broadcasted_iota(jnp.int32, (TILE, TILE), 0)
    ci = lax.broadcasted_iota(jnp.int32, (TILE, TILE), 1)
    same = (ri // blk) == (ci // blk)
    incl = jnp.logical_and(same, ri >= ci)
    strict = jnp.logical_and(same, ri > ci)
    return same, incl, strict


def _gate_cumsums(g_all, blk):
    same, incl, _ = _tile_masks(blk)
    gc = _dot_hi(incl.astype(f32), g_all)
    if TILE == 2 * blk:
        row = lax.broadcasted_iota(jnp.int32, (TILE, 1), 0)
        gt = jnp.where(row < blk, gc[blk - 1:blk, :], gc[TILE - 1:TILE, :])
    else:
        gt = _dot_hi(same.astype(f32), g_all)
    return gc, gt


def _unit_lower_inverse(a_mats, blk, base, nsq):
    ri = lax.broadcasted_iota(jnp.int32, (TILE, TILE), 0)
    ci = lax.broadcasted_iota(jnp.int32, (TILE, TILE), 1)
    ps = [jnp.where((ri // base) == (ci // base), -a, 0.0) for a in a_mats]
    eye = (ri == ci).astype(f32)
    ts = [eye + p for p in ps]
    for _ in range(nsq):
        ps = [_dot(p, p) for p in ps]
        ts = [t + _dot(p, t) for p, t in zip(ps, ts)]
    b = base
    while b < blk:
        off = jnp.logical_and((ri // (2 * b)) == (ci // (2 * b)), (ri // b) != (ci // b))
        ms = [_dot(jnp.where(off, a, 0.0), t) for a, t in zip(a_mats, ts)]
        ts = [t - _dot(t, m) for t, m in zip(ts, ms)]
        b *= 2
    return ts


def _delta_prepare(qs, ks, vs, betas, gcs, grows, gts, blk, nsq):
    _, incl, strict = _tile_masks(blk)
    n = range(len(qs))
    dmats = [gc - gr for gc, gr in zip(gcs, grows)]
    decays = [jnp.where(incl, jnp.exp(jnp.where(incl, d, 0.0)), 0.0) for d in dmats]
    egcs = [jnp.exp(gc) for gc in gcs]
    kbs = [ks[i] * betas[i] for i in n]
    a_mats = [jnp.where(strict, _dot_nt(kbs[i], ks[i]) * decays[i], 0.0) for i in n]
    qks = [_dot_nt(qs[i], ks[i]) * decays[i] for i in n]
    xs = [jnp.concatenate([vs[i] * betas[i], kbs[i] * egcs[i]], axis=1) for i in n]
    tinvs = _unit_lower_inverse(a_mats, blk, SUBLANES, nsq)
    xs = [_dot(t, x) for t, x in zip(tinvs, xs)]
    us = [x[:, :DH] for x in xs]
    ws = [x[:, DH:] for x in xs]
    q_decs = [qs[i] * egcs[i] for i in n]
    k_decs = [ks[i] * jnp.exp(gts[i] - gcs[i]) for i in n]
    return us, ws, q_decs, k_decs, qks


def _gdn_features(xc, ba, prm_ref, valid):
    s = _silu(xc)
    if valid is not None:
        s = jnp.where(valid, s, 0.0)
    qs, ks, vs = [], [], []
    for h in range(NH):
        qh = s[:, h * DH:(h + 1) * DH]
        kh = s[:, GW + h * DH:GW + (h + 1) * DH]
        qs.append(qh * lax.rsqrt(jnp.sum(qh * qh, axis=-1, keepdims=True) + 1e-6) * (DH ** -0.5))
        ks.append(kh * lax.rsqrt(jnp.sum(kh * kh, axis=-1, keepdims=True) + 1e-6))
        vs.append(s[:, 2 * GW + h * DH:2 * GW + (h + 1) * DH])
    beta = jax.nn.sigmoid(ba)
    xs = ba + prm_ref[pl.ds(1, 1), :]
    softplus = jnp.maximum(xs, 0.0) + jnp.log1p(jnp.exp(-jnp.abs(xs)))
    g = -jnp.exp(prm_ref[pl.ds(0, 1), :]) * softplus
    if valid is not None:
        beta = jnp.where(valid, beta, 0.0)
        g = jnp.where(valid, g, 0.0)
    return qs, ks, vs, beta, g


def _gated_out(o, z, gout):
    return _rms(o, gout) * _silu(z)


def _gdn_p_kernel(x_ref, halo_ref, ba_ref, wc_ref, prm_ref, gout_ref,
                  o_ref, s_ref, cst_ref,
                  q_scr, k_scr, v_scr, bg_scr, st_scr):
    j = pl.program_id(1)
    nj = pl.num_programs(1)

    @pl.when(j == 0)
    def _():
        st_scr[...] = jnp.zeros_like(st_scr)

    x = x_ref[:, :QKV]
    halo = jnp.where(j > 0, halo_ref[:, :QKV], 0.0)
    xp = jnp.concatenate([halo, x], axis=0)
    xc = x * wc_ref[0, pl.ds(GCONV - 1, 1), :]
    for s in range(1, GCONV):
        xc = xc + pltpu.roll(xp, s, 0)[SUBLANES:, :] * wc_ref[0, pl.ds(GCONV - 1 - s, 1), :]
    qs, ks, vs, beta, g = _gdn_features(xc, ba_ref[...], prm_ref.at[0], None)
    for h in range(NH):
        q_scr[:, h * DH:(h + 1) * DH] = qs[h]
        k_scr[:, h * DH:(h + 1) * DH] = ks[h]
        v_scr[:, h * DH:(h + 1) * DH] = vs[h]
    bg_scr[0] = beta
    bg_scr[1] = g

    nsq = int(math.log2(SUBLANES)) - 1
    nblk = TILE // GCHUNK

    def tile_body(t, carry):
        hs = range(NH)
        cols = [slice(h * DH, (h + 1) * DH) for h in hs]
        rows, qs, ks, vs, betas, gcs, gts, grows = [], [], [], [], [], [], [], []
        for ti in range(TILES_PER_ITER):
            r = pl.ds(pl.multiple_of((t * TILES_PER_ITER + ti) * TILE, TILE), TILE)
            rows.append(r)
            beta_t = bg_scr[0, r, :]
            gc_t, gt_t = _gate_cumsums(bg_scr[1, r, :], GCHUNK)
            gc_tt = gc_t.T
            for h in hs:
                qs.append(q_scr[r, cols[h]])
                ks.append(k_scr[r, cols[h]])
                vs.append(v_scr[r, cols[h]])
                betas.append(beta_t[:, h:h + 1])
                gcs.append(gc_t[:, NH + h:NH + h + 1])
                gts.append(gt_t[:, NH + h:NH + h + 1])
                grows.append(gc_tt[NH + h:NH + h + 1, :])
        us, ws, q_decs, k_decs, qks = _delta_prepare(qs, ks, vs, betas, gcs, grows, gts, GCHUNK, nsq)
        s_cur = [st_scr[h] for h in hs]
        for ti in range(TILES_PER_ITER):
            ch = [ti * NH + h for h in hs]
            egts = [jnp.exp(gts[i]) for i in ch]
            vn = [[] for _ in hs]
            qs_ = [[] for _ in hs]
            for c in range(nblk):
                rr = slice(c * GCHUNK, (c + 1) * GCHUNK)
                wss = [_dot(jnp.concatenate([ws[ch[h]][rr], q_decs[ch[h]][rr]], axis=0), s_cur[h]) for h in hs]
                for h in hs:
                    vn[h].append(us[ch[h]][rr] - wss[h][:GCHUNK])
                    qs_[h].append(wss[h][GCHUNK:])
                s_cur = [s_cur[h] * egts[h][c * GCHUNK:c * GCHUNK + 1, :] + _dot_tn(k_decs[ch[h]][rr], vn[h][c])
                         for h in hs]
            os_ = [jnp.concatenate(qs_[h], axis=0) + _dot(qks[ch[h]], jnp.concatenate(vn[h], axis=0)) for h in hs]
            for h in hs:
                z = x_ref[rows[ti], pl.ds(QKV + h * DH, DH)]
                o_ref[rows[ti], cols[h]] = _gated_out(os_[h], z, gout_ref[0]).astype(o_ref.dtype)
        for h in hs:
            st_scr[h] = s_cur[h]
        return carry

    lax.fori_loop(0, LB_G // (TILE * TILES_PER_ITER), tile_body, 0)

    @pl.when(j == nj - 1)
    def _():
        s_ref[0] = st_scr[...]
        cst_ref[0] = x_ref[pl.ds(LB_G - SUBLANES, SUBLANES), :QKV]


def _gdn_prompt(layer, proj, w_gconv, gprm, g_gout):
    nj = L_P // LB_G
    return pl.pallas_call(
        _gdn_p_kernel,
        grid=(NB_P, nj),
        in_specs=[
            pl.BlockSpec((LB_G, COL_GLU), lambda b, j: (b * nj + j, 0)),
            pl.BlockSpec((SUBLANES, COL_GLU),
                         lambda b, j: (jnp.maximum((b * nj + j) * (LB_G // SUBLANES) - 1, 0), 0)),
            pl.BlockSpec((LB_G, LANES), lambda b, j: (b * nj + j, COL_BA // LANES)),
            pl.BlockSpec((1, GCONV, QKV), lambda b, j: (layer, 0, 0)),
            pl.BlockSpec((1, SUBLANES, LANES), lambda b, j: (layer, 0, 0)),
            pl.BlockSpec((1, 1, DH), lambda b, j: (layer, 0, 0)),
        ],
        out_specs=[
            pl.BlockSpec((LB_G, GW), lambda b, j: (b * nj + j, 0)),
            pl.BlockSpec((1, NH, DH, DH), lambda b, j: (b, 0, 0, 0)),
            pl.BlockSpec((1, SUBLANES, QKV), lambda b, j: (b, 0, 0)),
        ],
        out_shape=[
            jax.ShapeDtypeStruct((T_P, GW), bf16),
            jax.ShapeDtypeStruct((NB_P, NH, DH, DH), f32),
            jax.ShapeDtypeStruct((NB_P, SUBLANES, QKV), f32),
        ],
        scratch_shapes=[pltpu.VMEM((LB_G, GW), f32), pltpu.VMEM((LB_G, GW), f32),
                        pltpu.VMEM((LB_G, GW), f32), pltpu.VMEM((2, LB_G, LANES), f32),
                        pltpu.VMEM((NH, DH, DH), f32)],
        compiler_params=_cparams(("parallel", "arbitrary")),
        name=f"gdn_prompt_{layer}",
    )(proj, proj, proj, w_gconv, gprm, g_gout)


BT_S = TILE // TPAD


def _gdn_s_kernel(x_ref, ba_ref, cin_ref, s0_ref, wc_ref, prm_ref, gout_ref,
                  o_ref, s_ref, cst_ref, st_scr):
    x = x_ref[:, :QKV]
    st_scr[...] = jnp.zeros_like(st_scr)
    for b in range(BT_S):
        st_scr[pl.ds((b + 1) * TPAD - (GCONV - 1), GCONV - 1), :] = cin_ref[0, b]
        cst_ref[b] = x_ref[pl.ds(b * TPAD + L_S - (GCONV - 1), GCONV - 1), :QKV]
    st = st_scr[...]
    trow = lax.broadcasted_iota(jnp.int32, (TILE, 1), 0) % TPAD
    xc = x * wc_ref[0, pl.ds(GCONV - 1, 1), :]
    for s in range(1, GCONV):
        xs = jnp.where(trow >= s, pltpu.roll(x, s, 0), pltpu.roll(st, TILE - TPAD + s, 0))
        xc = xc + xs * wc_ref[0, pl.ds(GCONV - 1 - s, 1), :]
    valid = trow < L_S
    qs, ks, vs, beta, g = _gdn_features(xc, ba_ref[...], prm_ref.at[0], valid)

    nsq = int(math.ceil(math.log2(L_S))) - 1
    gc_t, gt_t = _gate_cumsums(g, TPAD)
    hs = range(NH)
    gcs = [gc_t[:, NH + h:NH + h + 1] for h in hs]
    gts = [gt_t[:, NH + h:NH + h + 1] for h in hs]
    gc_tt = gc_t.T
    grows = [gc_tt[NH + h:NH + h + 1, :] for h in hs]
    us, ws, q_decs, k_decs, qks = _delta_prepare(qs, ks, vs, [beta[:, h:h + 1] for h in hs], gcs, grows, gts,
                                                 TPAD, nsq)
    for h in hs:
        egt = jnp.exp(gts[h])
        vn, qs_ = [], []
        for b in range(BT_S):
            rr = slice(b * TPAD, (b + 1) * TPAD)
            s0 = s0_ref[0, b, h]
            wsb = _dot(jnp.concatenate([ws[h][rr], q_decs[h][rr]], axis=0), s0)
            vn_b = us[h][rr] - wsb[:TPAD]
            vn.append(vn_b)
            qs_.append(wsb[TPAD:])
            s_ref[b, h] = s0 * egt[b * TPAD:b * TPAD + 1, :] + _dot_tn(k_decs[h][rr], vn_b)
        o = jnp.concatenate(qs_, axis=0) + _dot(qks[h], jnp.concatenate(vn, axis=0))
        z = x_ref[:, pl.ds(QKV + h * DH, DH)]
        o_ref[:, h * DH:(h + 1) * DH] = _gated_out(o, z, gout_ref[0]).astype(o_ref.dtype)


def _gdn_sample(layer, proj, conv_in, s0, w_gconv, gprm, g_gout):
    row0 = T_P // TILE
    return pl.pallas_call(
        _gdn_s_kernel,
        grid=(T_S // TILE,),
        in_specs=[
            pl.BlockSpec((TILE, COL_GLU), lambda i: (row0 + i, 0)),
            pl.BlockSpec((TILE, LANES), lambda i: (row0 + i, COL_BA // LANES)),
            pl.BlockSpec((1, BT_S, GCONV - 1, QKV), lambda i: (layer, i, 0, 0)),
            pl.BlockSpec((1, BT_S, NH, DH, DH), lambda i: (layer, i, 0, 0, 0)),
            pl.BlockSpec((1, GCONV, QKV), lambda i: (layer, 0, 0)),
            pl.BlockSpec((1, SUBLANES, LANES), lambda i: (layer, 0, 0)),
            pl.BlockSpec((1, 1, DH), lambda i: (layer, 0, 0)),
        ],
        out_specs=[
            pl.BlockSpec((TILE, GW), lambda i: (i, 0)),
            pl.BlockSpec((BT_S, NH, DH, DH), lambda i: (i, 0, 0, 0)),
            pl.BlockSpec((BT_S, GCONV - 1, QKV), lambda i: (i, 0, 0)),
        ],
        out_shape=[
            jax.ShapeDtypeStruct((T_S, GW), bf16),
            jax.ShapeDtypeStruct((NB_S, NH, DH, DH), f32),
            jax.ShapeDtypeStruct((NB_S, GCONV - 1, QKV), f32),
        ],
        scratch_shapes=[pltpu.VMEM((TILE, QKV), f32)],
        compiler_params=_cparams(("parallel",)),
        name=f"gdn_sample_{layer}",
    )(proj, proj, conv_in, s0, w_gconv, gprm, g_gout)


def _group_ln_silu(y, gl_ref, bl_ref):
    outs = []
    for gi in range(CGROUPS):
        w = CW // CGROUPS
        yg = y[:, gi * w:(gi + 1) * w]
        mu = jnp.mean(yg, axis=-1, keepdims=True)
        d = yg - mu
        var = jnp.mean(d * d, axis=-1, keepdims=True)
        outs.append(d * lax.rsqrt(var + LN_EPS))
    yn = jnp.concatenate(outs, axis=1) * gl_ref[0] + bl_ref[0]
    return _silu(yn)


def _conf_p_kernel(x_ref, halo_ref, w_ref, b_ref, gl_ref, bl_ref, o_ref, hst_ref, xp_scr, sh_scr):
    j = pl.program_id(1)
    nj = pl.num_programs(1)
    h = x_ref[:, :CW] * jax.nn.sigmoid(x_ref[:, CW:])
    hh = halo_ref[:, :CW] * jax.nn.sigmoid(halo_ref[:, CW:])
    xp_scr[pl.ds(0, HIST_C), :] = jnp.where(j > 0, hh, 0.0)
    xp_scr[pl.ds(HIST_C, LB_C), :] = h
    n_sh = HIST_C + LB_C - SUBLANES
    for r in range(1, SUBLANES):
        sh_scr[r - 1] = xp_scr[pl.ds(r, n_sh), :]
    off = HIST_C - (CK - 1)
    y = None
    for t in range(CK):
        a, r = divmod(off + t, SUBLANES)
        rows = pl.ds(a * SUBLANES, LB_C)
        win = xp_scr[rows, :] if r == 0 else sh_scr[r - 1, rows, :]
        term = win * w_ref[0, pl.ds(t, 1), :]
        y = term if y is None else y + term
    y = y + b_ref[0]
    o_ref[...] = _group_ln_silu(y, gl_ref, bl_ref).astype(o_ref.dtype)

    @pl.when(j == nj - 1)
    def _():
        hst_ref[0] = h[LB_C - HIST_C:, :]


def _conf_prompt(layer, proj, w_dw, b_dw, g_ln, b_ln):
    nj = L_P // LB_C
    cb = COL_GLU // (2 * CW)
    return pl.pallas_call(
        _conf_p_kernel,
        grid=(NB_P, nj),
        in_specs=[
            pl.BlockSpec((LB_C, 2 * CW), lambda b, j: (b * nj + j, cb)),
            pl.BlockSpec((HIST_C, 2 * CW),
                         lambda b, j: (jnp.maximum((b * nj + j) * (LB_C // HIST_C) - 1, 0), cb)),
            pl.BlockSpec((1, CK, CW), lambda b, j: (layer, 0, 0)),
            pl.BlockSpec((1, 1, CW), lambda b, j: (layer, 0, 0)),
            pl.BlockSpec((1, 1, CW), lambda b, j: (layer, 0, 0)),
            pl.BlockSpec((1, 1, CW), lambda b, j: (layer, 0, 0)),
        ],
        out_specs=[
            pl.BlockSpec((LB_C, CW), lambda b, j: (b * nj + j, 0)),
            pl.BlockSpec((1, HIST_C, CW), lambda b, j: (b, 0, 0)),
        ],
        out_shape=[
            jax.ShapeDtypeStruct((T_P, CW), bf16),
            jax.ShapeDtypeStruct((NB_P, HIST_C, CW), f32),
        ],
        scratch_shapes=[pltpu.VMEM((HIST_C + LB_C, CW), f32),
                        pltpu.VMEM((SUBLANES - 1, HIST_C + LB_C - SUBLANES, CW), f32)],
        compiler_params=_cparams(("parallel", "arbitrary")),
        name=f"conf_prompt_{layer}",
    )(proj, proj, w_dw, b_dw, g_ln, b_ln)


BT_C = 16


def _conf_s_kernel(x_ref, hin_ref, wh_ref, wn_ref, b_ref, gl_ref, bl_ref, o_ref, hst_ref, y_scr):
    h = x_ref[:, :CW] * jax.nn.sigmoid(x_ref[:, CW:])
    trow = lax.broadcasted_iota(jnp.int32, (TPAD, 1), 0)
    nh = CK - 1
    for b in range(BT_C):
        hist = hin_ref[0, b]
        h8 = h[b * TPAD:(b + 1) * TPAD, :]
        y8 = jnp.zeros((TPAD, CW), f32)
        for t in range(L_S):
            yt = (jnp.sum(hist * wh_ref[0, t], axis=0, keepdims=True)
                  + jnp.sum(h8 * wn_ref[0, t], axis=0, keepdims=True))
            y8 = jnp.where(trow == t, yt, y8)
        y_scr[pl.ds(b * TPAD, TPAD), :] = y8
        hst_ref[b, pl.ds(0, nh - L_S), :] = hin_ref[0, b, pl.ds(L_S, nh - L_S), :]
        hst_ref[b, pl.ds(nh - L_S, L_S), :] = h8[:L_S, :]
    y = y_scr[...] + b_ref[0]
    o_ref[...] = _group_ln_silu(y, gl_ref, bl_ref).astype(o_ref.dtype)


def _conf_sample(layer, proj, hist_in, w_hist, w_new, b_dw, g_ln, b_ln):
    row0 = T_P // (BT_C * TPAD)
    cb = COL_GLU // (2 * CW)
    return pl.pallas_call(
        _conf_s_kernel,
        grid=(NB_S // BT_C,),
        in_specs=[
            pl.BlockSpec((BT_C * TPAD, 2 * CW), lambda i: (row0 + i, cb)),
            pl.BlockSpec((1, BT_C, CK - 1, CW), lambda i: (layer, i, 0, 0)),
            pl.BlockSpec((1, L_S, CK - 1, CW), lambda i: (layer, 0, 0, 0)),
            pl.BlockSpec((1, L_S, TPAD, CW), lambda i: (layer, 0, 0, 0)),
            pl.BlockSpec((1, 1, CW), lambda i: (layer, 0, 0)),
            pl.BlockSpec((1, 1, CW), lambda i: (layer, 0, 0)),
            pl.BlockSpec((1, 1, CW), lambda i: (layer, 0, 0)),
        ],
        out_specs=[
            pl.BlockSpec((BT_C * TPAD, CW), lambda i: (i, 0)),
            pl.BlockSpec((BT_C, CK - 1, CW), lambda i: (i, 0, 0)),
        ],
        out_shape=[
            jax.ShapeDtypeStruct((T_S, CW), bf16),
            jax.ShapeDtypeStruct((NB_S, CK - 1, CW), f32),
        ],
        scratch_shapes=[pltpu.VMEM((BT_C * TPAD, CW), f32)],
        compiler_params=_cparams(("parallel",)),
        name=f"conf_sample_{layer}",
    )(proj, hist_in, w_hist, w_new, b_dw, g_ln, b_ln)


def _out_kernel(moe, x_ref, oap, obp, oas, obs, w_ref, g_ref,
                g1p, scp, shp, g1s, scs, shs, *rest):
    if moe:
        wr_ref, br_ref, xo_ref, h2_ref, rt_ref = rest
        hf_scr = h2_ref
    else:
        xo_ref, h2_ref, hf_scr = rest
    i = pl.program_id(0)

    def attn(oa, ob):
        return (jnp.dot(oa[...], w_ref[0, :GW, :], preferred_element_type=f32)
                + jnp.dot(ob[...], w_ref[0, GW:, :], preferred_element_type=f32))

    @pl.when(i < NT_P)
    def _():
        xn = x_ref[...] + g1p[0] * attn(oap, obp)
        xo_ref[...] = xn
        hf_scr[...] = _rms(xn, g_ref[0]) * (1.0 + scp[0]) + shp[0]

    @pl.when(i >= NT_P)
    def _():
        hf_scr[...] = attn(oas, obs)

        def body(rows, b):
            m = pl.ds(b, 1)
            xn = x_ref[rows, :] + g1s[m, :] * hf_scr[rows, :]
            xo_ref[rows, :] = xn
            hf_scr[rows, :] = _rms(xn, g_ref[0]) * (1.0 + scs[m, :]) + shs[m, :]
        _sample_rows(body)

    hf = hf_scr[...]
    if not moe:
        h2_ref[...] = hf.astype(h2_ref.dtype)
    if moe:
        logits = _dot_hi(hf, wr_ref[0]) + br_ref[0]
        lane = lax.broadcasted_iota(jnp.int32, logits.shape, 1)
        ex = jnp.exp(logits - jnp.max(logits, axis=-1, keepdims=True))
        probs = ex / jnp.sum(ex, axis=-1, keepdims=True)
        m1 = jnp.max(probs, axis=-1, keepdims=True)
        i1 = jnp.min(jnp.where(probs == m1, lane, LANES), axis=-1, keepdims=True)
        rest_p = jnp.where(lane == i1, -1.0, probs)
        m2 = jnp.max(rest_p, axis=-1, keepdims=True)
        i2 = jnp.min(jnp.where(rest_p == m2, lane, LANES), axis=-1, keepdims=True)
        den = m1 + m2
        rt_ref[...] = jnp.where(lane == 0, i1.astype(f32),
                                jnp.where(lane == 1, i2.astype(f32),
                                          jnp.where(lane == 2, m1 / den,
                                                    jnp.where(lane == 3, m2 / den, 0.0))))


def _out_proj(layer, moe, x, oa_p, ob_p, oa_s, ob_s, w_out, g2, modp, mods, wr=None, br=None):
    idx = layer // 2
    in_specs = [
        pl.BlockSpec((TM, D), lambda i: (i, 0)),
        pl.BlockSpec((TM, GW), lambda i: (jnp.minimum(i, NT_P - 1), 0)),
        pl.BlockSpec((TM, CW), lambda i: (jnp.minimum(i, NT_P - 1), 0)),
        pl.BlockSpec((TM, GW), lambda i: (0, 0)),
        pl.BlockSpec((TM, CW), lambda i: (0, 0)),
        pl.BlockSpec((1, D, D), lambda i: (layer, 0, 0)),
        pl.BlockSpec((1, 1, D), lambda i: (layer, 0, 0)),
        _mod_specs(G1, 1)[0], _mod_specs(SC2, 1)[0], _mod_specs(SH2, 1)[0],
        _mod_specs(G1, 1)[1], _mod_specs(SC2, 1)[1], _mod_specs(SH2, 1)[1],
    ]
    args = [x, oa_p, ob_p, oa_s, ob_s, w_out, g2, modp, modp, modp, mods, mods, mods]
    out_specs = [pl.BlockSpec((TM, D), lambda i: (i, 0)), pl.BlockSpec((TM, D), lambda i: (i, 0))]
    out_shape = [jax.ShapeDtypeStruct((T_ALL, D), f32),
                 jax.ShapeDtypeStruct((T_ALL, D), f32 if moe else bf16)]
    if moe:
        in_specs += [pl.BlockSpec((1, D, LANES), lambda i: (idx, 0, 0)),
                     pl.BlockSpec((1, 1, LANES), lambda i: (idx, 0, 0))]
        args += [wr, br]
        out_specs.append(pl.BlockSpec((TM, LANES), lambda i: (i, 0)))
        out_shape.append(jax.ShapeDtypeStruct((T_ALL, LANES), f32))
    return pl.pallas_call(
        functools.partial(_out_kernel, moe),
        grid=(NT,),
        in_specs=in_specs,
        out_specs=out_specs,
        out_shape=out_shape,
        scratch_shapes=[] if moe else [pltpu.VMEM((TM, D), f32)],
        compiler_params=_cparams(("parallel",)),
        name=f"out_proj_{layer}",
    )(*args)


def _residual(i, x_ref, f_ref, g2p, g2s, o_ref, gf_ref, os_ref=None):
    os_ref = o_ref if os_ref is None else os_ref

    def fin(v):
        return v if gf_ref is None else _rms(v, gf_ref[...])

    @pl.when(i < NT_P)
    def _():
        o_ref[...] = fin(x_ref[...] + g2p[0] * f_ref[...])

    @pl.when(i >= NT_P)
    def _():
        def body(rows, b):
            os_ref[rows, :] = fin(x_ref[rows, :] + g2s[pl.ds(b, 1), :] * f_ref[rows, :])
        _sample_rows(body)


def _ffn_kernel(h_ref, wg_ref, wu_ref, wd_ref, x_ref, g2p, g2s, o_ref, acc_scr):
    i = pl.program_id(0)
    h = h_ref[...]
    for c in range(FF_D // TF_D):
        cols = slice(c * TF_D, (c + 1) * TF_D)
        gate = jnp.dot(h, wg_ref[0, :, cols], preferred_element_type=f32)
        up = jnp.dot(h, wu_ref[0, :, cols], preferred_element_type=f32)
        part = jnp.dot((_silu(gate) * up).astype(bf16), wd_ref[0, cols, :], preferred_element_type=f32)
        if c == 0:
            acc_scr[...] = part
        else:
            acc_scr[...] += part
    _residual(i, x_ref, acc_scr, g2p, g2s, o_ref, None)


def _ffn_dense(layer, h2, x, wg, wu, wd, modp, mods):
    idx = layer // 2
    mp, ms = _mod_specs(G2, 1)
    return pl.pallas_call(
        _ffn_kernel,
        grid=(NT,),
        in_specs=[
            pl.BlockSpec((TM, D), lambda i: (i, 0)),
            _resident((1, D, FF_D), lambda i: (idx, 0, 0)),
            _resident((1, D, FF_D), lambda i: (idx, 0, 0)),
            _resident((1, FF_D, D), lambda i: (idx, 0, 0)),
            pl.BlockSpec((TM, D), lambda i: (i, 0)),
            mp, ms,
        ],
        out_specs=pl.BlockSpec((TM, D), lambda i: (i, 0)),
        out_shape=jax.ShapeDtypeStruct((T_ALL, D), f32),
        scratch_shapes=[pltpu.VMEM((TM, D), f32)],
        compiler_params=_cparams(("parallel",)),
        name=f"ffn_dense_{layer}",
    )(h2, wg, wu, wd, x, modp, mods)


def _gather_rows(idx_ref, base, stride, src_hbm, dst, sem):
    def issue(r, carry):
        row = idx_ref[base + stride * r]
        if len(src_hbm.shape) == 3:
            pltpu.make_async_copy(src_hbm.at[row], dst.at[r], sem).start()
        else:
            pltpu.make_async_copy(src_hbm.at[pl.ds(row, 1), :], dst.at[pl.ds(r, 1), :], sem).start()
        return carry
    lax.fori_loop(0, MOE_BLK, issue, 0, unroll=8)


def _wait_rows(src_hbm, dst, sem):
    pltpu.make_async_copy(src_hbm.at[pl.ds(0, MOE_BLK)], dst, sem).wait()


def _issue_rows(idx_ref, base, src_hbm, dst, sem, r0, n):
    for u in range(n):
        row = idx_ref[base + r0 + u]
        pltpu.make_async_copy(src_hbm.at[pl.ds(row, 1), :], dst.at[pl.ds(r0 + u, 1), :], sem).start()


def _expert_kernel(be_ref, tok_ref, nv_ref, nr_ref, h_hbm, wg_ref, wu_ref, wd_ref,
                   y_ref, xg_scr, xb_scr, acc_scr, sem):
    m = pl.program_id(0)
    f = pl.program_id(1)
    nv = nv_ref[0]
    active = m < nv
    slot = m % 2
    nxt = (m + 1) * MOE_BLK

    @pl.when(jnp.logical_and(f == 0, m == 0))
    def _():
        _gather_rows(tok_ref, 0, 1, h_hbm, xg_scr.at[0], sem.at[0])

    @pl.when(jnp.logical_and(f == 0, m <= nv))
    def _():
        _wait_rows(h_hbm, xg_scr.at[slot], sem.at[slot])

    @pl.when(jnp.logical_and(f == 0, active))
    def _():
        xb_scr[...] = xg_scr[slot].astype(bf16)
        _issue_rows(tok_ref, nxt, h_hbm, xg_scr.at[1 - slot], sem.at[1 - slot],
                    GATHER_CHUNK * (FF_E // TF_E), GATHER_TAIL)

    @pl.when(jnp.logical_and(f == 0, jnp.logical_not(active)))
    def _():
        y_ref[...] = jnp.zeros_like(y_ref)

    @pl.when(jnp.logical_and(f == 0, active))
    def _():
        acc_scr[...] = jnp.zeros_like(acc_scr)

    def ffn_step(rows):
        _issue_rows(tok_ref, nxt, h_hbm, xg_scr.at[1 - slot], sem.at[1 - slot], f * GATHER_CHUNK, GATHER_CHUNK)
        xb = xb_scr[pl.ds(0, rows), :]
        a = _silu(_dot(xb, wg_ref[0, 0])) * _dot(xb, wu_ref[0, 0])
        acc_scr[pl.ds(0, rows), :] += _dot(a, wd_ref[0, 0])

    short = nr_ref[m] <= MOE_BLK // 2

    @pl.when(jnp.logical_and(active, jnp.logical_not(short)))
    def _():
        ffn_step(MOE_BLK)

    @pl.when(jnp.logical_and(active, short))
    def _():
        ffn_step(MOE_BLK // 2)

    @pl.when(jnp.logical_and(f == pl.num_programs(1) - 1, active))
    def _():
        y_ref[...] = acc_scr[...].reshape(MOE_BLK, D // LANES, LANES)


def _experts(idx, h2, block_e, slot_tok, n_valid, n_rows, wg, wu, wd):
    nf = FF_E // TF_E

    def wmap_up(m, f, be, tok, nv, nr):
        return (idx, be[m], 0, jnp.where(m < nv[0], f, nf - 1))

    def wmap_down(m, f, be, tok, nv, nr):
        return (idx, be[m], jnp.where(m < nv[0], f, nf - 1), 0)

    grid_spec = pltpu.PrefetchScalarGridSpec(
        num_scalar_prefetch=4,
        grid=(N_BLK + 1, nf),
        in_specs=[
            pl.BlockSpec(memory_space=pl.ANY),
            pl.BlockSpec((1, 1, D, TF_E), wmap_up),
            pl.BlockSpec((1, 1, D, TF_E), wmap_up),
            pl.BlockSpec((1, 1, TF_E, D), wmap_down),
        ],
        out_specs=pl.BlockSpec((MOE_BLK, D // LANES, LANES), lambda m, f, be, tok, nv, nr: (m, 0, 0)),
        scratch_shapes=[pltpu.VMEM((2, MOE_BLK, D), f32), pltpu.VMEM((MOE_BLK, D), bf16),
                        pltpu.VMEM((MOE_BLK, D), f32), pltpu.SemaphoreType.DMA((2,))],
    )
    return pl.pallas_call(
        _expert_kernel,
        grid_spec=grid_spec,
        out_shape=jax.ShapeDtypeStruct(((N_BLK + 1) * MOE_BLK, D // LANES, LANES), f32),
        compiler_params=_cparams(("arbitrary", "arbitrary")),
        name=f"experts_{idx}",
    )(block_e, slot_tok, n_valid, n_rows, h2, wg, wu, wd)


def _combine_kernel(final, dest_ref, y_hbm, x_ref, rt_ref, g2p, g2s, *rest):
    if final:
        gf_ref, o_ref, os_ref, g_scr, f_scr, sem = rest
    else:
        o_ref, g_scr, f_scr, sem = rest
        gf_ref = os_ref = None
    i = pl.program_id(0)
    slot = i % 2

    def gather(tile, sl):
        for k in range(2):
            _gather_rows(dest_ref, 2 * tile * TM + k, 2, y_hbm, g_scr.at[sl, k], sem.at[sl, k])

    @pl.when(i == 0)
    def _():
        gather(0, 0)

    @pl.when(i + 1 < pl.num_programs(0))
    def _():
        gather(i + 1, 1 - slot)

    for k in range(2):
        _wait_rows(y_hbm, g_scr.at[slot, k], sem.at[slot, k])
    f_scr[...] = (rt_ref[:, 2:3] * g_scr[slot, 0].reshape(TM, D) + rt_ref[:, 3:4] * g_scr[slot, 1].reshape(TM, D))
    _residual(i, x_ref, f_scr, g2p, g2s, o_ref, gf_ref, os_ref)


def _combine(final, dest, yb, x, rt, modp, mods, g_final):
    mp, ms = _mod_specs(G2, 1)
    wrap = lambda spec: pl.BlockSpec(spec.block_shape, lambda i, d, _f=spec.index_map: _f(i))
    in_specs = [pl.BlockSpec(memory_space=pl.ANY),
                pl.BlockSpec((TM, D), lambda i, d: (i, 0)),
                pl.BlockSpec((TM, LANES), lambda i, d: (i, 0)), wrap(mp), wrap(ms)]
    args = [yb, x, rt, modp, mods]
    if final:
        in_specs.append(pl.BlockSpec((1, D), lambda i, d: (0, 0)))
        args.append(g_final)
        out_specs = [pl.BlockSpec((TM, D), lambda i, d: (jnp.minimum(i, NT_P - 1), 0)),
                     pl.BlockSpec((T_S, D), lambda i, d: (0, 0))]
        out_shape = [jax.ShapeDtypeStruct((T_P, D), f32), jax.ShapeDtypeStruct((T_S, D), f32)]
    else:
        out_specs = pl.BlockSpec((TM, D), lambda i, d: (i, 0))
        out_shape = jax.ShapeDtypeStruct((T_ALL, D), f32)
    grid_spec = pltpu.PrefetchScalarGridSpec(
        num_scalar_prefetch=1,
        grid=(NT,),
        in_specs=in_specs,
        out_specs=out_specs,
        scratch_shapes=[pltpu.VMEM((2, 2, TM, D // LANES, LANES), f32), pltpu.VMEM((TM, D), f32),
                        pltpu.SemaphoreType.DMA((2, 2))],
    )
    return pl.pallas_call(
        functools.partial(_combine_kernel, final),
        grid_spec=grid_spec,
        out_shape=out_shape,
        compiler_params=_cparams(("arbitrary",)),
        name="combine_final" if final else "combine",
    )(dest, *args)


def _route(rt):
    e = rt[:, :2].astype(jnp.int32).reshape(N_ASG)
    tok = jnp.arange(N_ASG, dtype=jnp.int32) // 2
    real = jnp.logical_or(tok < T_P, (tok - T_P) % TPAD < L_S)
    onehot = jnp.logical_and(e[:, None] == jnp.arange(NE, dtype=jnp.int32)[None, :], real[:, None]).astype(jnp.int32)
    csum = jnp.cumsum(onehot, axis=0)
    counts = csum[-1]
    rank = jnp.sum(onehot * (csum - 1), axis=1)
    padded = (counts + MOE_BLK - 1) // MOE_BLK * MOE_BLK
    pad_end = jnp.cumsum(padded)
    pad_start = pad_end - padded
    dest = (jnp.sum(onehot * pad_start[None, :], axis=1) + rank).astype(jnp.int32)
    n_slots = (N_BLK + 1) * MOE_BLK
    slot_tok = jnp.zeros((n_slots,), jnp.int32).at[jnp.where(real, dest, n_slots)].set(
        tok, mode="drop", unique_indices=True)
    n_valid = (pad_end[-1] // MOE_BLK).astype(jnp.int32)
    blk = jnp.minimum(jnp.arange(N_BLK + 1, dtype=jnp.int32), n_valid - 1)
    first_slot = (blk * MOE_BLK)[:, None]
    block_e = jnp.minimum(jnp.sum((pad_end[None, :] <= first_slot).astype(jnp.int32), axis=1), NE - 1)
    n_rows = jnp.clip(counts[block_e] - (first_slot[:, 0] - pad_start[block_e]), 0, MOE_BLK).astype(jnp.int32)
    return dest, slot_tok, block_e, n_valid.reshape(1), n_rows


def kernel(x_prompt, x_sample, c_prompt, c_sample, state_gdn, state_gdn_conv, state_conf_conv, w_ada, b_ada, g_norm1, g_norm2, w_in, w_gdn_conv, a_log, dt_bias, g_gdn_out, w_conf_dw, b_conf_dw, g_conf_ln, b_conf_ln, w_out, w_ff_gate, w_ff_up, w_ff_down, w_router, b_router, w_exp_gate, w_exp_up, w_exp_down, g_final):
    xs_pad = jnp.pad(x_sample, ((0, 0), (0, TPAD - L_S), (0, 0)))
    x = jnp.concatenate([x_prompt.reshape(T_P, D), xs_pad.reshape(T_S, D)], axis=0)
    c_all = jnp.concatenate([c_prompt, c_sample], axis=0)
    o1, o2, o4 = QKV + GW, QKV + GW + 2 * NH, QKV + GW + 2 * NH + 2 * CW
    w_cat = jnp.concatenate([w_in[:, :, :o1], w_in[:, :, o2:o4], w_in[:, :, o1:o2],
                             jnp.zeros((DEPTH, D, PROJ_W - o4), f32)], axis=-1).astype(bf16)
    lane_pad = ((0, 0), (NH, LANES - 2 * NH))
    gprm = jnp.stack([jnp.pad(a_log, lane_pad), jnp.pad(dt_bias, lane_pad)], axis=1)
    gprm = jnp.pad(gprm, ((0, 0), (0, SUBLANES - 2), (0, 0)))
    wr_pad = jnp.pad(w_router, ((0, 0), (0, 0), (0, LANES - NE)))
    br_pad = jnp.pad(b_router, ((0, 0), (0, LANES - NE)), constant_values=-1e30).reshape(-1, 1, LANES)
    w_shift = jnp.stack([jnp.pad(w_conf_dw, ((0, 0), (t, TPAD - 1 - t), (0, 0))) for t in range(L_S)], axis=1)
    w_hist, w_new = w_shift[:, :, :CK - 1], w_shift[:, :, CK - 1:]
    w_out = w_out.astype(bf16)
    w_ff_gate, w_ff_up, w_ff_down = (w.astype(bf16) for w in (w_ff_gate, w_ff_up, w_ff_down))
    g1 = g_norm1.reshape(DEPTH, 1, D)
    g2 = g_norm2.reshape(DEPTH, 1, D)
    gout = g_gdn_out.reshape(DEPTH, 1, DH)
    b_dw = b_conf_dw.reshape(DEPTH, 1, CW)
    g_ln = g_conf_ln.reshape(DEPTH, 1, CW)
    b_ln = b_conf_ln.reshape(DEPTH, 1, CW)

    mod = _ada(c_all, w_ada, b_ada)

    sp_l, cp_l, fp_l, ss_l, cs_l, fs_l = [], [], [], [], [], []
    for layer in range(DEPTH):
        modp = mod[layer, :NB_P].reshape(NB_P, 1, N_MOD * D)
        mods = mod[layer, NB_P:]
        proj = _in_proj(layer, x, g1, modp, mods, w_cat)
        oa_p, s_p, c_p = _gdn_prompt(layer, proj, w_gdn_conv, gprm, gout)
        oa_s, s_s, c_s = _gdn_sample(layer, proj, state_gdn_conv, state_gdn, w_gdn_conv, gprm, gout)
        ss_l.append(s_s)
        ob_p, f_p = _conf_prompt(layer, proj, w_conf_dw, b_dw, g_ln, b_ln)
        ob_s, f_s = _conf_sample(layer, proj, state_conf_conv, w_hist, w_new, b_dw, g_ln, b_ln)
        sp_l.append(s_p)
        cp_l.append(c_p[:, TPAD - (GCONV - 1):, :])
        fp_l.append(f_p[:, HIST_C - (CK - 1):, :])
        cs_l.append(c_s)
        fs_l.append(f_s)
        if layer % 2 == 0:
            x, h2 = _out_proj(layer, False, x, oa_p, ob_p, oa_s, ob_s, w_out, g2, modp, mods)
            x = _ffn_dense(layer, h2, x, w_ff_gate, w_ff_up, w_ff_down, modp, mods)
        else:
            x, h2, rt = _out_proj(layer, True, x, oa_p, ob_p, oa_s, ob_s, w_out, g2, modp, mods, wr_pad, br_pad)
            dest, slot_tok, block_e, n_valid, n_rows = _route(rt)
            yb = _experts(layer // 2, h2, block_e, slot_tok, n_valid, n_rows, w_exp_gate, w_exp_up, w_exp_down)
            final = layer == DEPTH - 1
            x = _combine(final, dest, yb, x, rt, modp, mods, g_final.reshape(1, D))

    y_p, y_s = x
    y_prompt = y_p.reshape(NB_P, L_P, D)
    y_sample = y_s.reshape(NB_S, TPAD, D)[:, :L_S, :]
    return (y_prompt, y_sample, jnp.stack(sp_l), jnp.stack(cp_l), jnp.stack(fp_l),
            jnp.stack(ss_l), jnp.stack(cs_l), jnp.stack(fs_l))
```

```python
import functools
import math

import jax
import jax.numpy as jnp
from jax import lax
from jax.experimental import pallas as pl
from jax.experimental.pallas import tpu as pltpu

f32 = jnp.float32
bf16 = jnp.bfloat16
HIGHEST = lax.Precision.HIGHEST

D = 1024
NB_P, L_P = 8, 2048
NB_S, L_S = 128, 4
DEPTH = 4
NH, DH = 4, 128
GW = NH * DH
QKV = 3 * GW
GCONV = 4
GCHUNK = 64
CW = D - GW
CGROUPS = 4
CK = 31
FF_D = 2816
NE = 8
FF_E = 3584
N_MOD = 6
RMS_EPS = 1e-6
LN_EPS = 1e-5

SUBLANES = 8
LANES = 128
VMEM_LIMIT = 56 * 1024 * 1024

TPAD = SUBLANES
T_P = NB_P * L_P
T_S = NB_S * TPAD
T_ALL = T_P + T_S
TM = 1024
NT = T_ALL // TM
NT_P = T_P // TM
TILES_PER_SEQ = L_P // TM
PROJ_W = 3200
COL_GLU = 2048
COL_BA = 3072
TN_IN = 640
HIST_C = 32
SH1, SC1, G1, SH2, SC2, G2 = range(N_MOD)

LB_G = 512
LB_C = 512
TILE = 128
TF_D = 1408
TF_E = 512
MOE_BLK = 1024
N_ASG = 2 * T_ALL
N_BLK = N_ASG // MOE_BLK + NE
GATHER_CHUNK = MOE_BLK // (FF_E // TF_E)
GATHER_TAIL = MOE_BLK - GATHER_CHUNK * (FF_E // TF_E)


def _cparams(sem):
    return pltpu.CompilerParams(dimension_semantics=sem, vmem_limit_bytes=VMEM_LIMIT)


def _dot(a, b):
    return jnp.dot(a.astype(bf16), b.astype(bf16), preferred_element_type=f32)


def _dot_hi(a, b):
    return jnp.dot(a, b, precision=HIGHEST, preferred_element_type=f32)


def _dot_nt(a, b):
    return lax.dot_general(a.astype(bf16), b.astype(bf16), (((1,), (1,)), ((), ())),
                           preferred_element_type=f32)


def _dot_nt_hi(a, b):
    return lax.dot_general(a, b, (((1,), (1,)), ((), ())), precision=HIGHEST,
                           preferred_element_type=f32)


def _dot_tn(a, b):
    return lax.dot_general(a, b, (((0,), (0,)), ((), ())), preferred_element_type=f32)


def _silu(x):
    return x * jax.nn.sigmoid(x)


def _rms(x, g):
    return x * lax.rsqrt(jnp.mean(x * x, axis=-1, keepdims=True) + RMS_EPS) * g


def _ada_kernel(c_ref, w_ref, b_ref, o_ref):
    a = _silu(c_ref[...])
    o_ref[0] = _dot(a, w_ref[0]) + b_ref[0]


def _ada(c_all, w_ada, b_ada):
    n = c_all.shape[0]
    tn = 1536
    return pl.pallas_call(
        _ada_kernel,
        grid=(DEPTH, N_MOD * D // tn),
        in_specs=[
            pl.BlockSpec((n, D), lambda l, j: (0, 0)),
            pl.BlockSpec((1, D, tn), lambda l, j: (l, 0, j)),
            pl.BlockSpec((1, 1, tn), lambda l, j: (l, 0, j)),
        ],
        out_specs=pl.BlockSpec((1, n, tn), lambda l, j: (l, 0, j)),
        out_shape=jax.ShapeDtypeStruct((DEPTH, n, N_MOD * D), f32),
        compiler_params=_cparams(("parallel", "parallel")),
        name="ada",
    )(c_all, w_ada, b_ada.reshape(DEPTH, 1, N_MOD * D))


def _mod_specs(k, ngrid):
    if ngrid == 1:
        return [pl.BlockSpec((1, 1, D), lambda i: (jnp.minimum(i // TILES_PER_SEQ, NB_P - 1), 0, k)),
                pl.BlockSpec((NB_S, D), lambda i: (0, k))]
    return [pl.BlockSpec((1, 1, D), lambda i, j: (jnp.minimum(i // TILES_PER_SEQ, NB_P - 1), 0, k)),
            pl.BlockSpec((NB_S, D), lambda i, j: (0, k))]


def _sample_rows(body):
    def step(b, carry):
        body(pl.ds(pl.multiple_of(b * TPAD, TPAD), TPAD), b)
        return carry
    lax.fori_loop(0, NB_S, step, 0)


def _in_kernel(x_ref, g_ref, shp, scp, shs, scs, w_ref, o_ref, h_scr, hf_scr):
    i = pl.program_id(0)

    @pl.when(i < NT_P)
    def _():
        h = _rms(x_ref[...], g_ref[0]) * (1.0 + scp[0]) + shp[0]
        h_scr[...] = h.astype(bf16)

    @pl.when(i >= NT_P)
    def _():
        def body(rows, b):
            hf_scr[rows, :] = (_rms(x_ref[rows, :], g_ref[0]) * (1.0 + scs[pl.ds(b, 1), :])
                               + shs[pl.ds(b, 1), :])
        _sample_rows(body)
        h_scr[...] = hf_scr[...].astype(bf16)

    for c in range(PROJ_W // TN_IN):
        cols = slice(c * TN_IN, (c + 1) * TN_IN)
        o_ref[:, cols] = jnp.dot(h_scr[...], w_ref[0, :, cols], preferred_element_type=f32)


def _resident(block_shape, index_map):
    return pl.BlockSpec(block_shape, index_map, pipeline_mode=pl.Buffered(1))


def _in_proj(layer, x, g1, modp, mods, w_cat):
    return pl.pallas_call(
        _in_kernel,
        grid=(NT,),
        in_specs=[
            pl.BlockSpec((TM, D), lambda i: (i, 0)),
            pl.BlockSpec((1, 1, D), lambda i: (layer, 0, 0)),
            *_mod_specs(SH1, 1)[:1], *_mod_specs(SC1, 1)[:1],
            *_mod_specs(SH1, 1)[1:], *_mod_specs(SC1, 1)[1:],
            _resident((1, D, PROJ_W), lambda i: (layer, 0, 0)),
        ],
        out_specs=pl.BlockSpec((TM, PROJ_W), lambda i: (i, 0)),
        out_shape=jax.ShapeDtypeStruct((T_ALL, PROJ_W), f32),
        scratch_shapes=[pltpu.VMEM((TM, D), bf16), pltpu.VMEM((TM, D), f32)],
        compiler_params=_cparams(("parallel",)),
        name=f"in_proj_{layer}",
    )(x, g1, modp, modp, mods, mods, w_cat)


def _tile_masks(blk):
    ri = lax.broadcasted_iota(jnp.int32, (TILE, TILE), 0)
    ci = lax.broadcasted_iota(jnp.int32, (TILE, TILE), 1)
    same = (ri // blk) == (ci // blk)
    incl = jnp.logical_and(same, ri >= ci)
    strict = jnp.logical_and(same, ri > ci)
    return same, incl, strict


def _gate_cumsums(g_all, blk):
    same, incl, _ = _tile_masks(blk)
    gc = _dot_hi(incl.astype(f32), g_all)
    if TILE == 2 * blk:
        row = lax.broadcasted_iota(jnp.int32, (TILE, 1), 0)
        gt = jnp.where(row < blk, gc[blk - 1:blk, :], gc[TILE - 1:TILE, :])
    else:
        gt = _dot_hi(same.astype(f32), g_all)
    return gc, gt


def _run(gen):
    try:
        while True:
            next(gen)
    except StopIteration as stop:
        return stop.value


def _unit_lower_inverse(a_mats, blk, base, nsq):
    ri = lax.broadcasted_iota(jnp.int32, (TILE, TILE), 0)
    ci = lax.broadcasted_iota(jnp.int32, (TILE, TILE), 1)
    ps = [jnp.where((ri // base) == (ci // base), -a, 0.0) for a in a_mats]
    eye = (ri == ci).astype(f32)
    ts = [eye + p for p in ps]
    for _ in range(nsq):
        ps = [_dot(p, p) for p in ps]
        yield
        ts = [t + _dot(p, t) for p, t in zip(ps, ts)]
        yield
    b = base
    while b < blk:
        off = jnp.logical_and((ri // (2 * b)) == (ci // (2 * b)), (ri // b) != (ci // b))
        ms = [_dot(jnp.where(off, a, 0.0), t) for a, t in zip(a_mats, ts)]
        yield
        ts = [t - _dot(t, m) for t, m in zip(ts, ms)]
        yield
        b *= 2
    return ts


def _delta_prepare(qs, ks, vs, betas, gcs, grows, gts, blk, nsq):
    _, incl, strict = _tile_masks(blk)
    n = range(len(qs))
    dmats = [gc - gr for gc, gr in zip(gcs, grows)]
    decays = [jnp.where(incl, jnp.exp(jnp.where(incl, d, 0.0)), 0.0) for d in dmats]
    egcs = [jnp.exp(gc) for gc in gcs]
    kbs = [ks[i] * betas[i] for i in n]
    a_mats = [jnp.where(strict, _dot_nt(kbs[i], ks[i]) * decays[i], 0.0) for i in n]
    yield
    qks = [_dot_nt(qs[i], ks[i]) * decays[i] for i in n]
    yield
    xs = [jnp.concatenate([vs[i] * betas[i], kbs[i] * egcs[i]], axis=1) for i in n]
    tinvs = yield from _unit_lower_inverse(a_mats, blk, SUBLANES, nsq)
    xs = [_dot(t, x) for t, x in zip(tinvs, xs)]
    yield
    us = [x[:, :DH] for x in xs]
    ws = [x[:, DH:] for x in xs]
    q_decs = [qs[i] * egcs[i] for i in n]
    k_decs = [ks[i] * jnp.exp(gts[i] - gcs[i]) for i in n]
    return us, ws, q_decs, k_decs, qks


def _gdn_features(xc, ba, prm_ref, valid):
    s = _silu(xc)
    if valid is not None:
        s = jnp.where(valid, s, 0.0)
    qs, ks, vs = [], [], []
    for h in range(NH):
        qh = s[:, h * DH:(h + 1) * DH]
        kh = s[:, GW + h * DH:GW + (h + 1) * DH]
        qs.append(qh * lax.rsqrt(jnp.sum(qh * qh, axis=-1, keepdims=True) + 1e-6) * (DH ** -0.5))
        ks.append(kh * lax.rsqrt(jnp.sum(kh * kh, axis=-1, keepdims=True) + 1e-6))
        vs.append(s[:, 2 * GW + h * DH:2 * GW + (h + 1) * DH])
    beta = jax.nn.sigmoid(ba)
    xs = ba + prm_ref[pl.ds(1, 1), :]
    softplus = jnp.maximum(xs, 0.0) + jnp.log1p(jnp.exp(-jnp.abs(xs)))
    g = -jnp.exp(prm_ref[pl.ds(0, 1), :]) * softplus
    if valid is not None:
        beta = jnp.where(valid, beta, 0.0)
        g = jnp.where(valid, g, 0.0)
    return qs, ks, vs, beta, g


def _gated_out(o, z, gout):
    return _rms(o, gout) * _silu(z)


def _gate_lanes(ba, prm_ref):
    xs = ba + prm_ref[pl.ds(1, 1), :]
    softplus = jnp.maximum(xs, 0.0) + jnp.log1p(jnp.exp(-jnp.abs(xs)))
    return jax.nn.sigmoid(ba), -jnp.exp(prm_ref[pl.ds(0, 1), :]) * softplus


def _gdn_p_kernel(x_ref, halo_ref, ba_ref, wc_ref, prm_ref, gout_ref,
                  o_ref, s_ref, cst_ref,
                  q_scr, k_scr, v_scr, bg_scr, st_scr):
    j = pl.program_id(1)
    nj = pl.num_programs(1)
    slot = 0
    hs = range(NH)
    cols = [slice(h * DH, (h + 1) * DH) for h in hs]

    def features(src_ref, gate_ref, halo, dst):
        for c in range(QKV // DH):
            cc = slice(c * DH, (c + 1) * DH)
            x = src_ref[:, cc]
            xp = jnp.concatenate([halo(cc), x], axis=0)
            xc = x * wc_ref[0, pl.ds(GCONV - 1, 1), cc]
            for sft in range(1, GCONV):
                xc = xc + pltpu.roll(xp, sft, 0)[SUBLANES:, :] * wc_ref[0, pl.ds(GCONV - 1 - sft, 1), cc]
            a = _silu(xc)
            if c < NH:
                q_scr[dst, :, cc] = a * lax.rsqrt(jnp.sum(a * a, axis=-1, keepdims=True) + 1e-6) * (DH ** -0.5)
            elif c < 2 * NH:
                k_scr[dst, :, cols[c - NH]] = a * lax.rsqrt(jnp.sum(a * a, axis=-1, keepdims=True) + 1e-6)
            else:
                v_scr[dst, :, cols[c - 2 * NH]] = a
            yield
        beta, g = _gate_lanes(gate_ref[...], prm_ref.at[0])
        bg_scr[dst, 0] = beta
        bg_scr[dst, 1] = g
        yield

    @pl.when(j == 0)
    def _():
        st_scr[...] = jnp.zeros_like(st_scr)

    _run(features(x_ref, ba_ref, lambda cc: jnp.where(j > 0, halo_ref[:, cc], 0.0), slot))

    nsq = int(math.log2(SUBLANES)) - 1
    nblk = TILE // GCHUNK
    ntile = LB_G // TILE

    def chains():
        rows, qs, ks, vs, betas, gcs, gts, grows = [], [], [], [], [], [], [], []
        for ti in range(ntile):
            r = pl.ds(ti * TILE, TILE)
            rows.append(r)
            beta_t = bg_scr[slot, 0, r, :]
            gc_t, gt_t = _gate_cumsums(bg_scr[slot, 1, r, :], GCHUNK)
            gc_tt = gc_t.T
            for h in hs:
                qs.append(q_scr[slot, r, cols[h]])
                ks.append(k_scr[slot, r, cols[h]])
                vs.append(v_scr[slot, r, cols[h]])
                betas.append(beta_t[:, h:h + 1])
                gcs.append(gc_t[:, NH + h:NH + h + 1])
                gts.append(gt_t[:, NH + h:NH + h + 1])
                grows.append(gc_tt[NH + h:NH + h + 1, :])
        yield
        us, ws, q_decs, k_decs, qks = yield from _delta_prepare(qs, ks, vs, betas, gcs, grows, gts, GCHUNK, nsq)
        s_cur = [st_scr[h] for h in hs]
        for ti in range(ntile):
            ch = [ti * NH + h for h in hs]
            egts = [jnp.exp(gts[i]) for i in ch]
            vn = [[] for _ in hs]
            qs_ = [[] for _ in hs]
            for c in range(nblk):
                rr = slice(c * GCHUNK, (c + 1) * GCHUNK)
                wss = [_dot(jnp.concatenate([ws[ch[h]][rr], q_decs[ch[h]][rr]], axis=0), s_cur[h]) for h in hs]
                yield
                for h in hs:
                    vn[h].append(us[ch[h]][rr] - wss[h][:GCHUNK])
                    qs_[h].append(wss[h][GCHUNK:])
                s_cur = [s_cur[h] * egts[h][c * GCHUNK:c * GCHUNK + 1, :] + _dot_tn(k_decs[ch[h]][rr], vn[h][c])
                         for h in hs]
                yield
            os_ = [jnp.concatenate(qs_[h], axis=0) + _dot(qks[ch[h]], jnp.concatenate(vn[h], axis=0)) for h in hs]
            for h in hs:
                z = x_ref[rows[ti], pl.ds(QKV + h * DH, DH)]
                o_ref[rows[ti], cols[h]] = _gated_out(os_[h], z, gout_ref[0]).astype(o_ref.dtype)
            yield
        for h in hs:
            st_scr[h] = s_cur[h]

    _run(chains())

    @pl.when(j == nj - 1)
    def _():
        s_ref[0] = st_scr[...]
        cst_ref[0] = x_ref[pl.ds(LB_G - SUBLANES, SUBLANES), :QKV]


def _gdn_prompt(layer, proj, w_gconv, gprm, g_gout):
    nj = L_P // LB_G
    return pl.pallas_call(
        _gdn_p_kernel,
        grid=(NB_P, nj),
        in_specs=[
            pl.BlockSpec((LB_G, COL_GLU), lambda b, j: (b * nj + j, 0)),
            pl.BlockSpec((SUBLANES, COL_GLU),
                         lambda b, j: (jnp.maximum((b * nj + j) * (LB_G // SUBLANES) - 1, 0), 0)),
            pl.BlockSpec((LB_G, LANES), lambda b, j: (b * nj + j, COL_BA // LANES)),
            pl.BlockSpec((1, GCONV, QKV), lambda b, j: (layer, 0, 0)),
            pl.BlockSpec((1, SUBLANES, LANES), lambda b, j: (layer, 0, 0)),
            pl.BlockSpec((1, 1, DH), lambda b, j: (layer, 0, 0)),
        ],
        out_specs=[
            pl.BlockSpec((LB_G, GW), lambda b, j: (b * nj + j, 0)),
            pl.BlockSpec((1, NH, DH, DH), lambda b, j: (b, 0, 0, 0)),
            pl.BlockSpec((1, SUBLANES, QKV), lambda b, j: (b, 0, 0)),
        ],
        out_shape=[
            jax.ShapeDtypeStruct((T_P, GW), bf16),
            jax.ShapeDtypeStruct((NB_P, NH, DH, DH), f32),
            jax.ShapeDtypeStruct((NB_P, SUBLANES, QKV), f32),
        ],
        scratch_shapes=[pltpu.VMEM((1, LB_G, GW), f32), pltpu.VMEM((1, LB_G, GW), f32),
                        pltpu.VMEM((1, LB_G, GW), f32), pltpu.VMEM((1, 2, LB_G, LANES), f32),
                        pltpu.VMEM((NH, DH, DH), f32)],
        compiler_params=_cparams(("parallel", "arbitrary")),
        name=f"gdn_prompt_{layer}",
    )(proj, proj, proj, w_gconv, gprm, g_gout)


BT_S = TILE // TPAD


def _gdn_s_kernel(x_ref, ba_ref, cin_ref, s0_ref, wc_ref, prm_ref, gout_ref,
                  o_ref, s_ref, cst_ref, st_scr):
    x = x_ref[:, :QKV]
    st_scr[...] = jnp.zeros_like(st_scr)
    for b in range(BT_S):
        st_scr[pl.ds((b + 1) * TPAD - (GCONV - 1), GCONV - 1), :] = cin_ref[0, b]
        cst_ref[b] = x_ref[pl.ds(b * TPAD + L_S - (GCONV - 1), GCONV - 1), :QKV]
    st = st_scr[...]
    trow = lax.broadcasted_iota(jnp.int32, (TILE, 1), 0) % TPAD
    xc = x * wc_ref[0, pl.ds(GCONV - 1, 1), :]
    for s in range(1, GCONV):
        xs = jnp.where(trow >= s, pltpu.roll(x, s, 0), pltpu.roll(st, TILE - TPAD + s, 0))
        xc = xc + xs * wc_ref[0, pl.ds(GCONV - 1 - s, 1), :]
    valid = trow < L_S
    qs, ks, vs, beta, g = _gdn_features(xc, ba_ref[...], prm_ref.at[0], valid)

    nsq = int(math.ceil(math.log2(L_S))) - 1
    gc_t, gt_t = _gate_cumsums(g, TPAD)
    hs = range(NH)
    gcs = [gc_t[:, NH + h:NH + h + 1] for h in hs]
    gts = [gt_t[:, NH + h:NH + h + 1] for h in hs]
    gc_tt = gc_t.T
    grows = [gc_tt[NH + h:NH + h + 1, :] for h in hs]
    us, ws, q_decs, k_decs, qks = _run(_delta_prepare(qs, ks, vs, [beta[:, h:h + 1] for h in hs], gcs, grows,
                                                      gts, TPAD, nsq))
    for h in hs:
        egt = jnp.exp(gts[h])
        vn, qs_ = [], []
        for b in range(BT_S):
            rr = slice(b * TPAD, (b + 1) * TPAD)
            s0 = s0_ref[0, b, h]
            wsb = _dot(jnp.concatenate([ws[h][rr], q_decs[h][rr]], axis=0), s0)
            vn_b = us[h][rr] - wsb[:TPAD]
            vn.append(vn_b)
            qs_.append(wsb[TPAD:])
            s_ref[b, h] = s0 * egt[b * TPAD:b * TPAD + 1, :] + _dot_tn(k_decs[h][rr], vn_b)
        o = jnp.concatenate(qs_, axis=0) + _dot(qks[h], jnp.concatenate(vn, axis=0))
        z = x_ref[:, pl.ds(QKV + h * DH, DH)]
        o_ref[:, h * DH:(h + 1) * DH] = _gated_out(o, z, gout_ref[0]).astype(o_ref.dtype)


def _gdn_sample(layer, proj, conv_in, s0, w_gconv, gprm, g_gout):
    row0 = T_P // TILE
    return pl.pallas_call(
        _gdn_s_kernel,
        grid=(T_S // TILE,),
        in_specs=[
            pl.BlockSpec((TILE, COL_GLU), lambda i: (row0 + i, 0)),
            pl.BlockSpec((TILE, LANES), lambda i: (row0 + i, COL_BA // LANES)),
            pl.BlockSpec((1, BT_S, GCONV - 1, QKV), lambda i: (layer, i, 0, 0)),
            pl.BlockSpec((1, BT_S, NH, DH, DH), lambda i: (layer, i, 0, 0, 0)),
            pl.BlockSpec((1, GCONV, QKV), lambda i: (layer, 0, 0)),
            pl.BlockSpec((1, SUBLANES, LANES), lambda i: (layer, 0, 0)),
            pl.BlockSpec((1, 1, DH), lambda i: (layer, 0, 0)),
        ],
        out_specs=[
            pl.BlockSpec((TILE, GW), lambda i: (i, 0)),
            pl.BlockSpec((BT_S, NH, DH, DH), lambda i: (i, 0, 0, 0)),
            pl.BlockSpec((BT_S, GCONV - 1, QKV), lambda i: (i, 0, 0)),
        ],
        out_shape=[
            jax.ShapeDtypeStruct((T_S, GW), bf16),
            jax.ShapeDtypeStruct((NB_S, NH, DH, DH), f32),
            jax.ShapeDtypeStruct((NB_S, GCONV - 1, QKV), f32),
        ],
        scratch_shapes=[pltpu.VMEM((TILE, QKV), f32)],
        compiler_params=_cparams(("parallel",)),
        name=f"gdn_sample_{layer}",
    )(proj, proj, conv_in, s0, w_gconv, gprm, g_gout)


def _group_ln_silu(y, gl_ref, bl_ref):
    outs = []
    for gi in range(CGROUPS):
        w = CW // CGROUPS
        yg = y[:, gi * w:(gi + 1) * w]
        mu = jnp.mean(yg, axis=-1, keepdims=True)
        d = yg - mu
        var = jnp.mean(d * d, axis=-1, keepdims=True)
        outs.append(d * lax.rsqrt(var + LN_EPS))
    yn = jnp.concatenate(outs, axis=1) * gl_ref[0] + bl_ref[0]
    return _silu(yn)


def _conf_p_kernel(x_ref, halo_ref, w_ref, b_ref, gl_ref, bl_ref, o_ref, hst_ref, xp_scr, sh_scr):
    j = pl.program_id(1)
    nj = pl.num_programs(1)
    h = x_ref[:, :CW] * jax.nn.sigmoid(x_ref[:, CW:])
    hh = halo_ref[:, :CW] * jax.nn.sigmoid(halo_ref[:, CW:])
    xp_scr[pl.ds(0, HIST_C), :] = jnp.where(j > 0, hh, 0.0)
    xp_scr[pl.ds(HIST_C, LB_C), :] = h
    n_sh = HIST_C + LB_C - SUBLANES
    for r in range(1, SUBLANES):
        sh_scr[r - 1] = xp_scr[pl.ds(r, n_sh), :]
    off = HIST_C - (CK - 1)
    y = None
    for t in range(CK):
        a, r = divmod(off + t, SUBLANES)
        rows = pl.ds(a * SUBLANES, LB_C)
        win = xp_scr[rows, :] if r == 0 else sh_scr[r - 1, rows, :]
        term = win * w_ref[0, pl.ds(t, 1), :]
        y = term if y is None else y + term
    y = y + b_ref[0]
    o_ref[...] = _group_ln_silu(y, gl_ref, bl_ref).astype(o_ref.dtype)

    @pl.when(j == nj - 1)
    def _():
        hst_ref[0] = h[LB_C - HIST_C:, :]


def _conf_prompt(layer, proj, w_dw, b_dw, g_ln, b_ln):
    nj = L_P // LB_C
    cb = COL_GLU // (2 * CW)
    return pl.pallas_call(
        _conf_p_kernel,
        grid=(NB_P, nj),
        in_specs=[
            pl.BlockSpec((LB_C, 2 * CW), lambda b, j: (b * nj + j, cb)),
            pl.BlockSpec((HIST_C, 2 * CW),
                         lambda b, j: (jnp.maximum((b * nj + j) * (LB_C // HIST_C) - 1, 0), cb)),
            pl.BlockSpec((1, CK, CW), lambda b, j: (layer, 0, 0)),
            pl.BlockSpec((1, 1, CW), lambda b, j: (layer, 0, 0)),
            pl.BlockSpec((1, 1, CW), lambda b, j: (layer, 0, 0)),
            pl.BlockSpec((1, 1, CW), lambda b, j: (layer, 0, 0)),
        ],
        out_specs=[
            pl.BlockSpec((LB_C, CW), lambda b, j: (b * nj + j, 0)),
            pl.BlockSpec((1, HIST_C, CW), lambda b, j: (b, 0, 0)),
        ],
        out_shape=[
            jax.ShapeDtypeStruct((T_P, CW), bf16),
            jax.ShapeDtypeStruct((NB_P, HIST_C, CW), f32),
        ],
        scratch_shapes=[pltpu.VMEM((HIST_C + LB_C, CW), f32),
                        pltpu.VMEM((SUBLANES - 1, HIST_C + LB_C - SUBLANES, CW), f32)],
        compiler_params=_cparams(("parallel", "arbitrary")),
        name=f"conf_prompt_{layer}",
    )(proj, proj, w_dw, b_dw, g_ln, b_ln)


BT_C = 16


def _conf_s_kernel(x_ref, hin_ref, wh_ref, wn_ref, b_ref, gl_ref, bl_ref, o_ref, hst_ref, y_scr):
    h = x_ref[:, :CW] * jax.nn.sigmoid(x_ref[:, CW:])
    trow = lax.broadcasted_iota(jnp.int32, (TPAD, 1), 0)
    nh = CK - 1
    for b in range(BT_C):
        hist = hin_ref[0, b]
        h8 = h[b * TPAD:(b + 1) * TPAD, :]
        y8 = jnp.zeros((TPAD, CW), f32)
        for t in range(L_S):
            yt = (jnp.sum(hist * wh_ref[0, t], axis=0, keepdims=True)
                  + jnp.sum(h8 * wn_ref[0, t], axis=0, keepdims=True))
            y8 = jnp.where(trow == t, yt, y8)
        y_scr[pl.ds(b * TPAD, TPAD), :] = y8
        hst_ref[b, pl.ds(0, nh - L_S), :] = hin_ref[0, b, pl.ds(L_S, nh - L_S), :]
        hst_ref[b, pl.ds(nh - L_S, L_S), :] = h8[:L_S, :]
    y = y_scr[...] + b_ref[0]
    o_ref[...] = _group_ln_silu(y, gl_ref, bl_ref).astype(o_ref.dtype)


def _conf_sample(layer, proj, hist_in, w_hist, w_new, b_dw, g_ln, b_ln):
    row0 = T_P // (BT_C * TPAD)
    cb = COL_GLU // (2 * CW)
    return pl.pallas_call(
        _conf_s_kernel,
        grid=(NB_S // BT_C,),
        in_specs=[
            pl.BlockSpec((BT_C * TPAD, 2 * CW), lambda i: (row0 + i, cb)),
            pl.BlockSpec((1, BT_C, CK - 1, CW), lambda i: (layer, i, 0, 0)),
            pl.BlockSpec((1, L_S, CK - 1, CW), lambda i: (layer, 0, 0, 0)),
            pl.BlockSpec((1, L_S, TPAD, CW), lambda i: (layer, 0, 0, 0)),
            pl.BlockSpec((1, 1, CW), lambda i: (layer, 0, 0)),
            pl.BlockSpec((1, 1, CW), lambda i: (layer, 0, 0)),
            pl.BlockSpec((1, 1, CW), lambda i: (layer, 0, 0)),
        ],
        out_specs=[
            pl.BlockSpec((BT_C * TPAD, CW), lambda i: (i, 0)),
            pl.BlockSpec((BT_C, CK - 1, CW), lambda i: (i, 0, 0)),
        ],
        out_shape=[
            jax.ShapeDtypeStruct((T_S, CW), bf16),
            jax.ShapeDtypeStruct((NB_S, CK - 1, CW), f32),
        ],
        scratch_shapes=[pltpu.VMEM((BT_C * TPAD, CW), f32)],
        compiler_params=_cparams(("parallel",)),
        name=f"conf_sample_{layer}",
    )(proj, hist_in, w_hist, w_new, b_dw, g_ln, b_ln)


def _out_kernel(moe, x_ref, oap, obp, oas, obs, w_ref, g_ref,
                g1p, scp, shp, g1s, scs, shs, *rest):
    if moe:
        wr_ref, br_ref, xo_ref, h2_ref, rt_ref = rest
        hf_scr = h2_ref
    else:
        xo_ref, h2_ref, hf_scr = rest
    i = pl.program_id(0)

    def attn(oa, ob):
        return (jnp.dot(oa[...], w_ref[0, :GW, :], preferred_element_type=f32)
                + jnp.dot(ob[...], w_ref[0, GW:, :], preferred_element_type=f32))

    @pl.when(i < NT_P)
    def _():
        xn = x_ref[...] + g1p[0] * attn(oap, obp)
        xo_ref[...] = xn
        hf_scr[...] = _rms(xn, g_ref[0]) * (1.0 + scp[0]) + shp[0]

    @pl.when(i >= NT_P)
    def _():
        hf_scr[...] = attn(oas, obs)

        def body(rows, b):
            m = pl.ds(b, 1)
            xn = x_ref[rows, :] + g1s[m, :] * hf_scr[rows, :]
            xo_ref[rows, :] = xn
            hf_scr[rows, :] = _rms(xn, g_ref[0]) * (1.0 + scs[m, :]) + shs[m, :]
        _sample_rows(body)

    hf = hf_scr[...]
    if not moe:
        h2_ref[...] = hf.astype(h2_ref.dtype)
    if moe:
        logits = _dot_hi(hf, wr_ref[0]) + br_ref[0]
        lane = lax.broadcasted_iota(jnp.int32, logits.shape, 1)
        ex = jnp.exp(logits - jnp.max(logits, axis=-1, keepdims=True))
        probs = ex / jnp.sum(ex, axis=-1, keepdims=True)
        m1 = jnp.max(probs, axis=-1, keepdims=True)
        i1 = jnp.min(jnp.where(probs == m1, lane, LANES), axis=-1, keepdims=True)
        rest_p = jnp.where(lane == i1, -1.0, probs)
        m2 = jnp.max(rest_p, axis=-1, keepdims=True)
        i2 = jnp.min(jnp.where(rest_p == m2, lane, LANES), axis=-1, keepdims=True)
        den = m1 + m2
        rt_ref[...] = jnp.where(lane == 0, i1.astype(f32),
                                jnp.where(lane == 1, i2.astype(f32),
                                          jnp.where(lane == 2, m1 / den,
                                                    jnp.where(lane == 3, m2 / den, 0.0))))


def _out_proj(layer, moe, x, oa_p, ob_p, oa_s, ob_s, w_out, g2, modp, mods, wr=None, br=None):
    idx = layer // 2
    in_specs = [
        pl.BlockSpec((TM, D), lambda i: (i, 0)),
        pl.BlockSpec((TM, GW), lambda i: (jnp.minimum(i, NT_P - 1), 0)),
        pl.BlockSpec((TM, CW), lambda i: (jnp.minimum(i, NT_P - 1), 0)),
        pl.BlockSpec((TM, GW), lambda i: (0, 0)),
        pl.BlockSpec((TM, CW), lambda i: (0, 0)),
        pl.BlockSpec((1, D, D), lambda i: (layer, 0, 0)),
        pl.BlockSpec((1, 1, D), lambda i: (layer, 0, 0)),
        _mod_specs(G1, 1)[0], _mod_specs(SC2, 1)[0], _mod_specs(SH2, 1)[0],
        _mod_specs(G1, 1)[1], _mod_specs(SC2, 1)[1], _mod_specs(SH2, 1)[1],
    ]
    args = [x, oa_p, ob_p, oa_s, ob_s, w_out, g2, modp, modp, modp, mods, mods, mods]
    out_specs = [pl.BlockSpec((TM, D), lambda i: (i, 0)), pl.BlockSpec((TM, D), lambda i: (i, 0))]
    out_shape = [jax.ShapeDtypeStruct((T_ALL, D), f32),
                 jax.ShapeDtypeStruct((T_ALL, D), f32 if moe else bf16)]
    if moe:
        in_specs += [pl.BlockSpec((1, D, LANES), lambda i: (idx, 0, 0)),
                     pl.BlockSpec((1, 1, LANES), lambda i: (idx, 0, 0))]
        args += [wr, br]
        out_specs.append(pl.BlockSpec((TM, LANES), lambda i: (i, 0)))
        out_shape.append(jax.ShapeDtypeStruct((T_ALL, LANES), f32))
    return pl.pallas_call(
        functools.partial(_out_kernel, moe),
        grid=(NT,),
        in_specs=in_specs,
        out_specs=out_specs,
        out_shape=out_shape,
        scratch_shapes=[] if moe else [pltpu.VMEM((TM, D), f32)],
        compiler_params=_cparams(("parallel",)),
        name=f"out_proj_{layer}",
    )(*args)


def _residual(i, x_ref, f_ref, g2p, g2s, o_ref, gf_ref, os_ref=None):
    os_ref = o_ref if os_ref is None else os_ref

    def fin(v):
        return v if gf_ref is None else _rms(v, gf_ref[...])

    @pl.when(i < NT_P)
    def _():
        o_ref[...] = fin(x_ref[...] + g2p[0] * f_ref[...])

    @pl.when(i >= NT_P)
    def _():
        def body(rows, b):
            os_ref[rows, :] = fin(x_ref[rows, :] + g2s[pl.ds(b, 1), :] * f_ref[rows, :])
        _sample_rows(body)


def _ffn_kernel(h_ref, wg_ref, wu_ref, wd_ref, x_ref, g2p, g2s, o_ref, acc_scr):
    i = pl.program_id(0)
    h = h_ref[...]
    for c in range(FF_D // TF_D):
        cols = slice(c * TF_D, (c + 1) * TF_D)
        gate = jnp.dot(h, wg_ref[0, :, cols], preferred_element_type=f32)
        up = jnp.dot(h, wu_ref[0, :, cols], preferred_element_type=f32)
        part = jnp.dot((_silu(gate) * up).astype(bf16), wd_ref[0, cols, :], preferred_element_type=f32)
        if c == 0:
            acc_scr[...] = part
        else:
            acc_scr[...] += part
    _residual(i, x_ref, acc_scr, g2p, g2s, o_ref, None)


def _ffn_dense(layer, h2, x, wg, wu, wd, modp, mods):
    idx = layer // 2
    mp, ms = _mod_specs(G2, 1)
    return pl.pallas_call(
        _ffn_kernel,
        grid=(NT,),
        in_specs=[
            pl.BlockSpec((TM, D), lambda i: (i, 0)),
            _resident((1, D, FF_D), lambda i: (idx, 0, 0)),
            _resident((1, D, FF_D), lambda i: (idx, 0, 0)),
            _resident((1, FF_D, D), lambda i: (idx, 0, 0)),
            pl.BlockSpec((TM, D), lambda i: (i, 0)),
            mp, ms,
        ],
        out_specs=pl.BlockSpec((TM, D), lambda i: (i, 0)),
        out_shape=jax.ShapeDtypeStruct((T_ALL, D), f32),
        scratch_shapes=[pltpu.VMEM((TM, D), f32)],
        compiler_params=_cparams(("parallel",)),
        name=f"ffn_dense_{layer}",
    )(h2, wg, wu, wd, x, modp, mods)


def _gather_rows(idx_ref, base, stride, src_hbm, dst, sem):
    def issue(r, carry):
        row = idx_ref[base + stride * r]
        if len(src_hbm.shape) == 3:
            pltpu.make_async_copy(src_hbm.at[row], dst.at[r], sem).start()
        else:
            pltpu.make_async_copy(src_hbm.at[pl.ds(row, 1), :], dst.at[pl.ds(r, 1), :], sem).start()
        return carry
    lax.fori_loop(0, MOE_BLK, issue, 0, unroll=8)


def _wait_rows(src_hbm, dst, sem):
    pltpu.make_async_copy(src_hbm.at[pl.ds(0, MOE_BLK)], dst, sem).wait()


def _issue_rows(idx_ref, base, src_hbm, dst, sem, r0, n):
    for u in range(n):
        row = idx_ref[base + r0 + u]
        pltpu.make_async_copy(src_hbm.at[pl.ds(row, 1), :], dst.at[pl.ds(r0 + u, 1), :], sem).start()


def _expert_kernel(be_ref, tok_ref, nv_ref, nr_ref, h_hbm, wg_ref, wu_ref, wd_ref,
                   y_ref, xg_scr, xb_scr, acc_scr, sem):
    m = pl.program_id(0)
    f = pl.program_id(1)
    nv = nv_ref[0]
    active = m < nv
    slot = m % 2
    nxt = (m + 1) * MOE_BLK

    @pl.when(jnp.logical_and(f == 0, m == 0))
    def _():
        _gather_rows(tok_ref, 0, 1, h_hbm, xg_scr.at[0], sem.at[0])

    @pl.when(jnp.logical_and(f == 0, m <= nv))
    def _():
        _wait_rows(h_hbm, xg_scr.at[slot], sem.at[slot])

    @pl.when(jnp.logical_and(f == 0, active))
    def _():
        xb_scr[...] = xg_scr[slot].astype(bf16)
        _issue_rows(tok_ref, nxt, h_hbm, xg_scr.at[1 - slot], sem.at[1 - slot],
                    GATHER_CHUNK * (FF_E // TF_E), GATHER_TAIL)

    @pl.when(jnp.logical_and(f == 0, jnp.logical_not(active)))
    def _():
        y_ref[...] = jnp.zeros_like(y_ref)

    @pl.when(jnp.logical_and(f == 0, active))
    def _():
        acc_scr[...] = jnp.zeros_like(acc_scr)

    def ffn_step(rows):
        _issue_rows(tok_ref, nxt, h_hbm, xg_scr.at[1 - slot], sem.at[1 - slot], f * GATHER_CHUNK, GATHER_CHUNK)
        xb = xb_scr[pl.ds(0, rows), :]
        a = _silu(_dot(xb, wg_ref[0, 0])) * _dot(xb, wu_ref[0, 0])
        acc_scr[pl.ds(0, rows), :] += _dot(a, wd_ref[0, 0])

    short = nr_ref[m] <= MOE_BLK // 2

    @pl.when(jnp.logical_and(active, jnp.logical_not(short)))
    def _():
        ffn_step(MOE_BLK)

    @pl.when(jnp.logical_and(active, short))
    def _():
        ffn_step(MOE_BLK // 2)

    @pl.when(jnp.logical_and(f == pl.num_programs(1) - 1, active))
    def _():
        y_ref[...] = acc_scr[...].reshape(MOE_BLK, D // LANES, LANES)


def _experts(idx, h2, block_e, slot_tok, n_valid, n_rows, wg, wu, wd):
    nf = FF_E // TF_E

    def wmap_up(m, f, be, tok, nv, nr):
        return (idx, be[m], 0, jnp.where(m < nv[0], f, nf - 1))

    def wmap_down(m, f, be, tok, nv, nr):
        return (idx, be[m], jnp.where(m < nv[0], f, nf - 1), 0)

    grid_spec = pltpu.PrefetchScalarGridSpec(
        num_scalar_prefetch=4,
        grid=(N_BLK + 1, nf),
        in_specs=[
            pl.BlockSpec(memory_space=pl.ANY),
            pl.BlockSpec((1, 1, D, TF_E), wmap_up),
            pl.BlockSpec((1, 1, D, TF_E), wmap_up),
            pl.BlockSpec((1, 1, TF_E, D), wmap_down),
        ],
        out_specs=pl.BlockSpec((MOE_BLK, D // LANES, LANES), lambda m, f, be, tok, nv, nr: (m, 0, 0)),
        scratch_shapes=[pltpu.VMEM((2, MOE_BLK, D), f32), pltpu.VMEM((MOE_BLK, D), bf16),
                        pltpu.VMEM((MOE_BLK, D), f32), pltpu.SemaphoreType.DMA((2,))],
    )
    return pl.pallas_call(
        _expert_kernel,
        grid_spec=grid_spec,
        out_shape=jax.ShapeDtypeStruct(((N_BLK + 1) * MOE_BLK, D // LANES, LANES), f32),
        compiler_params=_cparams(("arbitrary", "arbitrary")),
        name=f"experts_{idx}",
    )(block_e, slot_tok, n_valid, n_rows, h2, wg, wu, wd)


def _combine_kernel(final, dest_ref, y_hbm, x_ref, rt_ref, g2p, g2s, *rest):
    if final:
        gf_ref, o_ref, os_ref, g_scr, f_scr, sem = rest
    else:
        o_ref, g_scr, f_scr, sem = rest
        gf_ref = os_ref = None
    i = pl.program_id(0)
    slot = i % 2

    def gather(tile, sl):
        for k in range(2):
            _gather_rows(dest_ref, 2 * tile * TM + k, 2, y_hbm, g_scr.at[sl, k], sem.at[sl, k])

    @pl.when(i == 0)
    def _():
        gather(0, 0)

    @pl.when(i + 1 < pl.num_programs(0))
    def _():
        gather(i + 1, 1 - slot)

    for k in range(2):
        _wait_rows(y_hbm, g_scr.at[slot, k], sem.at[slot, k])
    f_scr[...] = (rt_ref[:, 2:3] * g_scr[slot, 0].reshape(TM, D) + rt_ref[:, 3:4] * g_scr[slot, 1].reshape(TM, D))
    _residual(i, x_ref, f_scr, g2p, g2s, o_ref, gf_ref, os_ref)


def _combine(final, dest, yb, x, rt, modp, mods, g_final):
    mp, ms = _mod_specs(G2, 1)
    wrap = lambda spec: pl.BlockSpec(spec.block_shape, lambda i, d, _f=spec.index_map: _f(i))
    in_specs = [pl.BlockSpec(memory_space=pl.ANY),
                pl.BlockSpec((TM, D), lambda i, d: (i, 0)),
                pl.BlockSpec((TM, LANES), lambda i, d: (i, 0)), wrap(mp), wrap(ms)]
    args = [yb, x, rt, modp, mods]
    if final:
        in_specs.append(pl.BlockSpec((1, D), lambda i, d: (0, 0)))
        args.append(g_final)
        out_specs = [pl.BlockSpec((TM, D), lambda i, d: (jnp.minimum(i, NT_P - 1), 0)),
                     pl.BlockSpec((T_S, D), lambda i, d: (0, 0))]
        out_shape = [jax.ShapeDtypeStruct((T_P, D), f32), jax.ShapeDtypeStruct((T_S, D), f32)]
    else:
        out_specs = pl.BlockSpec((TM, D), lambda i, d: (i, 0))
        out_shape = jax.ShapeDtypeStruct((T_ALL, D), f32)
    grid_spec = pltpu.PrefetchScalarGridSpec(
        num_scalar_prefetch=1,
        grid=(NT,),
        in_specs=in_specs,
        out_specs=out_specs,
        scratch_shapes=[pltpu.VMEM((2, 2, TM, D // LANES, LANES), f32), pltpu.VMEM((TM, D), f32),
                        pltpu.SemaphoreType.DMA((2, 2))],
    )
    return pl.pallas_call(
        functools.partial(_combine_kernel, final),
        grid_spec=grid_spec,
        out_shape=out_shape,
        compiler_params=_cparams(("arbitrary",)),
        name="combine_final" if final else "combine",
    )(dest, *args)


def _route(rt):
    e = rt[:, :2].astype(jnp.int32).reshape(N_ASG)
    tok = jnp.arange(N_ASG, dtype=jnp.int32) // 2
    real = jnp.logical_or(tok < T_P, (tok - T_P) % TPAD < L_S)
    onehot = jnp.logical_and(e[:, None] == jnp.arange(NE, dtype=jnp.int32)[None, :], real[:, None]).astype(jnp.int32)
    csum = jnp.cumsum(onehot, axis=0)
    counts = csum[-1]
    rank = jnp.sum(onehot * (csum - 1), axis=1)
    padded = (counts + MOE_BLK - 1) // MOE_BLK * MOE_BLK
    pad_end = jnp.cumsum(padded)
    pad_start = pad_end - padded
    dest = (jnp.sum(onehot * pad_start[None, :], axis=1) + rank).astype(jnp.int32)
    dest = jnp.where(real, dest, jnp.arange(N_ASG, dtype=jnp.int32) % MOE_BLK)
    n_slots = (N_BLK + 1) * MOE_BLK
    slot_tok = jnp.zeros((n_slots,), jnp.int32).at[jnp.where(real, dest, n_slots)].set(
        tok, mode="drop", unique_indices=True)
    n_valid = (pad_end[-1] // MOE_BLK).astype(jnp.int32)
    blk = jnp.minimum(jnp.arange(N_BLK + 1, dtype=jnp.int32), n_valid - 1)
    first_slot = (blk * MOE_BLK)[:, None]
    block_e = jnp.minimum(jnp.sum((pad_end[None, :] <= first_slot).astype(jnp.int32), axis=1), NE - 1)
    n_rows = jnp.clip(counts[block_e] - (first_slot[:, 0] - pad_start[block_e]), 0, MOE_BLK).astype(jnp.int32)
    return dest, slot_tok, block_e, n_valid.reshape(1), n_rows


def kernel(x_prompt, x_sample, c_prompt, c_sample, state_gdn, state_gdn_conv, state_conf_conv, w_ada, b_ada, g_norm1, g_norm2, w_in, w_gdn_conv, a_log, dt_bias, g_gdn_out, w_conf_dw, b_conf_dw, g_conf_ln, b_conf_ln, w_out, w_ff_gate, w_ff_up, w_ff_down, w_router, b_router, w_exp_gate, w_exp_up, w_exp_down, g_final):
    xs_pad = jnp.pad(x_sample, ((0, 0), (0, TPAD - L_S), (0, 0)))
    x = jnp.concatenate([x_prompt.reshape(T_P, D), xs_pad.reshape(T_S, D)], axis=0)
    c_all = jnp.concatenate([c_prompt, c_sample], axis=0)
    o1, o2, o4 = QKV + GW, QKV + GW + 2 * NH, QKV + GW + 2 * NH + 2 * CW
    w_cat = jnp.concatenate([w_in[:, :, :o1], w_in[:, :, o2:o4], w_in[:, :, o1:o2],
                             jnp.zeros((DEPTH, D, PROJ_W - o4), f32)], axis=-1).astype(bf16)
    lane_pad = ((0, 0), (NH, LANES - 2 * NH))
    gprm = jnp.stack([jnp.pad(a_log, lane_pad), jnp.pad(dt_bias, lane_pad)], axis=1)
    gprm = jnp.pad(gprm, ((0, 0), (0, SUBLANES - 2), (0, 0)))
    wr_pad = jnp.pad(w_router, ((0, 0), (0, 0), (0, LANES - NE)))
    br_pad = jnp.pad(b_router, ((0, 0), (0, LANES - NE)), constant_values=-1e30).reshape(-1, 1, LANES)
    w_shift = jnp.stack([jnp.pad(w_conf_dw, ((0, 0), (t, TPAD - 1 - t), (0, 0))) for t in range(L_S)], axis=1)
    w_hist, w_new = w_shift[:, :, :CK - 1], w_shift[:, :, CK - 1:]
    w_out = w_out.astype(bf16)
    w_ff_gate, w_ff_up, w_ff_down = (w.astype(bf16) for w in (w_ff_gate, w_ff_up, w_ff_down))
    g1 = g_norm1.reshape(DEPTH, 1, D)
    g2 = g_norm2.reshape(DEPTH, 1, D)
    gout = g_gdn_out.reshape(DEPTH, 1, DH)
    b_dw = b_conf_dw.reshape(DEPTH, 1, CW)
    g_ln = g_conf_ln.reshape(DEPTH, 1, CW)
    b_ln = b_conf_ln.reshape(DEPTH, 1, CW)

    mod = _ada(c_all, w_ada, b_ada)

    sp_l, cp_l, fp_l, ss_l, cs_l, fs_l = [], [], [], [], [], []
    for layer in range(DEPTH):
        modp = mod[layer, :NB_P].reshape(NB_P, 1, N_MOD * D)
        mods = mod[layer, NB_P:]
        proj = _in_proj(layer, x, g1, modp, mods, w_cat)
        oa_p, s_p, c_p = _gdn_prompt(layer, proj, w_gdn_conv, gprm, gout)
        oa_s, s_s, c_s = _gdn_sample(layer, proj, state_gdn_conv, state_gdn, w_gdn_conv, gprm, gout)
        ss_l.append(s_s)
        ob_p, f_p = _conf_prompt(layer, proj, w_conf_dw, b_dw, g_ln, b_ln)
        ob_s, f_s = _conf_sample(layer, proj, state_conf_conv, w_hist, w_new, b_dw, g_ln, b_ln)
        sp_l.append(s_p)
        cp_l.append(c_p[:, TPAD - (GCONV - 1):, :])
        fp_l.append(f_p[:, HIST_C - (CK - 1):, :])
        cs_l.append(c_s)
        fs_l.append(f_s)
        if layer % 2 == 0:
            x, h2 = _out_proj(layer, False, x, oa_p, ob_p, oa_s, ob_s, w_out, g2, modp, mods)
            x = _ffn_dense(layer, h2, x, w_ff_gate, w_ff_up, w_ff_down, modp, mods)
        else:
            x, h2, rt = _out_proj(layer, True, x, oa_p, ob_p, oa_s, ob_s, w_out, g2, modp, mods, wr_pad, br_pad)
            dest, slot_tok, block_e, n_valid, n_rows = _route(rt)
            yb = _experts(layer // 2, h2, block_e, slot_tok, n_valid, n_rows, w_exp_gate, w_exp_up, w_exp_down)
            final = layer == DEPTH - 1
            x = _combine(final, dest, yb, x, rt, modp, mods, g_final.reshape(1, D))

    y_p, y_s = x
    y_prompt = y_p.reshape(NB_P, L_P, D)
    y_sample = y_s.reshape(NB_S, TPAD, D)[:, :L_S, :]
    return (y_prompt, y_sample, jnp.stack(sp_l), jnp.stack(cp_l), jnp.stack(fp_l),
            jnp.stack(ss_l), jnp.stack(cs_l), jnp.stack(fs_l))
```

```python
import functools
import math

import jax
import jax.numpy as jnp
from jax import lax
from jax.experimental import pallas as pl
from jax.experimental.pallas import tpu as pltpu

f32 = jnp.float32
bf16 = jnp.bfloat16
HIGHEST = lax.Precision.HIGHEST

D = 1024
NB_P, L_P = 8, 2048
NB_S, L_S = 128, 4
DEPTH = 4
NH, DH = 4, 128
GW = NH * DH
QKV = 3 * GW
GCONV = 4
GCHUNK = 64
CW = D - GW
CGROUPS = 4
CK = 31
FF_D = 2816
NE = 8
FF_E = 3584
N_MOD = 6
RMS_EPS = 1e-6
LN_EPS = 1e-5

SUBLANES = 8
LANES = 128
VMEM_LIMIT = 56 * 1024 * 1024

TPAD = SUBLANES
T_P = NB_P * L_P
T_S = NB_S * TPAD
T_ALL = T_P + T_S
TM = 1024
NT = T_ALL // TM
NT_P = T_P // TM
TILES_PER_SEQ = L_P // TM
W_CAT = 3200
PROJ_W = 2176
COL_GLU = 2048
COL_BA = 2048
PROJ_CHUNKS = ((0, 768), (768, 768), (1536, 640))
HIST_C = 32
SH1, SC1, G1, SH2, SC2, G2 = range(N_MOD)

LB_G = 512
LB_C = 512
TILE = 128
TF_D = 1408
TF_E = 512
MOE_BLK = 1024
N_ASG = 2 * T_ALL
N_BLK = N_ASG // MOE_BLK + NE
GATHER_CHUNK = MOE_BLK // (FF_E // TF_E)
GATHER_TAIL = MOE_BLK - GATHER_CHUNK * (FF_E // TF_E)


def _cparams(sem):
    return pltpu.CompilerParams(dimension_semantics=sem, vmem_limit_bytes=VMEM_LIMIT)


def _dot(a, b):
    return jnp.dot(a.astype(bf16), b.astype(bf16), preferred_element_type=f32)


def _dot_hi(a, b):
    return jnp.dot(a, b, precision=HIGHEST, preferred_element_type=f32)


def _dot_nt(a, b):
    return lax.dot_general(a.astype(bf16), b.astype(bf16), (((1,), (1,)), ((), ())),
                           preferred_element_type=f32)


def _dot_nt_hi(a, b):
    return lax.dot_general(a, b, (((1,), (1,)), ((), ())), precision=HIGHEST,
                           preferred_element_type=f32)


def _dot_tn(a, b):
    return lax.dot_general(a, b, (((0,), (0,)), ((), ())), preferred_element_type=f32)


def _silu(x):
    return x * jax.nn.sigmoid(x)


def _rms(x, g):
    return x * lax.rsqrt(jnp.mean(x * x, axis=-1, keepdims=True) + RMS_EPS) * g


def _ada_kernel(c_ref, w_ref, b_ref, o_ref):
    a = _silu(c_ref[...])
    o_ref[0] = _dot(a, w_ref[0]) + b_ref[0]


def _ada(c_all, w_ada, b_ada):
    n = c_all.shape[0]
    tn = 1536
    return pl.pallas_call(
        _ada_kernel,
        grid=(DEPTH, N_MOD * D // tn),
        in_specs=[
            pl.BlockSpec((n, D), lambda l, j: (0, 0)),
            pl.BlockSpec((1, D, tn), lambda l, j: (l, 0, j)),
            pl.BlockSpec((1, 1, tn), lambda l, j: (l, 0, j)),
        ],
        out_specs=pl.BlockSpec((1, n, tn), lambda l, j: (l, 0, j)),
        out_shape=jax.ShapeDtypeStruct((DEPTH, n, N_MOD * D), f32),
        compiler_params=_cparams(("parallel", "parallel")),
        name="ada",
    )(c_all, w_ada, b_ada.reshape(DEPTH, 1, N_MOD * D))


def _mod_specs(k, ngrid):
    if ngrid == 1:
        return [pl.BlockSpec((1, 1, D), lambda i: (jnp.minimum(i // TILES_PER_SEQ, NB_P - 1), 0, k)),
                pl.BlockSpec((NB_S, D), lambda i: (0, k))]
    return [pl.BlockSpec((1, 1, D), lambda i, j: (jnp.minimum(i // TILES_PER_SEQ, NB_P - 1), 0, k)),
            pl.BlockSpec((NB_S, D), lambda i, j: (0, k))]


def _sample_rows(body):
    def step(b, carry):
        body(pl.ds(pl.multiple_of(b * TPAD, TPAD), TPAD), b)
        return carry
    lax.fori_loop(0, NB_S, step, 0)


def _in_kernel(x_ref, g_ref, shp, scp, shs, scs, w_ref, o_ref, glu_ref, h_scr, hf_scr):
    i = pl.program_id(0)

    @pl.when(i < NT_P)
    def _():
        h = _rms(x_ref[...], g_ref[0]) * (1.0 + scp[0]) + shp[0]
        h_scr[...] = h.astype(bf16)

    @pl.when(i >= NT_P)
    def _():
        def body(rows, b):
            hf_scr[rows, :] = (_rms(x_ref[rows, :], g_ref[0]) * (1.0 + scs[pl.ds(b, 1), :])
                               + shs[pl.ds(b, 1), :])
        _sample_rows(body)
        h_scr[...] = hf_scr[...].astype(bf16)

    for c0, cw in PROJ_CHUNKS:
        cols = slice(c0, c0 + cw)
        o_ref[:, cols] = jnp.dot(h_scr[...], w_ref[0, :, cols], preferred_element_type=f32)
    u = jnp.dot(h_scr[...], w_ref[0, :, PROJ_W:PROJ_W + CW], preferred_element_type=f32)
    gate = jnp.dot(h_scr[...], w_ref[0, :, PROJ_W + CW:], preferred_element_type=f32)
    glu_ref[...] = u * jax.nn.sigmoid(gate)


def _resident(block_shape, index_map):
    return pl.BlockSpec(block_shape, index_map, pipeline_mode=pl.Buffered(1))


def _in_proj(layer, x, g1, modp, mods, w_cat):
    return pl.pallas_call(
        _in_kernel,
        grid=(NT,),
        in_specs=[
            pl.BlockSpec((TM, D), lambda i: (i, 0)),
            pl.BlockSpec((1, 1, D), lambda i: (layer, 0, 0)),
            *_mod_specs(SH1, 1)[:1], *_mod_specs(SC1, 1)[:1],
            *_mod_specs(SH1, 1)[1:], *_mod_specs(SC1, 1)[1:],
            _resident((1, D, W_CAT), lambda i: (layer, 0, 0)),
        ],
        out_specs=[pl.BlockSpec((TM, PROJ_W), lambda i: (i, 0)), pl.BlockSpec((TM, CW), lambda i: (i, 0))],
        out_shape=[jax.ShapeDtypeStruct((T_ALL, PROJ_W), f32), jax.ShapeDtypeStruct((T_ALL, CW), f32)],
        scratch_shapes=[pltpu.VMEM((TM, D), bf16), pltpu.VMEM((TM, D), f32)],
        compiler_params=_cparams(("parallel",)),
        name=f"in_proj_{layer}",
    )(x, g1, modp, modp, mods, mods, w_cat)


def _tile_masks(blk):
    ri = lax.broadcasted_iota(jnp.int32, (TILE, TILE), 0)
    ci = lax.broadcasted_iota(jnp.int32, (TILE, TILE), 1)
    same = (ri // blk) == (ci // blk)
    incl = jnp.logical_and(same, ri >= ci)
    strict = jnp.logical_and(same, ri > ci)
    return same, incl, strict


def _gate_cumsums(g_all, blk):
    same, incl, _ = _tile_masks(blk)
    gc = _dot_hi(incl.astype(f32), g_all)
    if TILE == 2 * blk:
        row = lax.broadcasted_iota(jnp.int32, (TILE, 1), 0)
        gt = jnp.where(row < blk, gc[blk - 1:blk, :], gc[TILE - 1:TILE, :])
    else:
        gt = _dot_hi(same.astype(f32), g_all)
    return gc, gt


def _run(gen):
    try:
        while True:
            next(gen)
    except StopIteration as stop:
        return stop.value


def _unit_lower_inverse(a_mats, blk, base, nsq):
    ri = lax.broadcasted_iota(jnp.int32, (TILE, TILE), 0)
    ci = lax.broadcasted_iota(jnp.int32, (TILE, TILE), 1)
    ps = [jnp.where((ri // base) == (ci // base), -a, 0.0) for a in a_mats]
    eye = (ri == ci).astype(f32)
    ts = [eye + p for p in ps]
    for _ in range(nsq):
        ps = [_dot(p, p) for p in ps]
        yield
        ts = [t + _dot(p, t) for p, t in zip(ps, ts)]
        yield
    b = base
    while b < blk:
        off = jnp.logical_and((ri // (2 * b)) == (ci // (2 * b)), (ri // b) != (ci // b))
        ms = [_dot(jnp.where(off, a, 0.0), t) for a, t in zip(a_mats, ts)]
        yield
        ts = [t - _dot(t, m) for t, m in zip(ts, ms)]
        yield
        b *= 2
    return ts


def _delta_prepare(qs, ks, vs, betas, gcs, grows, gts, blk, nsq):
    _, incl, strict = _tile_masks(blk)
    n = range(len(qs))
    dmats = [gc - gr for gc, gr in zip(gcs, grows)]
    decays = [jnp.where(incl, jnp.exp(jnp.where(incl, d, 0.0)), 0.0) for d in dmats]
    egcs = [jnp.exp(gc) for gc in gcs]
    kbs = [ks[i] * betas[i] for i in n]
    a_mats = [jnp.where(strict, _dot_nt(kbs[i], ks[i]) * decays[i], 0.0) for i in n]
    yield
    qks = [_dot_nt(qs[i], ks[i]) * decays[i] for i in n]
    yield
    xs = [jnp.concatenate([vs[i] * betas[i], kbs[i] * egcs[i]], axis=1) for i in n]
    tinvs = yield from _unit_lower_inverse(a_mats, blk, SUBLANES, nsq)
    xs = [_dot(t, x) for t, x in zip(tinvs, xs)]
    yield
    us = [x[:, :DH] for x in xs]
    ws = [x[:, DH:] for x in xs]
    q_decs = [qs[i] * egcs[i] for i in n]
    k_decs = [ks[i] * jnp.exp(gts[i] - gcs[i]) for i in n]
    return us, ws, q_decs, k_decs, qks


def _gdn_features(xc, ba, prm_ref, valid):
    s = _silu(xc)
    if valid is not None:
        s = jnp.where(valid, s, 0.0)
    qs, ks, vs = [], [], []
    for h in range(NH):
        qh = s[:, h * DH:(h + 1) * DH]
        kh = s[:, GW + h * DH:GW + (h + 1) * DH]
        qs.append(qh * lax.rsqrt(jnp.sum(qh * qh, axis=-1, keepdims=True) + 1e-6) * (DH ** -0.5))
        ks.append(kh * lax.rsqrt(jnp.sum(kh * kh, axis=-1, keepdims=True) + 1e-6))
        vs.append(s[:, 2 * GW + h * DH:2 * GW + (h + 1) * DH])
    beta = jax.nn.sigmoid(ba)
    xs = ba + prm_ref[pl.ds(1, 1), :]
    softplus = jnp.maximum(xs, 0.0) + jnp.log1p(jnp.exp(-jnp.abs(xs)))
    g = -jnp.exp(prm_ref[pl.ds(0, 1), :]) * softplus
    if valid is not None:
        beta = jnp.where(valid, beta, 0.0)
        g = jnp.where(valid, g, 0.0)
    return qs, ks, vs, beta, g


def _gated_out(o, z, gout):
    return _rms(o, gout) * _silu(z)


def _gate_lanes(ba, prm_ref):
    xs = ba + prm_ref[pl.ds(1, 1), :]
    softplus = jnp.maximum(xs, 0.0) + jnp.log1p(jnp.exp(-jnp.abs(xs)))
    return jax.nn.sigmoid(ba), -jnp.exp(prm_ref[pl.ds(0, 1), :]) * softplus


def _gdn_p_kernel(x_ref, halo_ref, ba_ref, wc_ref, prm_ref, gout_ref,
                  o_ref, s_ref, cst_ref,
                  q_scr, k_scr, v_scr, bg_scr, st_scr):
    j = pl.program_id(1)
    nj = pl.num_programs(1)
    slot = 0
    hs = range(NH)
    cols = [slice(h * DH, (h + 1) * DH) for h in hs]

    def features(src_ref, gate_ref, halo, dst):
        for c in range(QKV // DH):
            cc = slice(c * DH, (c + 1) * DH)
            x = src_ref[:, cc]
            xp = jnp.concatenate([halo(cc), x], axis=0)
            xc = x * wc_ref[0, pl.ds(GCONV - 1, 1), cc]
            for sft in range(1, GCONV):
                xc = xc + pltpu.roll(xp, sft, 0)[SUBLANES:, :] * wc_ref[0, pl.ds(GCONV - 1 - sft, 1), cc]
            a = _silu(xc)
            if c < NH:
                q_scr[dst, :, cc] = a * lax.rsqrt(jnp.sum(a * a, axis=-1, keepdims=True) + 1e-6) * (DH ** -0.5)
            elif c < 2 * NH:
                k_scr[dst, :, cols[c - NH]] = a * lax.rsqrt(jnp.sum(a * a, axis=-1, keepdims=True) + 1e-6)
            else:
                v_scr[dst, :, cols[c - 2 * NH]] = a
            yield
        beta, g = _gate_lanes(gate_ref[...], prm_ref.at[0])
        bg_scr[dst, 0] = beta
        bg_scr[dst, 1] = g
        yield

    @pl.when(j == 0)
    def _():
        st_scr[...] = jnp.zeros_like(st_scr)

    _run(features(x_ref, ba_ref, lambda cc: jnp.where(j > 0, halo_ref[:, cc], 0.0), slot))

    nsq = int(math.log2(SUBLANES)) - 1
    nblk = TILE // GCHUNK
    ntile = LB_G // TILE

    def chains():
        rows, qs, ks, vs, betas, gcs, gts, grows = [], [], [], [], [], [], [], []
        for ti in range(ntile):
            r = pl.ds(ti * TILE, TILE)
            rows.append(r)
            beta_t = bg_scr[slot, 0, r, :]
            gc_t, gt_t = _gate_cumsums(bg_scr[slot, 1, r, :], GCHUNK)
            gc_tt = gc_t.T
            for h in hs:
                qs.append(q_scr[slot, r, cols[h]])
                ks.append(k_scr[slot, r, cols[h]])
                vs.append(v_scr[slot, r, cols[h]])
                betas.append(beta_t[:, h:h + 1])
                gcs.append(gc_t[:, NH + h:NH + h + 1])
                gts.append(gt_t[:, NH + h:NH + h + 1])
                grows.append(gc_tt[NH + h:NH + h + 1, :])
        yield
        us, ws, q_decs, k_decs, qks = yield from _delta_prepare(qs, ks, vs, betas, gcs, grows, gts, GCHUNK, nsq)
        s_cur = [st_scr[h] for h in hs]
        for ti in range(ntile):
            ch = [ti * NH + h for h in hs]
            egts = [jnp.exp(gts[i]) for i in ch]
            vn = [[] for _ in hs]
            qs_ = [[] for _ in hs]
            for c in range(nblk):
                rr = slice(c * GCHUNK, (c + 1) * GCHUNK)
                wss = [_dot(jnp.concatenate([ws[ch[h]][rr], q_decs[ch[h]][rr]], axis=0), s_cur[h]) for h in hs]
                yield
                for h in hs:
                    vn[h].append(us[ch[h]][rr] - wss[h][:GCHUNK])
                    qs_[h].append(wss[h][GCHUNK:])
                s_cur = [s_cur[h] * egts[h][c * GCHUNK:c * GCHUNK + 1, :] + _dot_tn(k_decs[ch[h]][rr], vn[h][c])
                         for h in hs]
                yield
            os_ = [jnp.concatenate(qs_[h], axis=0) + _dot(qks[ch[h]], jnp.concatenate(vn[h], axis=0)) for h in hs]
            for h in hs:
                z = x_ref[rows[ti], pl.ds(QKV + h * DH, DH)]
                o_ref[rows[ti], cols[h]] = _gated_out(os_[h], z, gout_ref[0]).astype(o_ref.dtype)
            yield
        for h in hs:
            st_scr[h] = s_cur[h]

    _run(chains())

    @pl.when(j == nj - 1)
    def _():
        s_ref[0] = st_scr[...]
        cst_ref[0] = x_ref[pl.ds(LB_G - SUBLANES, SUBLANES), :QKV]


def _gdn_prompt(layer, proj, w_gconv, gprm, g_gout):
    nj = L_P // LB_G
    return pl.pallas_call(
        _gdn_p_kernel,
        grid=(NB_P, nj),
        in_specs=[
            pl.BlockSpec((LB_G, COL_GLU), lambda b, j: (b * nj + j, 0)),
            pl.BlockSpec((SUBLANES, COL_GLU),
                         lambda b, j: (jnp.maximum((b * nj + j) * (LB_G // SUBLANES) - 1, 0), 0)),
            pl.BlockSpec((LB_G, LANES), lambda b, j: (b * nj + j, COL_BA // LANES)),
            pl.BlockSpec((1, GCONV, QKV), lambda b, j: (layer, 0, 0)),
            pl.BlockSpec((1, SUBLANES, LANES), lambda b, j: (layer, 0, 0)),
            pl.BlockSpec((1, 1, DH), lambda b, j: (layer, 0, 0)),
        ],
        out_specs=[
            pl.BlockSpec((LB_G, GW), lambda b, j: (b * nj + j, 0)),
            pl.BlockSpec((1, NH, DH, DH), lambda b, j: (b, 0, 0, 0)),
            pl.BlockSpec((1, SUBLANES, QKV), lambda b, j: (b, 0, 0)),
        ],
        out_shape=[
            jax.ShapeDtypeStruct((T_P, GW), bf16),
            jax.ShapeDtypeStruct((NB_P, NH, DH, DH), f32),
            jax.ShapeDtypeStruct((NB_P, SUBLANES, QKV), f32),
        ],
        scratch_shapes=[pltpu.VMEM((1, LB_G, GW), f32), pltpu.VMEM((1, LB_G, GW), f32),
                        pltpu.VMEM((1, LB_G, GW), f32), pltpu.VMEM((1, 2, LB_G, LANES), f32),
                        pltpu.VMEM((NH, DH, DH), f32)],
        compiler_params=_cparams(("parallel", "arbitrary")),
        name=f"gdn_prompt_{layer}",
    )(proj, proj, proj, w_gconv, gprm, g_gout)


BT_S = TILE // TPAD


def _gdn_s_kernel(x_ref, ba_ref, cin_ref, s0_ref, wc_ref, prm_ref, gout_ref,
                  o_ref, s_ref, cst_ref, st_scr):
    x = x_ref[:, :QKV]
    st_scr[...] = jnp.zeros_like(st_scr)
    for b in range(BT_S):
        st_scr[pl.ds((b + 1) * TPAD - (GCONV - 1), GCONV - 1), :] = cin_ref[0, b]
        cst_ref[b] = x_ref[pl.ds(b * TPAD + L_S - (GCONV - 1), GCONV - 1), :QKV]
    st = st_scr[...]
    trow = lax.broadcasted_iota(jnp.int32, (TILE, 1), 0) % TPAD
    xc = x * wc_ref[0, pl.ds(GCONV - 1, 1), :]
    for s in range(1, GCONV):
        xs = jnp.where(trow >= s, pltpu.roll(x, s, 0), pltpu.roll(st, TILE - TPAD + s, 0))
        xc = xc + xs * wc_ref[0, pl.ds(GCONV - 1 - s, 1), :]
    valid = trow < L_S
    qs, ks, vs, beta, g = _gdn_features(xc, ba_ref[...], prm_ref.at[0], valid)

    nsq = int(math.ceil(math.log2(L_S))) - 1
    gc_t, gt_t = _gate_cumsums(g, TPAD)
    hs = range(NH)
    gcs = [gc_t[:, NH + h:NH + h + 1] for h in hs]
    gts = [gt_t[:, NH + h:NH + h + 1] for h in hs]
    gc_tt = gc_t.T
    grows = [gc_tt[NH + h:NH + h + 1, :] for h in hs]
    us, ws, q_decs, k_decs, qks = _run(_delta_prepare(qs, ks, vs, [beta[:, h:h + 1] for h in hs], gcs, grows,
                                                      gts, TPAD, nsq))
    for h in hs:
        egt = jnp.exp(gts[h])
        vn, qs_ = [], []
        for b in range(BT_S):
            rr = slice(b * TPAD, (b + 1) * TPAD)
            s0 = s0_ref[0, b, h]
            wsb = _dot(jnp.concatenate([ws[h][rr], q_decs[h][rr]], axis=0), s0)
            vn_b = us[h][rr] - wsb[:TPAD]
            vn.append(vn_b)
            qs_.append(wsb[TPAD:])
            s_ref[b, h] = s0 * egt[b * TPAD:b * TPAD + 1, :] + _dot_tn(k_decs[h][rr], vn_b)
        o = jnp.concatenate(qs_, axis=0) + _dot(qks[h], jnp.concatenate(vn, axis=0))
        z = x_ref[:, pl.ds(QKV + h * DH, DH)]
        o_ref[:, h * DH:(h + 1) * DH] = _gated_out(o, z, gout_ref[0]).astype(o_ref.dtype)


def _gdn_sample(layer, proj, conv_in, s0, w_gconv, gprm, g_gout):
    row0 = T_P // TILE
    return pl.pallas_call(
        _gdn_s_kernel,
        grid=(T_S // TILE,),
        in_specs=[
            pl.BlockSpec((TILE, COL_GLU), lambda i: (row0 + i, 0)),
            pl.BlockSpec((TILE, LANES), lambda i: (row0 + i, COL_BA // LANES)),
            pl.BlockSpec((1, BT_S, GCONV - 1, QKV), lambda i: (layer, i, 0, 0)),
            pl.BlockSpec((1, BT_S, NH, DH, DH), lambda i: (layer, i, 0, 0, 0)),
            pl.BlockSpec((1, GCONV, QKV), lambda i: (layer, 0, 0)),
            pl.BlockSpec((1, SUBLANES, LANES), lambda i: (layer, 0, 0)),
            pl.BlockSpec((1, 1, DH), lambda i: (layer, 0, 0)),
        ],
        out_specs=[
            pl.BlockSpec((TILE, GW), lambda i: (i, 0)),
            pl.BlockSpec((BT_S, NH, DH, DH), lambda i: (i, 0, 0, 0)),
            pl.BlockSpec((BT_S, GCONV - 1, QKV), lambda i: (i, 0, 0)),
        ],
        out_shape=[
            jax.ShapeDtypeStruct((T_S, GW), bf16),
            jax.ShapeDtypeStruct((NB_S, NH, DH, DH), f32),
            jax.ShapeDtypeStruct((NB_S, GCONV - 1, QKV), f32),
        ],
        scratch_shapes=[pltpu.VMEM((TILE, QKV), f32)],
        compiler_params=_cparams(("parallel",)),
        name=f"gdn_sample_{layer}",
    )(proj, proj, conv_in, s0, w_gconv, gprm, g_gout)


def _group_ln_silu(y, gl_ref, bl_ref):
    outs = []
    for gi in range(CGROUPS):
        w = CW // CGROUPS
        yg = y[:, gi * w:(gi + 1) * w]
        mu = jnp.mean(yg, axis=-1, keepdims=True)
        d = yg - mu
        var = jnp.mean(d * d, axis=-1, keepdims=True)
        outs.append(d * lax.rsqrt(var + LN_EPS))
    yn = jnp.concatenate(outs, axis=1) * gl_ref[0] + bl_ref[0]
    return _silu(yn)


def _conf_p_kernel(x_ref, halo_ref, w_ref, b_ref, gl_ref, bl_ref, o_ref, hst_ref, xp_scr, sh_scr):
    j = pl.program_id(1)
    nj = pl.num_programs(1)
    h = x_ref[...]
    xp_scr[pl.ds(0, HIST_C), :] = jnp.where(j > 0, halo_ref[...], 0.0)
    xp_scr[pl.ds(HIST_C, LB_C), :] = h
    n_sh = HIST_C + LB_C - SUBLANES
    for r in range(1, SUBLANES):
        sh_scr[r - 1] = xp_scr[pl.ds(r, n_sh), :]
    off = HIST_C - (CK - 1)
    y = None
    for t in range(CK):
        a, r = divmod(off + t, SUBLANES)
        rows = pl.ds(a * SUBLANES, LB_C)
        win = xp_scr[rows, :] if r == 0 else sh_scr[r - 1, rows, :]
        term = win * w_ref[0, pl.ds(t, 1), :]
        y = term if y is None else y + term
    y = y + b_ref[0]
    o_ref[...] = _group_ln_silu(y, gl_ref, bl_ref).astype(o_ref.dtype)

    @pl.when(j == nj - 1)
    def _():
        hst_ref[0] = h[LB_C - HIST_C:, :]


def _conf_prompt(layer, glu, w_dw, b_dw, g_ln, b_ln):
    nj = L_P // LB_C
    return pl.pallas_call(
        _conf_p_kernel,
        grid=(NB_P, nj),
        in_specs=[
            pl.BlockSpec((LB_C, CW), lambda b, j: (b * nj + j, 0)),
            pl.BlockSpec((HIST_C, CW),
                         lambda b, j: (jnp.maximum((b * nj + j) * (LB_C // HIST_C) - 1, 0), 0)),
            pl.BlockSpec((1, CK, CW), lambda b, j: (layer, 0, 0)),
            pl.BlockSpec((1, 1, CW), lambda b, j: (layer, 0, 0)),
            pl.BlockSpec((1, 1, CW), lambda b, j: (layer, 0, 0)),
            pl.BlockSpec((1, 1, CW), lambda b, j: (layer, 0, 0)),
        ],
        out_specs=[
            pl.BlockSpec((LB_C, CW), lambda b, j: (b * nj + j, 0)),
            pl.BlockSpec((1, HIST_C, CW), lambda b, j: (b, 0, 0)),
        ],
        out_shape=[
            jax.ShapeDtypeStruct((T_P, CW), bf16),
            jax.ShapeDtypeStruct((NB_P, HIST_C, CW), f32),
        ],
        scratch_shapes=[pltpu.VMEM((HIST_C + LB_C, CW), f32),
                        pltpu.VMEM((SUBLANES - 1, HIST_C + LB_C - SUBLANES, CW), f32)],
        compiler_params=_cparams(("parallel", "arbitrary")),
        name=f"conf_prompt_{layer}",
    )(glu, glu, w_dw, b_dw, g_ln, b_ln)


BT_C = 16


def _conf_s_kernel(x_ref, hin_ref, wh_ref, wn_ref, b_ref, gl_ref, bl_ref, o_ref, hst_ref, y_scr):
    h = x_ref[...]
    trow = lax.broadcasted_iota(jnp.int32, (TPAD, 1), 0)
    nh = CK - 1
    for b in range(BT_C):
        hist = hin_ref[0, b]
        h8 = h[b * TPAD:(b + 1) * TPAD, :]
        y8 = jnp.zeros((TPAD, CW), f32)
        for t in range(L_S):
            yt = (jnp.sum(hist * wh_ref[0, t], axis=0, keepdims=True)
                  + jnp.sum(h8 * wn_ref[0, t], axis=0, keepdims=True))
            y8 = jnp.where(trow == t, yt, y8)
        y_scr[pl.ds(b * TPAD, TPAD), :] = y8
        hst_ref[b, pl.ds(0, nh - L_S), :] = hin_ref[0, b, pl.ds(L_S, nh - L_S), :]
        hst_ref[b, pl.ds(nh - L_S, L_S), :] = h8[:L_S, :]
    y = y_scr[...] + b_ref[0]
    o_ref[...] = _group_ln_silu(y, gl_ref, bl_ref).astype(o_ref.dtype)


def _conf_sample(layer, glu, hist_in, w_hist, w_new, b_dw, g_ln, b_ln):
    row0 = T_P // (BT_C * TPAD)
    return pl.pallas_call(
        _conf_s_kernel,
        grid=(NB_S // BT_C,),
        in_specs=[
            pl.BlockSpec((BT_C * TPAD, CW), lambda i: (row0 + i, 0)),
            pl.BlockSpec((1, BT_C, CK - 1, CW), lambda i: (layer, i, 0, 0)),
            pl.BlockSpec((1, L_S, CK - 1, CW), lambda i: (layer, 0, 0, 0)),
            pl.BlockSpec((1, L_S, TPAD, CW), lambda i: (layer, 0, 0, 0)),
            pl.BlockSpec((1, 1, CW), lambda i: (layer, 0, 0)),
            pl.BlockSpec((1, 1, CW), lambda i: (layer, 0, 0)),
            pl.BlockSpec((1, 1, CW), lambda i: (layer, 0, 0)),
        ],
        out_specs=[
            pl.BlockSpec((BT_C * TPAD, CW), lambda i: (i, 0)),
            pl.BlockSpec((BT_C, CK - 1, CW), lambda i: (i, 0, 0)),
        ],
        out_shape=[
            jax.ShapeDtypeStruct((T_S, CW), bf16),
            jax.ShapeDtypeStruct((NB_S, CK - 1, CW), f32),
        ],
        scratch_shapes=[pltpu.VMEM((BT_C * TPAD, CW), f32)],
        compiler_params=_cparams(("parallel",)),
        name=f"conf_sample_{layer}",
    )(glu, hist_in, w_hist, w_new, b_dw, g_ln, b_ln)


def _out_kernel(moe, x_ref, oap, obp, oas, obs, w_ref, g_ref,
                g1p, scp, shp, g1s, scs, shs, *rest):
    if moe:
        wr_ref, br_ref, xo_ref, h2_ref, rt_ref = rest
        hf_scr = h2_ref
    else:
        xo_ref, h2_ref, hf_scr = rest
    i = pl.program_id(0)

    def attn(oa, ob):
        return (jnp.dot(oa[...], w_ref[0, :GW, :], preferred_element_type=f32)
                + jnp.dot(ob[...], w_ref[0, GW:, :], preferred_element_type=f32))

    @pl.when(i < NT_P)
    def _():
        xn = x_ref[...] + g1p[0] * attn(oap, obp)
        xo_ref[...] = xn
        hf_scr[...] = _rms(xn, g_ref[0]) * (1.0 + scp[0]) + shp[0]

    @pl.when(i >= NT_P)
    def _():
        hf_scr[...] = attn(oas, obs)

        def body(rows, b):
            m = pl.ds(b, 1)
            xn = x_ref[rows, :] + g1s[m, :] * hf_scr[rows, :]
            xo_ref[rows, :] = xn
            hf_scr[rows, :] = _rms(xn, g_ref[0]) * (1.0 + scs[m, :]) + shs[m, :]
        _sample_rows(body)

    hf = hf_scr[...]
    if not moe:
        h2_ref[...] = hf.astype(h2_ref.dtype)
    if moe:
        logits = _dot_hi(hf, wr_ref[0]) + br_ref[0]
        lane = lax.broadcasted_iota(jnp.int32, logits.shape, 1)
        ex = jnp.exp(logits - jnp.max(logits, axis=-1, keepdims=True))
        probs = ex / jnp.sum(ex, axis=-1, keepdims=True)
        m1 = jnp.max(probs, axis=-1, keepdims=True)
        i1 = jnp.min(jnp.where(probs == m1, lane, LANES), axis=-1, keepdims=True)
        rest_p = jnp.where(lane == i1, -1.0, probs)
        m2 = jnp.max(rest_p, axis=-1, keepdims=True)
        i2 = jnp.min(jnp.where(rest_p == m2, lane, LANES), axis=-1, keepdims=True)
        den = m1 + m2
        rt_ref[...] = jnp.where(lane == 0, i1.astype(f32),
                                jnp.where(lane == 1, i2.astype(f32),
                                          jnp.where(lane == 2, m1 / den,
                                                    jnp.where(lane == 3, m2 / den, 0.0))))


def _out_proj(layer, moe, x, oa_p, ob_p, oa_s, ob_s, w_out, g2, modp, mods, wr=None, br=None):
    idx = layer // 2
    in_specs = [
        pl.BlockSpec((TM, D), lambda i: (i, 0)),
        pl.BlockSpec((TM, GW), lambda i: (jnp.minimum(i, NT_P - 1), 0)),
        pl.BlockSpec((TM, CW), lambda i: (jnp.minimum(i, NT_P - 1), 0)),
        pl.BlockSpec((TM, GW), lambda i: (0, 0)),
        pl.BlockSpec((TM, CW), lambda i: (0, 0)),
        pl.BlockSpec((1, D, D), lambda i: (layer, 0, 0)),
        pl.BlockSpec((1, 1, D), lambda i: (layer, 0, 0)),
        _mod_specs(G1, 1)[0], _mod_specs(SC2, 1)[0], _mod_specs(SH2, 1)[0],
        _mod_specs(G1, 1)[1], _mod_specs(SC2, 1)[1], _mod_specs(SH2, 1)[1],
    ]
    args = [x, oa_p, ob_p, oa_s, ob_s, w_out, g2, modp, modp, modp, mods, mods, mods]
    out_specs = [pl.BlockSpec((TM, D), lambda i: (i, 0)), pl.BlockSpec((TM, D), lambda i: (i, 0))]
    out_shape = [jax.ShapeDtypeStruct((T_ALL, D), f32),
                 jax.ShapeDtypeStruct((T_ALL, D), f32 if moe else bf16)]
    if moe:
        in_specs += [pl.BlockSpec((1, D, LANES), lambda i: (idx, 0, 0)),
                     pl.BlockSpec((1, 1, LANES), lambda i: (idx, 0, 0))]
        args += [wr, br]
        out_specs.append(pl.BlockSpec((TM, LANES), lambda i: (i, 0)))
        out_shape.append(jax.ShapeDtypeStruct((T_ALL, LANES), f32))
    return pl.pallas_call(
        functools.partial(_out_kernel, moe),
        grid=(NT,),
        in_specs=in_specs,
        out_specs=out_specs,
        out_shape=out_shape,
        scratch_shapes=[] if moe else [pltpu.VMEM((TM, D), f32)],
        compiler_params=_cparams(("parallel",)),
        name=f"out_proj_{layer}",
    )(*args)


def _residual(i, x_ref, f_ref, g2p, g2s, o_ref, gf_ref, os_ref=None):
    os_ref = o_ref if os_ref is None else os_ref

    def fin(v):
        return v if gf_ref is None else _rms(v, gf_ref[...])

    @pl.when(i < NT_P)
    def _():
        o_ref[...] = fin(x_ref[...] + g2p[0] * f_ref[...])

    @pl.when(i >= NT_P)
    def _():
        def body(rows, b):
            os_ref[rows, :] = fin(x_ref[rows, :] + g2s[pl.ds(b, 1), :] * f_ref[rows, :])
        _sample_rows(body)


def _ffn_kernel(h_ref, wg_ref, wu_ref, wd_ref, x_ref, g2p, g2s, o_ref, acc_scr):
    i = pl.program_id(0)
    h = h_ref[...]
    for c in range(FF_D // TF_D):
        cols = slice(c * TF_D, (c + 1) * TF_D)
        gate = jnp.dot(h, wg_ref[0, :, cols], preferred_element_type=f32)
        up = jnp.dot(h, wu_ref[0, :, cols], preferred_element_type=f32)
        part = jnp.dot((_silu(gate) * up).astype(bf16), wd_ref[0, cols, :], preferred_element_type=f32)
        if c == 0:
            acc_scr[...] = part
        else:
            acc_scr[...] += part
    _residual(i, x_ref, acc_scr, g2p, g2s, o_ref, None)


def _ffn_dense(layer, h2, x, wg, wu, wd, modp, mods):
    idx = layer // 2
    mp, ms = _mod_specs(G2, 1)
    return pl.pallas_call(
        _ffn_kernel,
        grid=(NT,),
        in_specs=[
            pl.BlockSpec((TM, D), lambda i: (i, 0)),
            _resident((1, D, FF_D), lambda i: (idx, 0, 0)),
            _resident((1, D, FF_D), lambda i: (idx, 0, 0)),
            _resident((1, FF_D, D), lambda i: (idx, 0, 0)),
            pl.BlockSpec((TM, D), lambda i: (i, 0)),
            mp, ms,
        ],
        out_specs=pl.BlockSpec((TM, D), lambda i: (i, 0)),
        out_shape=jax.ShapeDtypeStruct((T_ALL, D), f32),
        scratch_shapes=[pltpu.VMEM((TM, D), f32)],
        compiler_params=_cparams(("parallel",)),
        name=f"ffn_dense_{layer}",
    )(h2, wg, wu, wd, x, modp, mods)


def _gather_rows(idx_ref, base, stride, src_hbm, dst, sem):
    def issue(r, carry):
        row = idx_ref[base + stride * r]
        if len(src_hbm.shape) == 3:
            pltpu.make_async_copy(src_hbm.at[row], dst.at[r], sem).start()
        else:
            pltpu.make_async_copy(src_hbm.at[pl.ds(row, 1), :], dst.at[pl.ds(r, 1), :], sem).start()
        return carry
    lax.fori_loop(0, MOE_BLK, issue, 0, unroll=8)


def _wait_rows(src_hbm, dst, sem):
    pltpu.make_async_copy(src_hbm.at[pl.ds(0, MOE_BLK)], dst, sem).wait()


def _issue_rows(idx_ref, base, src_hbm, dst, sem, r0, n):
    for u in range(n):
        row = idx_ref[base + r0 + u]
        pltpu.make_async_copy(src_hbm.at[pl.ds(row, 1), :], dst.at[pl.ds(r0 + u, 1), :], sem).start()


def _expert_kernel(be_ref, tok_ref, nv_ref, nr_ref, h_hbm, wg_ref, wu_ref, wd_ref,
                   y_ref, xg_scr, xb_scr, acc_scr, sem):
    m = pl.program_id(0)
    f = pl.program_id(1)
    nv = nv_ref[0]
    active = m < nv
    slot = m % 2
    nxt = (m + 1) * MOE_BLK

    @pl.when(jnp.logical_and(f == 0, m == 0))
    def _():
        _gather_rows(tok_ref, 0, 1, h_hbm, xg_scr.at[0], sem.at[0])
        acc_scr[...] = jnp.zeros_like(acc_scr)

    @pl.when(jnp.logical_and(f == 0, m <= nv))
    def _():
        _wait_rows(h_hbm, xg_scr.at[slot], sem.at[slot])

    @pl.when(jnp.logical_and(f == 0, active))
    def _():
        xb_scr[...] = xg_scr[slot].astype(bf16)
        _issue_rows(tok_ref, nxt, h_hbm, xg_scr.at[1 - slot], sem.at[1 - slot],
                    GATHER_CHUNK * (FF_E // TF_E), GATHER_TAIL)

    @pl.when(jnp.logical_and(f == 0, jnp.logical_not(active)))
    def _():
        y_ref[...] = jnp.zeros_like(y_ref)

    short = nr_ref[m] <= MOE_BLK // 2

    @pl.when(jnp.logical_and(f == 0, jnp.logical_and(active, short)))
    def _():
        acc_scr[pl.ds(MOE_BLK // 2, MOE_BLK // 2), :] = jnp.zeros((MOE_BLK // 2, D), f32)

    def ffn_step(rows):
        _issue_rows(tok_ref, nxt, h_hbm, xg_scr.at[1 - slot], sem.at[1 - slot], f * GATHER_CHUNK, GATHER_CHUNK)
        xb = xb_scr[pl.ds(0, rows), :]
        a = _silu(_dot(xb, wg_ref[0, 0])) * _dot(xb, wu_ref[0, 0])
        part = _dot(a, wd_ref[0, 0])
        acc_scr[pl.ds(0, rows), :] = jnp.where(f == 0, part, acc_scr[pl.ds(0, rows), :] + part)

    @pl.when(jnp.logical_and(active, jnp.logical_not(short)))
    def _():
        ffn_step(MOE_BLK)

    @pl.when(jnp.logical_and(active, short))
    def _():
        ffn_step(MOE_BLK // 2)

    @pl.when(jnp.logical_and(f == pl.num_programs(1) - 1, active))
    def _():
        y_ref[...] = acc_scr[...].reshape(MOE_BLK, D // LANES, LANES)


def _experts(idx, h2, block_e, slot_tok, n_valid, n_rows, wg, wu, wd):
    nf = FF_E // TF_E

    def wmap_up(m, f, be, tok, nv, nr):
        return (idx, be[m], 0, jnp.where(m < nv[0], f, nf - 1))

    def wmap_down(m, f, be, tok, nv, nr):
        return (idx, be[m], jnp.where(m < nv[0], f, nf - 1), 0)

    grid_spec = pltpu.PrefetchScalarGridSpec(
        num_scalar_prefetch=4,
        grid=(N_BLK + 1, nf),
        in_specs=[
            pl.BlockSpec(memory_space=pl.ANY),
            pl.BlockSpec((1, 1, D, TF_E), wmap_up),
            pl.BlockSpec((1, 1, D, TF_E), wmap_up),
            pl.BlockSpec((1, 1, TF_E, D), wmap_down),
        ],
        out_specs=pl.BlockSpec((MOE_BLK, D // LANES, LANES), lambda m, f, be, tok, nv, nr: (m, 0, 0)),
        scratch_shapes=[pltpu.VMEM((2, MOE_BLK, D), f32), pltpu.VMEM((MOE_BLK, D), bf16),
                        pltpu.VMEM((MOE_BLK, D), f32), pltpu.SemaphoreType.DMA((2,))],
    )
    return pl.pallas_call(
        _expert_kernel,
        grid_spec=grid_spec,
        out_shape=jax.ShapeDtypeStruct(((N_BLK + 1) * MOE_BLK, D // LANES, LANES), f32),
        compiler_params=_cparams(("arbitrary", "arbitrary")),
        name=f"experts_{idx}",
    )(block_e, slot_tok, n_valid, n_rows, h2, wg, wu, wd)


def _combine_kernel(final, dest_ref, y_hbm, x_ref, rt_ref, g2p, g2s, *rest):
    if final:
        gf_ref, o_ref, os_ref, g_scr, f_scr, sem = rest
    else:
        o_ref, g_scr, f_scr, sem = rest
        gf_ref = os_ref = None
    i = pl.program_id(0)
    slot = i % 2

    def gather(tile, sl):
        for k in range(2):
            _gather_rows(dest_ref, 2 * tile * TM + k, 2, y_hbm, g_scr.at[sl, k], sem.at[sl, k])

    @pl.when(i == 0)
    def _():
        gather(0, 0)

    @pl.when(i + 1 < pl.num_programs(0))
    def _():
        gather(i + 1, 1 - slot)

    for k in range(2):
        _wait_rows(y_hbm, g_scr.at[slot, k], sem.at[slot, k])
    f_scr[...] = (rt_ref[:, 2:3] * g_scr[slot, 0].reshape(TM, D) + rt_ref[:, 3:4] * g_scr[slot, 1].reshape(TM, D))
    _residual(i, x_ref, f_scr, g2p, g2s, o_ref, gf_ref, os_ref)


def _combine(final, dest, yb, x, rt, modp, mods, g_final):
    mp, ms = _mod_specs(G2, 1)
    wrap = lambda spec: pl.BlockSpec(spec.block_shape, lambda i, d, _f=spec.index_map: _f(i))
    in_specs = [pl.BlockSpec(memory_space=pl.ANY),
                pl.BlockSpec((TM, D), lambda i, d: (i, 0)),
                pl.BlockSpec((TM, LANES), lambda i, d: (i, 0)), wrap(mp), wrap(ms)]
    args = [yb, x, rt, modp, mods]
    if final:
        in_specs.append(pl.BlockSpec((1, D), lambda i, d: (0, 0)))
        args.append(g_final)
        out_specs = [pl.BlockSpec((TM, D), lambda i, d: (jnp.minimum(i, NT_P - 1), 0)),
                     pl.BlockSpec((T_S, D), lambda i, d: (0, 0))]
        out_shape = [jax.ShapeDtypeStruct((T_P, D), f32), jax.ShapeDtypeStruct((T_S, D), f32)]
    else:
        out_specs = pl.BlockSpec((TM, D), lambda i, d: (i, 0))
        out_shape = jax.ShapeDtypeStruct((T_ALL, D), f32)
    grid_spec = pltpu.PrefetchScalarGridSpec(
        num_scalar_prefetch=1,
        grid=(NT,),
        in_specs=in_specs,
        out_specs=out_specs,
        scratch_shapes=[pltpu.VMEM((2, 2, TM, D // LANES, LANES), f32), pltpu.VMEM((TM, D), f32),
                        pltpu.SemaphoreType.DMA((2, 2))],
    )
    return pl.pallas_call(
        functools.partial(_combine_kernel, final),
        grid_spec=grid_spec,
        out_shape=out_shape,
        compiler_params=_cparams(("arbitrary",)),
        name="combine_final" if final else "combine",
    )(dest, *args)


def _route(rt):
    e = rt[:, :2].astype(jnp.int32).reshape(N_ASG)
    tok = jnp.arange(N_ASG, dtype=jnp.int32) // 2
    real = jnp.logical_or(tok < T_P, (tok - T_P) % TPAD < L_S)
    onehot = jnp.logical_and(e[:, None] == jnp.arange(NE, dtype=jnp.int32)[None, :], real[:, None]).astype(jnp.int32)
    csum = jnp.cumsum(onehot, axis=0)
    counts = csum[-1]
    rank = jnp.sum(onehot * (csum - 1), axis=1)
    padded = (counts + MOE_BLK - 1) // MOE_BLK * MOE_BLK
    pad_end = jnp.cumsum(padded)
    pad_start = pad_end - padded
    dest = (jnp.sum(onehot * pad_start[None, :], axis=1) + rank).astype(jnp.int32)
    dest = jnp.where(real, dest, jnp.arange(N_ASG, dtype=jnp.int32) % MOE_BLK)
    n_slots = (N_BLK + 1) * MOE_BLK
    slot_tok = jnp.zeros((n_slots,), jnp.int32).at[jnp.where(real, dest, n_slots)].set(
        tok, mode="drop", unique_indices=True)
    n_valid = (pad_end[-1] // MOE_BLK).astype(jnp.int32)
    blk = jnp.minimum(jnp.arange(N_BLK + 1, dtype=jnp.int32), n_valid - 1)
    first_slot = (blk * MOE_BLK)[:, None]
    block_e = jnp.minimum(jnp.sum((pad_end[None, :] <= first_slot).astype(jnp.int32), axis=1), NE - 1)
    n_rows = jnp.clip(counts[block_e] - (first_slot[:, 0] - pad_start[block_e]), 0, MOE_BLK).astype(jnp.int32)
    return dest, slot_tok, block_e, n_valid.reshape(1), n_rows


def kernel(x_prompt, x_sample, c_prompt, c_sample, state_gdn, state_gdn_conv, state_conf_conv, w_ada, b_ada, g_norm1, g_norm2, w_in, w_gdn_conv, a_log, dt_bias, g_gdn_out, w_conf_dw, b_conf_dw, g_conf_ln, b_conf_ln, w_out, w_ff_gate, w_ff_up, w_ff_down, w_router, b_router, w_exp_gate, w_exp_up, w_exp_down, g_final):
    xs_pad = jnp.pad(x_sample, ((0, 0), (0, TPAD - L_S), (0, 0)))
    x = jnp.concatenate([x_prompt.reshape(T_P, D), xs_pad.reshape(T_S, D)], axis=0)
    c_all = jnp.concatenate([c_prompt, c_sample], axis=0)
    o1, o2, o4 = QKV + GW, QKV + GW + 2 * NH, QKV + GW + 2 * NH + 2 * CW
    w_cat = jnp.concatenate([w_in[:, :, :o2], jnp.zeros((DEPTH, D, W_CAT - o4), f32), w_in[:, :, o2:o4]],
                            axis=-1).astype(bf16)
    lane_pad = ((0, 0), (NH, LANES - 2 * NH))
    gprm = jnp.stack([jnp.pad(a_log, lane_pad), jnp.pad(dt_bias, lane_pad)], axis=1)
    gprm = jnp.pad(gprm, ((0, 0), (0, SUBLANES - 2), (0, 0)))
    wr_pad = jnp.pad(w_router, ((0, 0), (0, 0), (0, LANES - NE)))
    br_pad = jnp.pad(b_router, ((0, 0), (0, LANES - NE)), constant_values=-1e30).reshape(-1, 1, LANES)
    w_shift = jnp.stack([jnp.pad(w_conf_dw, ((0, 0), (t, TPAD - 1 - t), (0, 0))) for t in range(L_S)], axis=1)
    w_hist, w_new = w_shift[:, :, :CK - 1], w_shift[:, :, CK - 1:]
    w_out = w_out.astype(bf16)
    w_ff_gate, w_ff_up, w_ff_down = (w.astype(bf16) for w in (w_ff_gate, w_ff_up, w_ff_down))
    g1 = g_norm1.reshape(DEPTH, 1, D)
    g2 = g_norm2.reshape(DEPTH, 1, D)
    gout = g_gdn_out.reshape(DEPTH, 1, DH)
    b_dw = b_conf_dw.reshape(DEPTH, 1, CW)
    g_ln = g_conf_ln.reshape(DEPTH, 1, CW)
    b_ln = b_conf_ln.reshape(DEPTH, 1, CW)

    mod = _ada(c_all, w_ada, b_ada)

    sp_l, cp_l, fp_l, ss_l, cs_l, fs_l = [], [], [], [], [], []
    for layer in range(DEPTH):
        modp = mod[layer, :NB_P].reshape(NB_P, 1, N_MOD * D)
        mods = mod[layer, NB_P:]
        proj, glu = _in_proj(layer, x, g1, modp, mods, w_cat)
        oa_p, s_p, c_p = _gdn_prompt(layer, proj, w_gdn_conv, gprm, gout)
        oa_s, s_s, c_s = _gdn_sample(layer, proj, state_gdn_conv, state_gdn, w_gdn_conv, gprm, gout)
        ss_l.append(s_s)
        ob_p, f_p = _conf_prompt(layer, glu, w_conf_dw, b_dw, g_ln, b_ln)
        ob_s, f_s = _conf_sample(layer, glu, state_conf_conv, w_hist, w_new, b_dw, g_ln, b_ln)
        sp_l.append(s_p)
        cp_l.append(c_p[:, TPAD - (GCONV - 1):, :])
        fp_l.append(f_p[:, HIST_C - (CK - 1):, :])
        cs_l.append(c_s)
        fs_l.append(f_s)
        if layer % 2 == 0:
            x, h2 = _out_proj(layer, False, x, oa_p, ob_p, oa_s, ob_s, w_out, g2, modp, mods)
            x = _ffn_dense(layer, h2, x, w_ff_gate, w_ff_up, w_ff_down, modp, mods)
        else:
            x, h2, rt = _out_proj(layer, True, x, oa_p, ob_p, oa_s, ob_s, w_out, g2, modp, mods, wr_pad, br_pad)
            dest, slot_tok, block_e, n_valid, n_rows = _route(rt)
            yb = _experts(layer // 2, h2, block_e, slot_tok, n_valid, n_rows, w_exp_gate, w_exp_up, w_exp_down)
            final = layer == DEPTH - 1
            x = _combine(final, dest, yb, x, rt, modp, mods, g_final.reshape(1, D))

    y_p, y_s = x
    y_prompt = y_p.reshape(NB_P, L_P, D)
    y_sample = y_s.reshape(NB_S, TPAD, D)[:, :L_S, :]
    return (y_prompt, y_sample, jnp.stack(sp_l), jnp.stack(cp_l), jnp.stack(fp_l),
            jnp.stack(ss_l), jnp.stack(cs_l), jnp.stack(fs_l))
```

```python
import functools
import math

import jax
import jax.numpy as jnp
from jax import lax
from jax.experimental import pallas as pl
from jax.experimental.pallas import tpu as pltpu

f32 = jnp.float32
bf16 = jnp.bfloat16
HIGHEST = lax.Precision.HIGHEST

D = 1024
NB_P, L_P = 8, 2048
NB_S, L_S = 128, 4
DEPTH = 4
NH, DH = 4, 128
GW = NH * DH
QKV = 3 * GW
GCONV = 4
GCHUNK = 64
CW = D - GW
CGROUPS = 4
CK = 31
FF_D = 2816
NE = 8
FF_E = 3584
N_MOD = 6
RMS_EPS = 1e-6
LN_EPS = 1e-5

SUBLANES = 8
LANES = 128
VMEM_LIMIT = 56 * 1024 * 1024

TPAD = SUBLANES
T_P = NB_P * L_P
T_S = NB_S * TPAD
T_ALL = T_P + T_S
TM = 1024
NT = T_ALL // TM
NT_P = T_P // TM
TILES_PER_SEQ = L_P // TM
W_CAT = 3200
PROJ_W = 2176
COL_GLU = 2048
COL_BA = 2048
PROJ_CHUNKS = ((0, 768), (768, 768), (1536, 640))
HIST_C = 32
SH1, SC1, G1, SH2, SC2, G2 = range(N_MOD)

LB_G = 512
LB_C = 512
TILE = 128
TF_D = 1408
TF_E = 512
MOE_BLK = 1024
N_ASG = 2 * T_ALL
N_BLK = N_ASG // MOE_BLK + NE
GATHER_CHUNK = MOE_BLK // (FF_E // TF_E)
GATHER_TAIL = MOE_BLK - GATHER_CHUNK * (FF_E // TF_E)


def _cparams(sem):
    return pltpu.CompilerParams(dimension_semantics=sem, vmem_limit_bytes=VMEM_LIMIT)


def _dot(a, b):
    return jnp.dot(a.astype(bf16), b.astype(bf16), preferred_element_type=f32)


def _dot_hi(a, b):
    return jnp.dot(a, b, precision=HIGHEST, preferred_element_type=f32)


def _dot_nt(a, b):
    return lax.dot_general(a.astype(bf16), b.astype(bf16), (((1,), (1,)), ((), ())),
                           preferred_element_type=f32)


def _dot_nt_hi(a, b):
    return lax.dot_general(a, b, (((1,), (1,)), ((), ())), precision=HIGHEST,
                           preferred_element_type=f32)


def _dot_tn(a, b):
    return lax.dot_general(a, b, (((0,), (0,)), ((), ())), preferred_element_type=f32)


def _silu(x):
    return x * jax.nn.sigmoid(x)


def _rms(x, g):
    return x * lax.rsqrt(jnp.mean(x * x, axis=-1, keepdims=True) + RMS_EPS) * g


def _ada_kernel(c_ref, w_ref, b_ref, o_ref):
    a = _silu(c_ref[...])
    o_ref[0] = _dot(a, w_ref[0]) + b_ref[0]


def _ada(c_all, w_ada, b_ada):
    n = c_all.shape[0]
    tn = 1536
    return pl.pallas_call(
        _ada_kernel,
        grid=(DEPTH, N_MOD * D // tn),
        in_specs=[
            pl.BlockSpec((n, D), lambda l, j: (0, 0)),
            pl.BlockSpec((1, D, tn), lambda l, j: (l, 0, j)),
            pl.BlockSpec((1, 1, tn), lambda l, j: (l, 0, j)),
        ],
        out_specs=pl.BlockSpec((1, n, tn), lambda l, j: (l, 0, j)),
        out_shape=jax.ShapeDtypeStruct((DEPTH, n, N_MOD * D), f32),
        compiler_params=_cparams(("parallel", "parallel")),
        name="ada",
    )(c_all, w_ada, b_ada.reshape(DEPTH, 1, N_MOD * D))


def _mod_specs(k, ngrid):
    if ngrid == 1:
        return [pl.BlockSpec((1, 1, D), lambda i: (jnp.minimum(i // TILES_PER_SEQ, NB_P - 1), 0, k)),
                pl.BlockSpec((NB_S, D), lambda i: (0, k))]
    return [pl.BlockSpec((1, 1, D), lambda i, j: (jnp.minimum(i // TILES_PER_SEQ, NB_P - 1), 0, k)),
            pl.BlockSpec((NB_S, D), lambda i, j: (0, k))]


def _sample_rows(body):
    def step(b, carry):
        body(pl.ds(pl.multiple_of(b * TPAD, TPAD), TPAD), b)
        return carry
    lax.fori_loop(0, NB_S, step, 0)


def _in_kernel(x_ref, g_ref, shp, scp, shs, scs, w_ref, o_ref, glu_ref, h_scr, hf_scr):
    i = pl.program_id(0)

    @pl.when(i < NT_P)
    def _():
        h = _rms(x_ref[...], g_ref[0]) * (1.0 + scp[0]) + shp[0]
        h_scr[...] = h.astype(bf16)

    @pl.when(i >= NT_P)
    def _():
        def body(rows, b):
            hf_scr[rows, :] = (_rms(x_ref[rows, :], g_ref[0]) * (1.0 + scs[pl.ds(b, 1), :])
                               + shs[pl.ds(b, 1), :])
        _sample_rows(body)
        h_scr[...] = hf_scr[...].astype(bf16)

    for c0, cw in PROJ_CHUNKS:
        cols = slice(c0, c0 + cw)
        o_ref[:, cols] = jnp.dot(h_scr[...], w_ref[0, :, cols], preferred_element_type=f32)
    u = jnp.dot(h_scr[...], w_ref[0, :, PROJ_W:PROJ_W + CW], preferred_element_type=f32)
    gate = jnp.dot(h_scr[...], w_ref[0, :, PROJ_W + CW:], preferred_element_type=f32)
    glu_ref[...] = u * jax.nn.sigmoid(gate)


def _resident(block_shape, index_map):
    return pl.BlockSpec(block_shape, index_map, pipeline_mode=pl.Buffered(1))


def _in_proj(layer, x, g1, modp, mods, w_cat):
    return pl.pallas_call(
        _in_kernel,
        grid=(NT,),
        in_specs=[
            pl.BlockSpec((TM, D), lambda i: (i, 0)),
            pl.BlockSpec((1, 1, D), lambda i: (layer, 0, 0)),
            *_mod_specs(SH1, 1)[:1], *_mod_specs(SC1, 1)[:1],
            *_mod_specs(SH1, 1)[1:], *_mod_specs(SC1, 1)[1:],
            _resident((1, D, W_CAT), lambda i: (layer, 0, 0)),
        ],
        out_specs=[pl.BlockSpec((TM, PROJ_W), lambda i: (i, 0)), pl.BlockSpec((TM, CW), lambda i: (i, 0))],
        out_shape=[jax.ShapeDtypeStruct((T_ALL, PROJ_W), f32), jax.ShapeDtypeStruct((T_ALL, CW), f32)],
        scratch_shapes=[pltpu.VMEM((TM, D), bf16), pltpu.VMEM((TM, D), f32)],
        compiler_params=_cparams(("parallel",)),
        name=f"in_proj_{layer}",
    )(x, g1, modp, modp, mods, mods, w_cat)


def _tile_masks(blk):
    ri = lax.broadcasted_iota(jnp.int32, (TILE, TILE), 0)
    ci = lax.broadcasted_iota(jnp.int32, (TILE, TILE), 1)
    same = (ri // blk) == (ci // blk)
    incl = jnp.logical_and(same, ri >= ci)
    strict = jnp.logical_and(same, ri > ci)
    return same, incl, strict


def _gate_cumsums(g_all, blk):
    same, incl, _ = _tile_masks(blk)
    gc = _dot_hi(incl.astype(f32), g_all)
    if TILE == 2 * blk:
        row = lax.broadcasted_iota(jnp.int32, (TILE, 1), 0)
        gt = jnp.where(row < blk, gc[blk - 1:blk, :], gc[TILE - 1:TILE, :])
    else:
        gt = _dot_hi(same.astype(f32), g_all)
    return gc, gt


def _run(gen):
    try:
        while True:
            next(gen)
    except StopIteration as stop:
        return stop.value


def _unit_lower_inverse(a_mats, blk, base, nsq):
    ri = lax.broadcasted_iota(jnp.int32, (TILE, TILE), 0)
    ci = lax.broadcasted_iota(jnp.int32, (TILE, TILE), 1)
    ps = [jnp.where((ri // base) == (ci // base), -a, 0.0) for a in a_mats]
    eye = (ri == ci).astype(f32)
    ts = [eye + p for p in ps]
    for _ in range(nsq):
        ps = [_dot(p, p) for p in ps]
        yield
        ts = [t + _dot(p, t) for p, t in zip(ps, ts)]
        yield
    b = base
    while b < blk:
        off = jnp.logical_and((ri // (2 * b)) == (ci // (2 * b)), (ri // b) != (ci // b))
        ms = [_dot(jnp.where(off, a, 0.0), t) for a, t in zip(a_mats, ts)]
        yield
        ts = [t - _dot(t, m) for t, m in zip(ts, ms)]
        yield
        b *= 2
    return ts


def _delta_prepare(qs, ks, vs, betas, gcs, grows, gts, blk, nsq):
    _, incl, strict = _tile_masks(blk)
    n = range(len(qs))
    dmats = [gc - gr for gc, gr in zip(gcs, grows)]
    decays = [jnp.where(incl, jnp.exp(jnp.where(incl, d, 0.0)), 0.0) for d in dmats]
    egcs = [jnp.exp(gc) for gc in gcs]
    kbs = [ks[i] * betas[i] for i in n]
    a_mats = [jnp.where(strict, _dot_nt(kbs[i], ks[i]) * decays[i], 0.0) for i in n]
    yield
    qks = [_dot_nt(qs[i], ks[i]) * decays[i] for i in n]
    yield
    xs = [jnp.concatenate([vs[i] * betas[i], kbs[i] * egcs[i]], axis=1) for i in n]
    tinvs = yield from _unit_lower_inverse(a_mats, blk, SUBLANES, nsq)
    xs = [_dot(t, x) for t, x in zip(tinvs, xs)]
    yield
    us = [x[:, :DH] for x in xs]
    ws = [x[:, DH:] for x in xs]
    q_decs = [qs[i] * egcs[i] for i in n]
    k_decs = [ks[i] * jnp.exp(gts[i] - gcs[i]) for i in n]
    return us, ws, q_decs, k_decs, qks


def _gdn_features(xc, ba, prm_ref, valid):
    s = _silu(xc)
    if valid is not None:
        s = jnp.where(valid, s, 0.0)
    qs, ks, vs = [], [], []
    for h in range(NH):
        qh = s[:, h * DH:(h + 1) * DH]
        kh = s[:, GW + h * DH:GW + (h + 1) * DH]
        qs.append(qh * lax.rsqrt(jnp.sum(qh * qh, axis=-1, keepdims=True) + 1e-6) * (DH ** -0.5))
        ks.append(kh * lax.rsqrt(jnp.sum(kh * kh, axis=-1, keepdims=True) + 1e-6))
        vs.append(s[:, 2 * GW + h * DH:2 * GW + (h + 1) * DH])
    beta = jax.nn.sigmoid(ba)
    xs = ba + prm_ref[pl.ds(1, 1), :]
    softplus = jnp.maximum(xs, 0.0) + jnp.log1p(jnp.exp(-jnp.abs(xs)))
    g = -jnp.exp(prm_ref[pl.ds(0, 1), :]) * softplus
    if valid is not None:
        beta = jnp.where(valid, beta, 0.0)
        g = jnp.where(valid, g, 0.0)
    return qs, ks, vs, beta, g


def _gated_out(o, z, gout):
    return _rms(o, gout) * _silu(z)


def _gate_lanes(ba, prm_ref):
    xs = ba + prm_ref[pl.ds(1, 1), :]
    softplus = jnp.maximum(xs, 0.0) + jnp.log1p(jnp.exp(-jnp.abs(xs)))
    return jax.nn.sigmoid(ba), -jnp.exp(prm_ref[pl.ds(0, 1), :]) * softplus


def _gdn_p_kernel(x_ref, halo_ref, ba_ref, wc_ref, prm_ref, gout_ref,
                  o_ref, s_ref, cst_ref,
                  q_scr, k_scr, v_scr, bg_scr, st_scr):
    j = pl.program_id(1)
    nj = pl.num_programs(1)
    slot = 0
    hs = range(NH)
    cols = [slice(h * DH, (h + 1) * DH) for h in hs]

    def features(src_ref, gate_ref, halo, dst):
        for c in range(QKV // DH):
            cc = slice(c * DH, (c + 1) * DH)
            x = src_ref[:, cc]
            xp = jnp.concatenate([halo(cc), x], axis=0)
            xc = x * wc_ref[0, pl.ds(GCONV - 1, 1), cc]
            for sft in range(1, GCONV):
                xc = xc + pltpu.roll(xp, sft, 0)[SUBLANES:, :] * wc_ref[0, pl.ds(GCONV - 1 - sft, 1), cc]
            a = _silu(xc)
            if c < NH:
                q_scr[dst, :, cc] = a * lax.rsqrt(jnp.sum(a * a, axis=-1, keepdims=True) + 1e-6) * (DH ** -0.5)
            elif c < 2 * NH:
                k_scr[dst, :, cols[c - NH]] = a * lax.rsqrt(jnp.sum(a * a, axis=-1, keepdims=True) + 1e-6)
            else:
                v_scr[dst, :, cols[c - 2 * NH]] = a
            yield
        beta, g = _gate_lanes(gate_ref[...], prm_ref.at[0])
        bg_scr[dst, 0] = beta
        bg_scr[dst, 1] = g
        yield

    @pl.when(j == 0)
    def _():
        st_scr[...] = jnp.zeros_like(st_scr)

    _run(features(x_ref, ba_ref, lambda cc: jnp.where(j > 0, halo_ref[:, cc], 0.0), slot))

    nsq = int(math.log2(SUBLANES)) - 1
    nblk = TILE // GCHUNK
    ntile = LB_G // TILE

    def chains():
        rows, qs, ks, vs, betas, gcs, gts, grows = [], [], [], [], [], [], [], []
        for ti in range(ntile):
            r = pl.ds(ti * TILE, TILE)
            rows.append(r)
            beta_t = bg_scr[slot, 0, r, :]
            gc_t, gt_t = _gate_cumsums(bg_scr[slot, 1, r, :], GCHUNK)
            gc_tt = gc_t.T
            for h in hs:
                qs.append(q_scr[slot, r, cols[h]])
                ks.append(k_scr[slot, r, cols[h]])
                vs.append(v_scr[slot, r, cols[h]])
                betas.append(beta_t[:, h:h + 1])
                gcs.append(gc_t[:, NH + h:NH + h + 1])
                gts.append(gt_t[:, NH + h:NH + h + 1])
                grows.append(gc_tt[NH + h:NH + h + 1, :])
        yield
        us, ws, q_decs, k_decs, qks = yield from _delta_prepare(qs, ks, vs, betas, gcs, grows, gts, GCHUNK, nsq)
        s_cur = [st_scr[h] for h in hs]
        for ti in range(ntile):
            ch = [ti * NH + h for h in hs]
            egts = [jnp.exp(gts[i]) for i in ch]
            vn = [[] for _ in hs]
            qs_ = [[] for _ in hs]
            for c in range(nblk):
                rr = slice(c * GCHUNK, (c + 1) * GCHUNK)
                wss = [_dot(jnp.concatenate([ws[ch[h]][rr], q_decs[ch[h]][rr]], axis=0), s_cur[h]) for h in hs]
                yield
                for h in hs:
                    vn[h].append(us[ch[h]][rr] - wss[h][:GCHUNK])
                    qs_[h].append(wss[h][GCHUNK:])
                s_cur = [s_cur[h] * egts[h][c * GCHUNK:c * GCHUNK + 1, :] + _dot_tn(k_decs[ch[h]][rr], vn[h][c])
                         for h in hs]
                yield
            os_ = [jnp.concatenate(qs_[h], axis=0) + _dot(qks[ch[h]], jnp.concatenate(vn[h], axis=0)) for h in hs]
            for h in hs:
                z = x_ref[rows[ti], pl.ds(QKV + h * DH, DH)]
                o_ref[rows[ti], cols[h]] = _gated_out(os_[h], z, gout_ref[0]).astype(o_ref.dtype)
            yield
        for h in hs:
            st_scr[h] = s_cur[h]

    _run(chains())

    @pl.when(j == nj - 1)
    def _():
        s_ref[0] = st_scr[...]
        cst_ref[0] = x_ref[pl.ds(LB_G - SUBLANES, SUBLANES), :QKV]


def _gdn_prompt(layer, proj, w_gconv, gprm, g_gout):
    nj = L_P // LB_G
    return pl.pallas_call(
        _gdn_p_kernel,
        grid=(NB_P, nj),
        in_specs=[
            pl.BlockSpec((LB_G, COL_GLU), lambda b, j: (b * nj + j, 0)),
            pl.BlockSpec((SUBLANES, COL_GLU),
                         lambda b, j: (jnp.maximum((b * nj + j) * (LB_G // SUBLANES) - 1, 0), 0)),
            pl.BlockSpec((LB_G, LANES), lambda b, j: (b * nj + j, COL_BA // LANES)),
            pl.BlockSpec((1, GCONV, QKV), lambda b, j: (layer, 0, 0)),
            pl.BlockSpec((1, SUBLANES, LANES), lambda b, j: (layer, 0, 0)),
            pl.BlockSpec((1, 1, DH), lambda b, j: (layer, 0, 0)),
        ],
        out_specs=[
            pl.BlockSpec((LB_G, GW), lambda b, j: (b * nj + j, 0)),
            pl.BlockSpec((1, NH, DH, DH), lambda b, j: (b, 0, 0, 0)),
            pl.BlockSpec((1, SUBLANES, QKV), lambda b, j: (b, 0, 0)),
        ],
        out_shape=[
            jax.ShapeDtypeStruct((T_P, GW), bf16),
            jax.ShapeDtypeStruct((NB_P, NH, DH, DH), f32),
            jax.ShapeDtypeStruct((NB_P, SUBLANES, QKV), f32),
        ],
        scratch_shapes=[pltpu.VMEM((1, LB_G, GW), f32), pltpu.VMEM((1, LB_G, GW), f32),
                        pltpu.VMEM((1, LB_G, GW), f32), pltpu.VMEM((1, 2, LB_G, LANES), f32),
                        pltpu.VMEM((NH, DH, DH), f32)],
        compiler_params=_cparams(("parallel", "arbitrary")),
        name=f"gdn_prompt_{layer}",
    )(proj, proj, proj, w_gconv, gprm, g_gout)


BT_S = TILE // TPAD


def _gdn_s_kernel(x_ref, ba_ref, cin_ref, s0_ref, wc_ref, prm_ref, gout_ref,
                  o_ref, s_ref, cst_ref, st_scr):
    x = x_ref[:, :QKV]
    st_scr[...] = jnp.zeros_like(st_scr)
    for b in range(BT_S):
        st_scr[pl.ds((b + 1) * TPAD - (GCONV - 1), GCONV - 1), :] = cin_ref[0, b]
        cst_ref[b] = x_ref[pl.ds(b * TPAD + L_S - (GCONV - 1), GCONV - 1), :QKV]
    st = st_scr[...]
    trow = lax.broadcasted_iota(jnp.int32, (TILE, 1), 0) % TPAD
    xc = x * wc_ref[0, pl.ds(GCONV - 1, 1), :]
    for s in range(1, GCONV):
        xs = jnp.where(trow >= s, pltpu.roll(x, s, 0), pltpu.roll(st, TILE - TPAD + s, 0))
        xc = xc + xs * wc_ref[0, pl.ds(GCONV - 1 - s, 1), :]
    valid = trow < L_S
    qs, ks, vs, beta, g = _gdn_features(xc, ba_ref[...], prm_ref.at[0], valid)

    nsq = int(math.ceil(math.log2(L_S))) - 1
    gc_t, gt_t = _gate_cumsums(g, TPAD)
    hs = range(NH)
    gcs = [gc_t[:, NH + h:NH + h + 1] for h in hs]
    gts = [gt_t[:, NH + h:NH + h + 1] for h in hs]
    gc_tt = gc_t.T
    grows = [gc_tt[NH + h:NH + h + 1, :] for h in hs]
    us, ws, q_decs, k_decs, qks = _run(_delta_prepare(qs, ks, vs, [beta[:, h:h + 1] for h in hs], gcs, grows,
                                                      gts, TPAD, nsq))
    for h in hs:
        egt = jnp.exp(gts[h])
        vn, qs_ = [], []
        for b in range(BT_S):
            rr = slice(b * TPAD, (b + 1) * TPAD)
            s0 = s0_ref[0, b, h]
            wsb = _dot(jnp.concatenate([ws[h][rr], q_decs[h][rr]], axis=0), s0)
            vn_b = us[h][rr] - wsb[:TPAD]
            vn.append(vn_b)
            qs_.append(wsb[TPAD:])
            s_ref[b, h] = s0 * egt[b * TPAD:b * TPAD + 1, :] + _dot_tn(k_decs[h][rr], vn_b)
        o = jnp.concatenate(qs_, axis=0) + _dot(qks[h], jnp.concatenate(vn, axis=0))
        z = x_ref[:, pl.ds(QKV + h * DH, DH)]
        o_ref[:, h * DH:(h + 1) * DH] = _gated_out(o, z, gout_ref[0]).astype(o_ref.dtype)


def _gdn_sample(layer, proj, conv_in, s0, w_gconv, gprm, g_gout):
    row0 = T_P // TILE
    return pl.pallas_call(
        _gdn_s_kernel,
        grid=(T_S // TILE,),
        in_specs=[
            pl.BlockSpec((TILE, COL_GLU), lambda i: (row0 + i, 0)),
            pl.BlockSpec((TILE, LANES), lambda i: (row0 + i, COL_BA // LANES)),
            pl.BlockSpec((1, BT_S, GCONV - 1, QKV), lambda i: (layer, i, 0, 0)),
            pl.BlockSpec((1, BT_S, NH, DH, DH), lambda i: (layer, i, 0, 0, 0)),
            pl.BlockSpec((1, GCONV, QKV), lambda i: (layer, 0, 0)),
            pl.BlockSpec((1, SUBLANES, LANES), lambda i: (layer, 0, 0)),
            pl.BlockSpec((1, 1, DH), lambda i: (layer, 0, 0)),
        ],
        out_specs=[
            pl.BlockSpec((TILE, GW), lambda i: (i, 0)),
            pl.BlockSpec((BT_S, NH, DH, DH), lambda i: (i, 0, 0, 0)),
            pl.BlockSpec((BT_S, GCONV - 1, QKV), lambda i: (i, 0, 0)),
        ],
        out_shape=[
            jax.ShapeDtypeStruct((T_S, GW), bf16),
            jax.ShapeDtypeStruct((NB_S, NH, DH, DH), f32),
            jax.ShapeDtypeStruct((NB_S, GCONV - 1, QKV), f32),
        ],
        scratch_shapes=[pltpu.VMEM((TILE, QKV), f32)],
        compiler_params=_cparams(("parallel",)),
        name=f"gdn_sample_{layer}",
    )(proj, proj, conv_in, s0, w_gconv, gprm, g_gout)


def _group_ln_silu(y, gl_ref, bl_ref):
    outs = []
    for gi in range(CGROUPS):
        w = CW // CGROUPS
        yg = y[:, gi * w:(gi + 1) * w]
        mu = jnp.mean(yg, axis=-1, keepdims=True)
        d = yg - mu
        var = jnp.mean(d * d, axis=-1, keepdims=True)
        outs.append(d * lax.rsqrt(var + LN_EPS))
    yn = jnp.concatenate(outs, axis=1) * gl_ref[0] + bl_ref[0]
    return _silu(yn)


def _conf_p_kernel(x_ref, halo_ref, w_ref, b_ref, gl_ref, bl_ref, o_ref, hst_ref, xp_scr, sh_scr):
    j = pl.program_id(1)
    nj = pl.num_programs(1)
    h = x_ref[...]
    xp_scr[pl.ds(0, HIST_C), :] = jnp.where(j > 0, halo_ref[...], 0.0)
    xp_scr[pl.ds(HIST_C, LB_C), :] = h
    n_sh = HIST_C + LB_C - SUBLANES
    for r in range(1, SUBLANES):
        sh_scr[r - 1] = xp_scr[pl.ds(r, n_sh), :]
    off = HIST_C - (CK - 1)
    y = None
    for t in range(CK):
        a, r = divmod(off + t, SUBLANES)
        rows = pl.ds(a * SUBLANES, LB_C)
        win = xp_scr[rows, :] if r == 0 else sh_scr[r - 1, rows, :]
        term = win * w_ref[0, pl.ds(t, 1), :]
        y = term if y is None else y + term
    y = y + b_ref[0]
    o_ref[...] = _group_ln_silu(y, gl_ref, bl_ref).astype(o_ref.dtype)

    @pl.when(j == nj - 1)
    def _():
        hst_ref[0] = h[LB_C - HIST_C:, :]


def _conf_prompt(layer, glu, w_dw, b_dw, g_ln, b_ln):
    nj = L_P // LB_C
    return pl.pallas_call(
        _conf_p_kernel,
        grid=(NB_P, nj),
        in_specs=[
            pl.BlockSpec((LB_C, CW), lambda b, j: (b * nj + j, 0)),
            pl.BlockSpec((HIST_C, CW),
                         lambda b, j: (jnp.maximum((b * nj + j) * (LB_C // HIST_C) - 1, 0), 0)),
            pl.BlockSpec((1, CK, CW), lambda b, j: (layer, 0, 0)),
            pl.BlockSpec((1, 1, CW), lambda b, j: (layer, 0, 0)),
            pl.BlockSpec((1, 1, CW), lambda b, j: (layer, 0, 0)),
            pl.BlockSpec((1, 1, CW), lambda b, j: (layer, 0, 0)),
        ],
        out_specs=[
            pl.BlockSpec((LB_C, CW), lambda b, j: (b * nj + j, 0)),
            pl.BlockSpec((1, HIST_C, CW), lambda b, j: (b, 0, 0)),
        ],
        out_shape=[
            jax.ShapeDtypeStruct((T_P, CW), bf16),
            jax.ShapeDtypeStruct((NB_P, HIST_C, CW), f32),
        ],
        scratch_shapes=[pltpu.VMEM((HIST_C + LB_C, CW), f32),
                        pltpu.VMEM((SUBLANES - 1, HIST_C + LB_C - SUBLANES, CW), f32)],
        compiler_params=_cparams(("parallel", "arbitrary")),
        name=f"conf_prompt_{layer}",
    )(glu, glu, w_dw, b_dw, g_ln, b_ln)


BT_C = 16


def _conf_s_kernel(x_ref, hin_ref, wh_ref, wn_ref, b_ref, gl_ref, bl_ref, o_ref, hst_ref, y_scr):
    h = x_ref[...]
    trow = lax.broadcasted_iota(jnp.int32, (TPAD, 1), 0)
    nh = CK - 1
    for b in range(BT_C):
        hist = hin_ref[0, b]
        h8 = h[b * TPAD:(b + 1) * TPAD, :]
        y8 = jnp.zeros((TPAD, CW), f32)
        for t in range(L_S):
            yt = (jnp.sum(hist * wh_ref[0, t], axis=0, keepdims=True)
                  + jnp.sum(h8 * wn_ref[0, t], axis=0, keepdims=True))
            y8 = jnp.where(trow == t, yt, y8)
        y_scr[pl.ds(b * TPAD, TPAD), :] = y8
        hst_ref[b, pl.ds(0, nh - L_S), :] = hin_ref[0, b, pl.ds(L_S, nh - L_S), :]
        hst_ref[b, pl.ds(nh - L_S, L_S), :] = h8[:L_S, :]
    y = y_scr[...] + b_ref[0]
    o_ref[...] = _group_ln_silu(y, gl_ref, bl_ref).astype(o_ref.dtype)


def _conf_sample(layer, glu, hist_in, w_hist, w_new, b_dw, g_ln, b_ln):
    row0 = T_P // (BT_C * TPAD)
    return pl.pallas_call(
        _conf_s_kernel,
        grid=(NB_S // BT_C,),
        in_specs=[
            pl.BlockSpec((BT_C * TPAD, CW), lambda i: (row0 + i, 0)),
            pl.BlockSpec((1, BT_C, CK - 1, CW), lambda i: (layer, i, 0, 0)),
            pl.BlockSpec((1, L_S, CK - 1, CW), lambda i: (layer, 0, 0, 0)),
            pl.BlockSpec((1, L_S, TPAD, CW), lambda i: (layer, 0, 0, 0)),
            pl.BlockSpec((1, 1, CW), lambda i: (layer, 0, 0)),
            pl.BlockSpec((1, 1, CW), lambda i: (layer, 0, 0)),
            pl.BlockSpec((1, 1, CW), lambda i: (layer, 0, 0)),
        ],
        out_specs=[
            pl.BlockSpec((BT_C * TPAD, CW), lambda i: (i, 0)),
            pl.BlockSpec((BT_C, CK - 1, CW), lambda i: (i, 0, 0)),
        ],
        out_shape=[
            jax.ShapeDtypeStruct((T_S, CW), bf16),
            jax.ShapeDtypeStruct((NB_S, CK - 1, CW), f32),
        ],
        scratch_shapes=[pltpu.VMEM((BT_C * TPAD, CW), f32)],
        compiler_params=_cparams(("parallel",)),
        name=f"conf_sample_{layer}",
    )(glu, hist_in, w_hist, w_new, b_dw, g_ln, b_ln)


def _out_kernel(moe, x_ref, oap, obp, oas, obs, w_ref, g_ref,
                g1p, scp, shp, g1s, scs, shs, *rest):
    if moe:
        wr_ref, br_ref, xo_ref, h2_ref, rt_ref = rest
        hf_scr = h2_ref
    else:
        xo_ref, h2_ref, hf_scr = rest
    i = pl.program_id(0)

    def attn(oa, ob):
        return (jnp.dot(oa[...], w_ref[0, :GW, :], preferred_element_type=f32)
                + jnp.dot(ob[...], w_ref[0, GW:, :], preferred_element_type=f32))

    @pl.when(i < NT_P)
    def _():
        xn = x_ref[...] + g1p[0] * attn(oap, obp)
        xo_ref[...] = xn
        hf_scr[...] = _rms(xn, g_ref[0]) * (1.0 + scp[0]) + shp[0]

    @pl.when(i >= NT_P)
    def _():
        hf_scr[...] = attn(oas, obs)

        def body(rows, b):
            m = pl.ds(b, 1)
            xn = x_ref[rows, :] + g1s[m, :] * hf_scr[rows, :]
            xo_ref[rows, :] = xn
            hf_scr[rows, :] = _rms(xn, g_ref[0]) * (1.0 + scs[m, :]) + shs[m, :]
        _sample_rows(body)

    hf = hf_scr[...]
    if not moe:
        h2_ref[...] = hf.astype(h2_ref.dtype)
    if moe:
        logits = _dot_hi(hf, wr_ref[0]) + br_ref[0]
        lane = lax.broadcasted_iota(jnp.int32, logits.shape, 1)
        ex = jnp.exp(logits - jnp.max(logits, axis=-1, keepdims=True))
        probs = ex / jnp.sum(ex, axis=-1, keepdims=True)
        m1 = jnp.max(probs, axis=-1, keepdims=True)
        i1 = jnp.min(jnp.where(probs == m1, lane, LANES), axis=-1, keepdims=True)
        rest_p = jnp.where(lane == i1, -1.0, probs)
        m2 = jnp.max(rest_p, axis=-1, keepdims=True)
        i2 = jnp.min(jnp.where(rest_p == m2, lane, LANES), axis=-1, keepdims=True)
        den = m1 + m2
        rt_ref[...] = jnp.where(lane == 0, i1.astype(f32),
                                jnp.where(lane == 1, i2.astype(f32),
                                          jnp.where(lane == 2, m1 / den,
                                                    jnp.where(lane == 3, m2 / den, 0.0))))


def _out_proj(layer, moe, x, oa_p, ob_p, oa_s, ob_s, w_out, g2, modp, mods, wr=None, br=None):
    idx = layer // 2
    in_specs = [
        pl.BlockSpec((TM, D), lambda i: (i, 0)),
        pl.BlockSpec((TM, GW), lambda i: (jnp.minimum(i, NT_P - 1), 0)),
        pl.BlockSpec((TM, CW), lambda i: (jnp.minimum(i, NT_P - 1), 0)),
        pl.BlockSpec((TM, GW), lambda i: (0, 0)),
        pl.BlockSpec((TM, CW), lambda i: (0, 0)),
        pl.BlockSpec((1, D, D), lambda i: (layer, 0, 0)),
        pl.BlockSpec((1, 1, D), lambda i: (layer, 0, 0)),
        _mod_specs(G1, 1)[0], _mod_specs(SC2, 1)[0], _mod_specs(SH2, 1)[0],
        _mod_specs(G1, 1)[1], _mod_specs(SC2, 1)[1], _mod_specs(SH2, 1)[1],
    ]
    args = [x, oa_p, ob_p, oa_s, ob_s, w_out, g2, modp, modp, modp, mods, mods, mods]
    out_specs = [pl.BlockSpec((TM, D), lambda i: (i, 0)), pl.BlockSpec((TM, D), lambda i: (i, 0))]
    out_shape = [jax.ShapeDtypeStruct((T_ALL, D), f32),
                 jax.ShapeDtypeStruct((T_ALL, D), f32 if moe else bf16)]
    if moe:
        in_specs += [pl.BlockSpec((1, D, LANES), lambda i: (idx, 0, 0)),
                     pl.BlockSpec((1, 1, LANES), lambda i: (idx, 0, 0))]
        args += [wr, br]
        out_specs.append(pl.BlockSpec((TM, LANES), lambda i: (i, 0)))
        out_shape.append(jax.ShapeDtypeStruct((T_ALL, LANES), f32))
    return pl.pallas_call(
        functools.partial(_out_kernel, moe),
        grid=(NT,),
        in_specs=in_specs,
        out_specs=out_specs,
        out_shape=out_shape,
        scratch_shapes=[] if moe else [pltpu.VMEM((TM, D), f32)],
        compiler_params=_cparams(("parallel",)),
        name=f"out_proj_{layer}",
    )(*args)


def _residual(i, x_ref, f_ref, g2p, g2s, o_ref, gf_ref, os_ref=None):
    os_ref = o_ref if os_ref is None else os_ref

    def fin(v):
        return v if gf_ref is None else _rms(v, gf_ref[...])

    @pl.when(i < NT_P)
    def _():
        o_ref[...] = fin(x_ref[...] + g2p[0] * f_ref[...])

    @pl.when(i >= NT_P)
    def _():
        def body(rows, b):
            os_ref[rows, :] = fin(x_ref[rows, :] + g2s[pl.ds(b, 1), :] * f_ref[rows, :])
        _sample_rows(body)


def _ffn_kernel(h_ref, wg_ref, wu_ref, wd_ref, x_ref, g2p, g2s, o_ref, acc_scr):
    i = pl.program_id(0)
    h = h_ref[...]
    for c in range(FF_D // TF_D):
        cols = slice(c * TF_D, (c + 1) * TF_D)
        gate = jnp.dot(h, wg_ref[0, :, cols], preferred_element_type=f32)
        up = jnp.dot(h, wu_ref[0, :, cols], preferred_element_type=f32)
        part = jnp.dot((_silu(gate) * up).astype(bf16), wd_ref[0, cols, :], preferred_element_type=f32)
        if c == 0:
            acc_scr[...] = part
        else:
            acc_scr[...] += part
    _residual(i, x_ref, acc_scr, g2p, g2s, o_ref, None)


def _ffn_dense(layer, h2, x, wg, wu, wd, modp, mods):
    idx = layer // 2
    mp, ms = _mod_specs(G2, 1)
    return pl.pallas_call(
        _ffn_kernel,
        grid=(NT,),
        in_specs=[
            pl.BlockSpec((TM, D), lambda i: (i, 0)),
            _resident((1, D, FF_D), lambda i: (idx, 0, 0)),
            _resident((1, D, FF_D), lambda i: (idx, 0, 0)),
            _resident((1, FF_D, D), lambda i: (idx, 0, 0)),
            pl.BlockSpec((TM, D), lambda i: (i, 0)),
            mp, ms,
        ],
        out_specs=pl.BlockSpec((TM, D), lambda i: (i, 0)),
        out_shape=jax.ShapeDtypeStruct((T_ALL, D), f32),
        scratch_shapes=[pltpu.VMEM((TM, D), f32)],
        compiler_params=_cparams(("parallel",)),
        name=f"ffn_dense_{layer}",
    )(h2, wg, wu, wd, x, modp, mods)


def _gather_rows(idx_ref, base, stride, src_hbm, dst, sem):
    def issue(r, carry):
        row = idx_ref[base + stride * r]
        if len(src_hbm.shape) == 3:
            pltpu.make_async_copy(src_hbm.at[row], dst.at[r], sem).start()
        else:
            pltpu.make_async_copy(src_hbm.at[pl.ds(row, 1), :], dst.at[pl.ds(r, 1), :], sem).start()
        return carry
    lax.fori_loop(0, MOE_BLK, issue, 0, unroll=8)


def _wait_rows(src_hbm, dst, sem):
    pltpu.make_async_copy(src_hbm.at[pl.ds(0, MOE_BLK)], dst, sem).wait()


def _issue_rows(idx_ref, base, src_hbm, dst, sem, r0, n):
    for u in range(n):
        row = idx_ref[base + r0 + u]
        pltpu.make_async_copy(src_hbm.at[pl.ds(row, 1), :], dst.at[pl.ds(r0 + u, 1), :], sem).start()


def _expert_kernel(be_ref, tok_ref, nv_ref, nr_ref, h_hbm, wg_ref, wu_ref, wd_ref,
                   y_ref, xg_scr, xb_scr, acc_scr, sem):
    m = pl.program_id(0)
    f = pl.program_id(1)
    nv = nv_ref[0]
    active = m < nv
    slot = m % 2
    nxt = (m + 1) * MOE_BLK

    @pl.when(jnp.logical_and(f == 0, m == 0))
    def _():
        _gather_rows(tok_ref, 0, 1, h_hbm, xg_scr.at[0], sem.at[0])

    @pl.when(jnp.logical_and(f == 0, m <= nv))
    def _():
        _wait_rows(h_hbm, xg_scr.at[slot], sem.at[slot])

    @pl.when(jnp.logical_and(f == 0, active))
    def _():
        xb_scr[...] = xg_scr[slot].astype(bf16)
        _issue_rows(tok_ref, nxt, h_hbm, xg_scr.at[1 - slot], sem.at[1 - slot],
                    GATHER_CHUNK * (FF_E // TF_E), GATHER_TAIL)

    @pl.when(jnp.logical_and(f == 0, jnp.logical_not(active)))
    def _():
        y_ref[...] = jnp.zeros_like(y_ref)

    @pl.when(jnp.logical_and(f == 0, active))
    def _():
        acc_scr[...] = jnp.zeros_like(acc_scr)

    def ffn_step(rows):
        _issue_rows(tok_ref, nxt, h_hbm, xg_scr.at[1 - slot], sem.at[1 - slot], f * GATHER_CHUNK, GATHER_CHUNK)
        xb = xb_scr[pl.ds(0, rows), :]
        a = _silu(_dot(xb, wg_ref[0, 0])) * _dot(xb, wu_ref[0, 0])
        acc_scr[pl.ds(0, rows), :] += _dot(a, wd_ref[0, 0])

    short = nr_ref[m] <= MOE_BLK // 2

    @pl.when(jnp.logical_and(active, jnp.logical_not(short)))
    def _():
        ffn_step(MOE_BLK)

    @pl.when(jnp.logical_and(active, short))
    def _():
        ffn_step(MOE_BLK // 2)

    @pl.when(jnp.logical_and(f == pl.num_programs(1) - 1, active))
    def _():
        y_ref[...] = acc_scr[...].reshape(MOE_BLK, D // LANES, LANES)


def _experts(idx, h2, block_e, slot_tok, n_valid, n_rows, wg, wu, wd):
    nf = FF_E // TF_E

    def wmap_up(m, f, be, tok, nv, nr):
        return (idx, be[m], 0, jnp.where(m < nv[0], f, nf - 1))

    def wmap_down(m, f, be, tok, nv, nr):
        return (idx, be[m], jnp.where(m < nv[0], f, nf - 1), 0)

    grid_spec = pltpu.PrefetchScalarGridSpec(
        num_scalar_prefetch=4,
        grid=(N_BLK + 1, nf),
        in_specs=[
            pl.BlockSpec(memory_space=pl.ANY),
            pl.BlockSpec((1, 1, D, TF_E), wmap_up),
            pl.BlockSpec((1, 1, D, TF_E), wmap_up),
            pl.BlockSpec((1, 1, TF_E, D), wmap_down),
        ],
        out_specs=pl.BlockSpec((MOE_BLK, D // LANES, LANES), lambda m, f, be, tok, nv, nr: (m, 0, 0)),
        scratch_shapes=[pltpu.VMEM((2, MOE_BLK, D), f32), pltpu.VMEM((MOE_BLK, D), bf16),
                        pltpu.VMEM((MOE_BLK, D), f32), pltpu.SemaphoreType.DMA((2,))],
    )
    return pl.pallas_call(
        _expert_kernel,
        grid_spec=grid_spec,
        out_shape=jax.ShapeDtypeStruct(((N_BLK + 1) * MOE_BLK, D // LANES, LANES), f32),
        compiler_params=_cparams(("arbitrary", "arbitrary")),
        name=f"experts_{idx}",
    )(block_e, slot_tok, n_valid, n_rows, h2, wg, wu, wd)


def _combine_kernel(final, dest_ref, y_hbm, x_ref, rt_ref, g2p, g2s, *rest):
    if final:
        gf_ref, o_ref, os_ref, g_scr, f_scr, sem = rest
    else:
        o_ref, g_scr, f_scr, sem = rest
        gf_ref = os_ref = None
    i = pl.program_id(0)
    slot = i % 2

    def gather(tile, sl):
        for k in range(2):
            _gather_rows(dest_ref, 2 * tile * TM + k, 2, y_hbm, g_scr.at[sl, k], sem.at[sl, k])

    @pl.when(i == 0)
    def _():
        gather(0, 0)

    @pl.when(i + 1 < pl.num_programs(0))
    def _():
        gather(i + 1, 1 - slot)

    for k in range(2):
        _wait_rows(y_hbm, g_scr.at[slot, k], sem.at[slot, k])
    f_scr[...] = (rt_ref[:, 2:3] * g_scr[slot, 0].reshape(TM, D) + rt_ref[:, 3:4] * g_scr[slot, 1].reshape(TM, D))
    _residual(i, x_ref, f_scr, g2p, g2s, o_ref, gf_ref, os_ref)


def _combine(final, dest, yb, x, rt, modp, mods, g_final):
    mp, ms = _mod_specs(G2, 1)
    wrap = lambda spec: pl.BlockSpec(spec.block_shape, lambda i, d, _f=spec.index_map: _f(i))
    in_specs = [pl.BlockSpec(memory_space=pl.ANY),
                pl.BlockSpec((TM, D), lambda i, d: (i, 0)),
                pl.BlockSpec((TM, LANES), lambda i, d: (i, 0)), wrap(mp), wrap(ms)]
    args = [yb, x, rt, modp, mods]
    if final:
        in_specs.append(pl.BlockSpec((1, D), lambda i, d: (0, 0)))
        args.append(g_final)
        out_specs = [pl.BlockSpec((TM, D), lambda i, d: (jnp.minimum(i, NT_P - 1), 0)),
                     pl.BlockSpec((T_S, D), lambda i, d: (0, 0))]
        out_shape = [jax.ShapeDtypeStruct((T_P, D), f32), jax.ShapeDtypeStruct((T_S, D), f32)]
    else:
        out_specs = pl.BlockSpec((TM, D), lambda i, d: (i, 0))
        out_shape = jax.ShapeDtypeStruct((T_ALL, D), f32)
    grid_spec = pltpu.PrefetchScalarGridSpec(
        num_scalar_prefetch=1,
        grid=(NT,),
        in_specs=in_specs,
        out_specs=out_specs,
        scratch_shapes=[pltpu.VMEM((2, 2, TM, D // LANES, LANES), f32), pltpu.VMEM((TM, D), f32),
                        pltpu.SemaphoreType.DMA((2, 2))],
    )
    return pl.pallas_call(
        functools.partial(_combine_kernel, final),
        grid_spec=grid_spec,
        out_shape=out_shape,
        compiler_params=_cparams(("arbitrary",)),
        name="combine_final" if final else "combine",
    )(dest, *args)


def _route(rt):
    e = rt[:, :2].astype(jnp.int32).reshape(N_ASG)
    tok = jnp.arange(N_ASG, dtype=jnp.int32) // 2
    real = jnp.logical_or(tok < T_P, (tok - T_P) % TPAD < L_S)
    onehot = jnp.logical_and(e[:, None] == jnp.arange(NE, dtype=jnp.int32)[None, :], real[:, None]).astype(jnp.int32)
    csum = jnp.cumsum(onehot, axis=0)
    counts = csum[-1]
    rank = jnp.sum(onehot * (csum - 1), axis=1)
    padded = (counts + MOE_BLK - 1) // MOE_BLK * MOE_BLK
    pad_end = jnp.cumsum(padded)
    pad_start = pad_end - padded
    dest = (jnp.sum(onehot * pad_start[None, :], axis=1) + rank).astype(jnp.int32)
    dest = jnp.where(real, dest, jnp.arange(N_ASG, dtype=jnp.int32) % MOE_BLK)
    n_slots = (N_BLK + 1) * MOE_BLK
    slot_tok = jnp.zeros((n_slots,), jnp.int32).at[jnp.where(real, dest, n_slots)].set(
        tok, mode="drop", unique_indices=True)
    n_valid = (pad_end[-1] // MOE_BLK).astype(jnp.int32)
    blk = jnp.minimum(jnp.arange(N_BLK + 1, dtype=jnp.int32), n_valid - 1)
    first_slot = (blk * MOE_BLK)[:, None]
    block_e = jnp.minimum(jnp.sum((pad_end[None, :] <= first_slot).astype(jnp.int32), axis=1), NE - 1)
    n_rows = jnp.clip(counts[block_e] - (first_slot[:, 0] - pad_start[block_e]), 0, MOE_BLK).astype(jnp.int32)
    return dest, slot_tok, block_e, n_valid.reshape(1), n_rows


def kernel(x_prompt, x_sample, c_prompt, c_sample, state_gdn, state_gdn_conv, state_conf_conv, w_ada, b_ada, g_norm1, g_norm2, w_in, w_gdn_conv, a_log, dt_bias, g_gdn_out, w_conf_dw, b_conf_dw, g_conf_ln, b_conf_ln, w_out, w_ff_gate, w_ff_up, w_ff_down, w_router, b_router, w_exp_gate, w_exp_up, w_exp_down, g_final):
    xs_pad = jnp.pad(x_sample, ((0, 0), (0, TPAD - L_S), (0, 0)))
    x = jnp.concatenate([x_prompt.reshape(T_P, D), xs_pad.reshape(T_S, D)], axis=0)
    c_all = jnp.concatenate([c_prompt, c_sample], axis=0)
    o1, o2, o4 = QKV + GW, QKV + GW + 2 * NH, QKV + GW + 2 * NH + 2 * CW
    w_cat = jnp.concatenate([w_in[:, :, :o2], jnp.zeros((DEPTH, D, W_CAT - o4), f32), w_in[:, :, o2:o4]],
                            axis=-1).astype(bf16)
    lane_pad = ((0, 0), (NH, LANES - 2 * NH))
    gprm = jnp.stack([jnp.pad(a_log, lane_pad), jnp.pad(dt_bias, lane_pad)], axis=1)
    gprm = jnp.pad(gprm, ((0, 0), (0, SUBLANES - 2), (0, 0)))
    wr_pad = jnp.pad(w_router, ((0, 0), (0, 0), (0, LANES - NE)))
    br_pad = jnp.pad(b_router, ((0, 0), (0, LANES - NE)), constant_values=-1e30).reshape(-1, 1, LANES)
    w_shift = jnp.stack([jnp.pad(w_conf_dw, ((0, 0), (t, TPAD - 1 - t), (0, 0))) for t in range(L_S)], axis=1)
    w_hist, w_new = w_shift[:, :, :CK - 1], w_shift[:, :, CK - 1:]
    w_out = w_out.astype(bf16)
    w_ff_gate, w_ff_up, w_ff_down = (w.astype(bf16) for w in (w_ff_gate, w_ff_up, w_ff_down))
    g1 = g_norm1.reshape(DEPTH, 1, D)
    g2 = g_norm2.reshape(DEPTH, 1, D)
    gout = g_gdn_out.reshape(DEPTH, 1, DH)
    b_dw = b_conf_dw.reshape(DEPTH, 1, CW)
    g_ln = g_conf_ln.reshape(DEPTH, 1, CW)
    b_ln = b_conf_ln.reshape(DEPTH, 1, CW)

    mod = _ada(c_all, w_ada, b_ada)

    sp_l, cp_l, fp_l, ss_l, cs_l, fs_l = [], [], [], [], [], []
    for layer in range(DEPTH):
        modp = mod[layer, :NB_P].reshape(NB_P, 1, N_MOD * D)
        mods = mod[layer, NB_P:]
        proj, glu = _in_proj(layer, x, g1, modp, mods, w_cat)
        oa_p, s_p, c_p = _gdn_prompt(layer, proj, w_gdn_conv, gprm, gout)
        oa_s, s_s, c_s = _gdn_sample(layer, proj, state_gdn_conv, state_gdn, w_gdn_conv, gprm, gout)
        ss_l.append(s_s)
        ob_p, f_p = _conf_prompt(layer, glu, w_conf_dw, b_dw, g_ln, b_ln)
        ob_s, f_s = _conf_sample(layer, glu, state_conf_conv, w_hist, w_new, b_dw, g_ln, b_ln)
        sp_l.append(s_p)
        cp_l.append(c_p[:, TPAD - (GCONV - 1):, :])
        fp_l.append(f_p[:, HIST_C - (CK - 1):, :])
        cs_l.append(c_s)
        fs_l.append(f_s)
        if layer % 2 == 0:
            x, h2 = _out_proj(layer, False, x, oa_p, ob_p, oa_s, ob_s, w_out, g2, modp, mods)
            x = _ffn_dense(layer, h2, x, w_ff_gate, w_ff_up, w_ff_down, modp, mods)
        else:
            x, h2, rt = _out_proj(layer, True, x, oa_p, ob_p, oa_s, ob_s, w_out, g2, modp, mods, wr_pad, br_pad)
            dest, slot_tok, block_e, n_valid, n_rows = _route(rt)
            yb = _experts(layer // 2, h2, block_e, slot_tok, n_valid, n_rows, w_exp_gate, w_exp_up, w_exp_down)
            final = layer == DEPTH - 1
            x = _combine(final, dest, yb, x, rt, modp, mods, g_final.reshape(1, D))

    y_p, y_s = x
    y_prompt = y_p.reshape(NB_P, L_P, D)
    y_sample = y_s.reshape(NB_S, TPAD, D)[:, :L_S, :]
    return (y_prompt, y_sample, jnp.stack(sp_l), jnp.stack(cp_l), jnp.stack(fp_l),
            jnp.stack(ss_l), jnp.stack(cs_l), jnp.stack(fs_l))
```

```python
import functools
import math

import jax
import jax.numpy as jnp
from jax import lax
from jax.experimental import pallas as pl
from jax.experimental.pallas import tpu as pltpu

f32 = jnp.float32
bf16 = jnp.bfloat16
HIGHEST = lax.Precision.HIGHEST

D = 1024
NB_P, L_P = 8, 2048
NB_S, L_S = 128, 4
DEPTH = 4
NH, DH = 4, 128
GW = NH * DH
QKV = 3 * GW
GCONV = 4
GCHUNK = 64
CW = D - GW
CGROUPS = 4
CK = 31
FF_D = 2816
NE = 8
FF_E = 3584
N_MOD = 6
RMS_EPS = 1e-6
LN_EPS = 1e-5

SUBLANES = 8
LANES = 128
VMEM_LIMIT = 56 * 1024 * 1024

TPAD = SUBLANES
T_P = NB_P * L_P
T_S = NB_S * TPAD
T_ALL = T_P + T_S
TM = 1024
NT = T_ALL // TM
NT_P = T_P // TM
TILES_PER_SEQ = L_P // TM
W_CAT = 3328
PROJ_W = 2304
COL_GLU = 2048
COL_BA = 2048
PROJ_CHUNKS = ((0, 768), (768, 768), (1536, 768))
HIST_C = 32
SH1, SC1, G1, SH2, SC2, G2 = range(N_MOD)

LB_G = 512
LB_C = 512
TILE = 128
TF_D = 1408
TF_E = 512
MOE_BLK = 1024
N_ASG = 2 * T_ALL
N_BLK = N_ASG // MOE_BLK + NE
GATHER_CHUNK = MOE_BLK // (FF_E // TF_E)
GATHER_TAIL = MOE_BLK - GATHER_CHUNK * (FF_E // TF_E)


def _cparams(sem):
    return pltpu.CompilerParams(dimension_semantics=sem, vmem_limit_bytes=VMEM_LIMIT)


def _dot(a, b):
    return jnp.dot(a.astype(bf16), b.astype(bf16), preferred_element_type=f32)


def _dot_hi(a, b):
    return jnp.dot(a, b, precision=HIGHEST, preferred_element_type=f32)


def _dot_nt(a, b):
    return lax.dot_general(a.astype(bf16), b.astype(bf16), (((1,), (1,)), ((), ())),
                           preferred_element_type=f32)


def _dot_nt_hi(a, b):
    return lax.dot_general(a, b, (((1,), (1,)), ((), ())), precision=HIGHEST,
                           preferred_element_type=f32)


def _dot_tn(a, b):
    return lax.dot_general(a, b, (((0,), (0,)), ((), ())), preferred_element_type=f32)


def _silu(x):
    return x * jax.nn.sigmoid(x)


def _rms(x, g):
    return x * lax.rsqrt(jnp.mean(x * x, axis=-1, keepdims=True) + RMS_EPS) * g


def _ada_kernel(c_ref, w_ref, b_ref, o_ref):
    a = _silu(c_ref[...])
    o_ref[0] = _dot(a, w_ref[0]) + b_ref[0]


def _ada(c_all, w_ada, b_ada):
    n = c_all.shape[0]
    tn = 1536
    return pl.pallas_call(
        _ada_kernel,
        grid=(DEPTH, N_MOD * D // tn),
        in_specs=[
            pl.BlockSpec((n, D), lambda l, j: (0, 0)),
            pl.BlockSpec((1, D, tn), lambda l, j: (l, 0, j)),
            pl.BlockSpec((1, 1, tn), lambda l, j: (l, 0, j)),
        ],
        out_specs=pl.BlockSpec((1, n, tn), lambda l, j: (l, 0, j)),
        out_shape=jax.ShapeDtypeStruct((DEPTH, n, N_MOD * D), f32),
        compiler_params=_cparams(("parallel", "parallel")),
        name="ada",
    )(c_all, w_ada, b_ada.reshape(DEPTH, 1, N_MOD * D))


def _mod_specs(k, ngrid):
    if ngrid == 1:
        return [pl.BlockSpec((1, 1, D), lambda i: (jnp.minimum(i // TILES_PER_SEQ, NB_P - 1), 0, k)),
                pl.BlockSpec((NB_S, D), lambda i: (0, k))]
    return [pl.BlockSpec((1, 1, D), lambda i, j: (jnp.minimum(i // TILES_PER_SEQ, NB_P - 1), 0, k)),
            pl.BlockSpec((NB_S, D), lambda i, j: (0, k))]


def _sample_rows(body):
    def step(b, carry):
        body(pl.ds(pl.multiple_of(b * TPAD, TPAD), TPAD), b)
        return carry
    lax.fori_loop(0, NB_S, step, 0)


def _in_kernel(x_ref, g_ref, shp, scp, shs, scs, w_ref, o_ref, glu_ref, h_scr, hf_scr):
    i = pl.program_id(0)

    @pl.when(i < NT_P)
    def _():
        h = _rms(x_ref[...], g_ref[0]) * (1.0 + scp[0]) + shp[0]
        h_scr[...] = h.astype(bf16)

    @pl.when(i >= NT_P)
    def _():
        def body(rows, b):
            hf_scr[rows, :] = (_rms(x_ref[rows, :], g_ref[0]) * (1.0 + scs[pl.ds(b, 1), :])
                               + shs[pl.ds(b, 1), :])
        _sample_rows(body)
        h_scr[...] = hf_scr[...].astype(bf16)

    for c0, cw in PROJ_CHUNKS:
        cols = slice(c0, c0 + cw)
        o_ref[:, cols] = jnp.dot(h_scr[...], w_ref[0, :, cols], preferred_element_type=f32)
    u = jnp.dot(h_scr[...], w_ref[0, :, PROJ_W:PROJ_W + CW], preferred_element_type=f32)
    gate = jnp.dot(h_scr[...], w_ref[0, :, PROJ_W + CW:], preferred_element_type=f32)
    glu_ref[...] = u * jax.nn.sigmoid(gate)


def _resident(block_shape, index_map):
    return pl.BlockSpec(block_shape, index_map, pipeline_mode=pl.Buffered(1))


def _in_proj(layer, x, g1, modp, mods, w_cat):
    return pl.pallas_call(
        _in_kernel,
        grid=(NT,),
        in_specs=[
            pl.BlockSpec((TM, D), lambda i: (i, 0)),
            pl.BlockSpec((1, 1, D), lambda i: (layer, 0, 0)),
            *_mod_specs(SH1, 1)[:1], *_mod_specs(SC1, 1)[:1],
            *_mod_specs(SH1, 1)[1:], *_mod_specs(SC1, 1)[1:],
            _resident((1, D, W_CAT), lambda i: (layer, 0, 0)),
        ],
        out_specs=[pl.BlockSpec((TM, PROJ_W), lambda i: (i, 0)), pl.BlockSpec((TM, CW), lambda i: (i, 0))],
        out_shape=[jax.ShapeDtypeStruct((T_ALL, PROJ_W), f32), jax.ShapeDtypeStruct((T_ALL, CW), f32)],
        scratch_shapes=[pltpu.VMEM((TM, D), bf16), pltpu.VMEM((TM, D), f32)],
        compiler_params=_cparams(("parallel",)),
        name=f"in_proj_{layer}",
    )(x, g1, modp, modp, mods, mods, w_cat)


def _tile_masks(blk):
    ri = lax.broadcasted_iota(jnp.int32, (TILE, TILE), 0)
    ci = lax.broadcasted_iota(jnp.int32, (TILE, TILE), 1)
    same = (ri // blk) == (ci // blk)
    incl = jnp.logical_and(same, ri >= ci)
    strict = jnp.logical_and(same, ri > ci)
    return same, incl, strict


def _gate_cumsums(g_all, blk):
    same, incl, _ = _tile_masks(blk)
    gc = _dot_hi(incl.astype(f32), g_all)
    if TILE == 2 * blk:
        row = lax.broadcasted_iota(jnp.int32, (TILE, 1), 0)
        gt = jnp.where(row < blk, gc[blk - 1:blk, :], gc[TILE - 1:TILE, :])
    else:
        gt = _dot_hi(same.astype(f32), g_all)
    return gc, gt


def _run(gen):
    try:
        while True:
            next(gen)
    except StopIteration as stop:
        return stop.value


def _unit_lower_inverse(a_mats, blk, base, nsq):
    ri = lax.broadcasted_iota(jnp.int32, (TILE, TILE), 0)
    ci = lax.broadcasted_iota(jnp.int32, (TILE, TILE), 1)
    ps = [jnp.where((ri // base) == (ci // base), -a, 0.0) for a in a_mats]
    eye = (ri == ci).astype(f32)
    ts = [eye + p for p in ps]
    for _ in range(nsq):
        ps = [_dot(p, p) for p in ps]
        yield
        ts = [t + _dot(p, t) for p, t in zip(ps, ts)]
        yield
    b = base
    while b < blk:
        off = jnp.logical_and((ri // (2 * b)) == (ci // (2 * b)), (ri // b) != (ci // b))
        ms = [_dot(jnp.where(off, a, 0.0), t) for a, t in zip(a_mats, ts)]
        yield
        ts = [t - _dot(t, m) for t, m in zip(ts, ms)]
        yield
        b *= 2
    return ts


def _delta_prepare(qs, ks, vs, betas, gcs, grows, gts, blk, nsq):
    _, incl, strict = _tile_masks(blk)
    n = range(len(qs))
    dmats = [gc - gr for gc, gr in zip(gcs, grows)]
    decays = [jnp.where(incl, jnp.exp(jnp.where(incl, d, 0.0)), 0.0) for d in dmats]
    egcs = [jnp.exp(gc) for gc in gcs]
    kbs = [ks[i] * betas[i] for i in n]
    a_mats = [jnp.where(strict, _dot_nt(kbs[i], ks[i]) * decays[i], 0.0) for i in n]
    yield
    qks = [_dot_nt(qs[i], ks[i]) * decays[i] for i in n]
    yield
    xs = [jnp.concatenate([vs[i] * betas[i], kbs[i] * egcs[i]], axis=1) for i in n]
    tinvs = yield from _unit_lower_inverse(a_mats, blk, SUBLANES, nsq)
    xs = [_dot(t, x) for t, x in zip(tinvs, xs)]
    yield
    us = [x[:, :DH] for x in xs]
    ws = [x[:, DH:] for x in xs]
    q_decs = [qs[i] * egcs[i] for i in n]
    k_decs = [ks[i] * jnp.exp(gts[i] - gcs[i]) for i in n]
    return us, ws, q_decs, k_decs, qks


def _gdn_features(xc, ba, prm_ref, valid):
    s = _silu(xc)
    if valid is not None:
        s = jnp.where(valid, s, 0.0)
    qs, ks, vs = [], [], []
    for h in range(NH):
        qh = s[:, h * DH:(h + 1) * DH]
        kh = s[:, GW + h * DH:GW + (h + 1) * DH]
        qs.append(qh * lax.rsqrt(jnp.sum(qh * qh, axis=-1, keepdims=True) + 1e-6) * (DH ** -0.5))
        ks.append(kh * lax.rsqrt(jnp.sum(kh * kh, axis=-1, keepdims=True) + 1e-6))
        vs.append(s[:, 2 * GW + h * DH:2 * GW + (h + 1) * DH])
    beta = jax.nn.sigmoid(ba)
    xs = ba + prm_ref[pl.ds(1, 1), :]
    softplus = jnp.maximum(xs, 0.0) + jnp.log1p(jnp.exp(-jnp.abs(xs)))
    g = -jnp.exp(prm_ref[pl.ds(0, 1), :]) * softplus
    if valid is not None:
        beta = jnp.where(valid, beta, 0.0)
        g = jnp.where(valid, g, 0.0)
    return qs, ks, vs, beta, g


def _gated_out(o, z, gout):
    return _rms(o, gout) * _silu(z)


def _gate_lanes(ba, prm_ref):
    xs = ba + prm_ref[pl.ds(1, 1), :]
    softplus = jnp.maximum(xs, 0.0) + jnp.log1p(jnp.exp(-jnp.abs(xs)))
    return jax.nn.sigmoid(ba), -jnp.exp(prm_ref[pl.ds(0, 1), :]) * softplus


def _gdn_p_kernel(x_ref, halo_ref, ba_ref, wc_ref, prm_ref, gout_ref,
                  o_ref, s_ref, cst_ref,
                  q_scr, k_scr, v_scr, bg_scr, st_scr):
    j = pl.program_id(1)
    nj = pl.num_programs(1)
    slot = 0
    hs = range(NH)
    cols = [slice(h * DH, (h + 1) * DH) for h in hs]

    def features(src_ref, gate_ref, halo, dst):
        for c in range(QKV // DH):
            cc = slice(c * DH, (c + 1) * DH)
            x = src_ref[:, cc]
            xp = jnp.concatenate([halo(cc), x], axis=0)
            xc = x * wc_ref[0, pl.ds(GCONV - 1, 1), cc]
            for sft in range(1, GCONV):
                xc = xc + pltpu.roll(xp, sft, 0)[SUBLANES:, :] * wc_ref[0, pl.ds(GCONV - 1 - sft, 1), cc]
            a = _silu(xc)
            if c < NH:
                q_scr[dst, :, cc] = a * lax.rsqrt(jnp.sum(a * a, axis=-1, keepdims=True) + 1e-6) * (DH ** -0.5)
            elif c < 2 * NH:
                k_scr[dst, :, cols[c - NH]] = a * lax.rsqrt(jnp.sum(a * a, axis=-1, keepdims=True) + 1e-6)
            else:
                v_scr[dst, :, cols[c - 2 * NH]] = a
            yield
        beta, g = _gate_lanes(gate_ref[...], prm_ref.at[0])
        bg_scr[dst, 0] = beta
        bg_scr[dst, 1] = g
        yield

    @pl.when(j == 0)
    def _():
        st_scr[...] = jnp.zeros_like(st_scr)

    _run(features(x_ref, ba_ref, lambda cc: jnp.where(j > 0, halo_ref[:, cc], 0.0), slot))

    nsq = int(math.log2(SUBLANES)) - 1
    nblk = TILE // GCHUNK
    ntile = LB_G // TILE

    def chains():
        rows, qs, ks, vs, betas, gcs, gts, grows = [], [], [], [], [], [], [], []
        for ti in range(ntile):
            r = pl.ds(ti * TILE, TILE)
            rows.append(r)
            beta_t = bg_scr[slot, 0, r, :]
            gc_t, gt_t = _gate_cumsums(bg_scr[slot, 1, r, :], GCHUNK)
            gc_tt = gc_t.T
            for h in hs:
                qs.append(q_scr[slot, r, cols[h]])
                ks.append(k_scr[slot, r, cols[h]])
                vs.append(v_scr[slot, r, cols[h]])
                betas.append(beta_t[:, h:h + 1])
                gcs.append(gc_t[:, NH + h:NH + h + 1])
                gts.append(gt_t[:, NH + h:NH + h + 1])
                grows.append(gc_tt[NH + h:NH + h + 1, :])
        yield
        us, ws, q_decs, k_decs, qks = yield from _delta_prepare(qs, ks, vs, betas, gcs, grows, gts, GCHUNK, nsq)
        s_cur = [st_scr[h] for h in hs]
        for ti in range(ntile):
            ch = [ti * NH + h for h in hs]
            egts = [jnp.exp(gts[i]) for i in ch]
            vn = [[] for _ in hs]
            qs_ = [[] for _ in hs]
            for c in range(nblk):
                rr = slice(c * GCHUNK, (c + 1) * GCHUNK)
                wss = [_dot(jnp.concatenate([ws[ch[h]][rr], q_decs[ch[h]][rr]], axis=0), s_cur[h]) for h in hs]
                yield
                for h in hs:
                    vn[h].append(us[ch[h]][rr] - wss[h][:GCHUNK])
                    qs_[h].append(wss[h][GCHUNK:])
                s_cur = [s_cur[h] * egts[h][c * GCHUNK:c * GCHUNK + 1, :] + _dot_tn(k_decs[ch[h]][rr], vn[h][c])
                         for h in hs]
                yield
            os_ = [jnp.concatenate(qs_[h], axis=0) + _dot(qks[ch[h]], jnp.concatenate(vn[h], axis=0)) for h in hs]
            for h in hs:
                z = x_ref[rows[ti], pl.ds(QKV + h * DH, DH)]
                o_ref[rows[ti], cols[h]] = _gated_out(os_[h], z, gout_ref[0]).astype(o_ref.dtype)
            yield
        for h in hs:
            st_scr[h] = s_cur[h]

    _run(chains())

    @pl.when(j == nj - 1)
    def _():
        s_ref[0] = st_scr[...]
        cst_ref[0] = x_ref[pl.ds(LB_G - SUBLANES, SUBLANES), :QKV]


def _gdn_prompt(layer, proj, w_gconv, gprm, g_gout):
    nj = L_P // LB_G
    return pl.pallas_call(
        _gdn_p_kernel,
        grid=(NB_P, nj),
        in_specs=[
            pl.BlockSpec((LB_G, COL_GLU), lambda b, j: (b * nj + j, 0)),
            pl.BlockSpec((SUBLANES, COL_GLU),
                         lambda b, j: (jnp.maximum((b * nj + j) * (LB_G // SUBLANES) - 1, 0), 0)),
            pl.BlockSpec((LB_G, LANES), lambda b, j: (b * nj + j, COL_BA // LANES)),
            pl.BlockSpec((1, GCONV, QKV), lambda b, j: (layer, 0, 0)),
            pl.BlockSpec((1, SUBLANES, LANES), lambda b, j: (layer, 0, 0)),
            pl.BlockSpec((1, 1, DH), lambda b, j: (layer, 0, 0)),
        ],
        out_specs=[
            pl.BlockSpec((LB_G, GW), lambda b, j: (b * nj + j, 0)),
            pl.BlockSpec((1, NH, DH, DH), lambda b, j: (b, 0, 0, 0)),
            pl.BlockSpec((1, SUBLANES, QKV), lambda b, j: (b, 0, 0)),
        ],
        out_shape=[
            jax.ShapeDtypeStruct((T_P, GW), bf16),
            jax.ShapeDtypeStruct((NB_P, NH, DH, DH), f32),
            jax.ShapeDtypeStruct((NB_P, SUBLANES, QKV), f32),
        ],
        scratch_shapes=[pltpu.VMEM((1, LB_G, GW), f32), pltpu.VMEM((1, LB_G, GW), f32),
                        pltpu.VMEM((1, LB_G, GW), f32), pltpu.VMEM((1, 2, LB_G, LANES), f32),
                        pltpu.VMEM((NH, DH, DH), f32)],
        compiler_params=_cparams(("parallel", "arbitrary")),
        name=f"gdn_prompt_{layer}",
    )(proj, proj, proj, w_gconv, gprm, g_gout)


BT_S = TILE // TPAD


def _gdn_s_kernel(x_ref, ba_ref, cin_ref, s0_ref, wc_ref, prm_ref, gout_ref,
                  o_ref, s_ref, cst_ref, st_scr):
    x = x_ref[:, :QKV]
    st_scr[...] = jnp.zeros_like(st_scr)
    for b in range(BT_S):
        st_scr[pl.ds((b + 1) * TPAD - (GCONV - 1), GCONV - 1), :] = cin_ref[0, b]
        cst_ref[b] = x_ref[pl.ds(b * TPAD + L_S - (GCONV - 1), GCONV - 1), :QKV]
    st = st_scr[...]
    trow = lax.broadcasted_iota(jnp.int32, (TILE, 1), 0) % TPAD
    xc = x * wc_ref[0, pl.ds(GCONV - 1, 1), :]
    for s in range(1, GCONV):
        xs = jnp.where(trow >= s, pltpu.roll(x, s, 0), pltpu.roll(st, TILE - TPAD + s, 0))
        xc = xc + xs * wc_ref[0, pl.ds(GCONV - 1 - s, 1), :]
    valid = trow < L_S
    qs, ks, vs, beta, g = _gdn_features(xc, ba_ref[...], prm_ref.at[0], valid)

    nsq = int(math.ceil(math.log2(L_S))) - 1
    gc_t, gt_t = _gate_cumsums(g, TPAD)
    hs = range(NH)
    gcs = [gc_t[:, NH + h:NH + h + 1] for h in hs]
    gts = [gt_t[:, NH + h:NH + h + 1] for h in hs]
    gc_tt = gc_t.T
    grows = [gc_tt[NH + h:NH + h + 1, :] for h in hs]
    us, ws, q_decs, k_decs, qks = _run(_delta_prepare(qs, ks, vs, [beta[:, h:h + 1] for h in hs], gcs, grows,
                                                      gts, TPAD, nsq))
    for h in hs:
        egt = jnp.exp(gts[h])
        vn, qs_ = [], []
        for b in range(BT_S):
            rr = slice(b * TPAD, (b + 1) * TPAD)
            s0 = s0_ref[0, b, h]
            wsb = _dot(jnp.concatenate([ws[h][rr], q_decs[h][rr]], axis=0), s0)
            vn_b = us[h][rr] - wsb[:TPAD]
            vn.append(vn_b)
            qs_.append(wsb[TPAD:])
            s_ref[b, h] = s0 * egt[b * TPAD:b * TPAD + 1, :] + _dot_tn(k_decs[h][rr], vn_b)
        o = jnp.concatenate(qs_, axis=0) + _dot(qks[h], jnp.concatenate(vn, axis=0))
        z = x_ref[:, pl.ds(QKV + h * DH, DH)]
        o_ref[:, h * DH:(h + 1) * DH] = _gated_out(o, z, gout_ref[0]).astype(o_ref.dtype)


def _gdn_sample(layer, proj, conv_in, s0, w_gconv, gprm, g_gout):
    row0 = T_P // TILE
    return pl.pallas_call(
        _gdn_s_kernel,
        grid=(T_S // TILE,),
        in_specs=[
            pl.BlockSpec((TILE, COL_GLU), lambda i: (row0 + i, 0)),
            pl.BlockSpec((TILE, LANES), lambda i: (row0 + i, COL_BA // LANES)),
            pl.BlockSpec((1, BT_S, GCONV - 1, QKV), lambda i: (layer, i, 0, 0)),
            pl.BlockSpec((1, BT_S, NH, DH, DH), lambda i: (layer, i, 0, 0, 0)),
            pl.BlockSpec((1, GCONV, QKV), lambda i: (layer, 0, 0)),
            pl.BlockSpec((1, SUBLANES, LANES), lambda i: (layer, 0, 0)),
            pl.BlockSpec((1, 1, DH), lambda i: (layer, 0, 0)),
        ],
        out_specs=[
            pl.BlockSpec((TILE, GW), lambda i: (i, 0)),
            pl.BlockSpec((BT_S, NH, DH, DH), lambda i: (i, 0, 0, 0)),
            pl.BlockSpec((BT_S, GCONV - 1, QKV), lambda i: (i, 0, 0)),
        ],
        out_shape=[
            jax.ShapeDtypeStruct((T_S, GW), bf16),
            jax.ShapeDtypeStruct((NB_S, NH, DH, DH), f32),
            jax.ShapeDtypeStruct((NB_S, GCONV - 1, QKV), f32),
        ],
        scratch_shapes=[pltpu.VMEM((TILE, QKV), f32)],
        compiler_params=_cparams(("parallel",)),
        name=f"gdn_sample_{layer}",
    )(proj, proj, conv_in, s0, w_gconv, gprm, g_gout)


def _group_ln_silu(y, gl_ref, bl_ref):
    outs = []
    for gi in range(CGROUPS):
        w = CW // CGROUPS
        yg = y[:, gi * w:(gi + 1) * w]
        mu = jnp.mean(yg, axis=-1, keepdims=True)
        d = yg - mu
        var = jnp.mean(d * d, axis=-1, keepdims=True)
        outs.append(d * lax.rsqrt(var + LN_EPS))
    yn = jnp.concatenate(outs, axis=1) * gl_ref[0] + bl_ref[0]
    return _silu(yn)


def _conf_p_kernel(x_ref, halo_ref, w_ref, b_ref, gl_ref, bl_ref, o_ref, hst_ref, xp_scr, sh_scr):
    j = pl.program_id(1)
    nj = pl.num_programs(1)
    h = x_ref[...]
    xp_scr[pl.ds(0, HIST_C), :] = jnp.where(j > 0, halo_ref[...], 0.0)
    xp_scr[pl.ds(HIST_C, LB_C), :] = h
    n_sh = HIST_C + LB_C - SUBLANES
    for r in range(1, SUBLANES):
        sh_scr[r - 1] = xp_scr[pl.ds(r, n_sh), :]
    off = HIST_C - (CK - 1)
    y = None
    for t in range(CK):
        a, r = divmod(off + t, SUBLANES)
        rows = pl.ds(a * SUBLANES, LB_C)
        win = xp_scr[rows, :] if r == 0 else sh_scr[r - 1, rows, :]
        term = win * w_ref[0, pl.ds(t, 1), :]
        y = term if y is None else y + term
    y = y + b_ref[0]
    o_ref[...] = _group_ln_silu(y, gl_ref, bl_ref).astype(o_ref.dtype)

    @pl.when(j == nj - 1)
    def _():
        hst_ref[0] = h[LB_C - HIST_C:, :]


def _conf_prompt(layer, glu, w_dw, b_dw, g_ln, b_ln):
    nj = L_P // LB_C
    return pl.pallas_call(
        _conf_p_kernel,
        grid=(NB_P, nj),
        in_specs=[
            pl.BlockSpec((LB_C, CW), lambda b, j: (b * nj + j, 0)),
            pl.BlockSpec((HIST_C, CW),
                         lambda b, j: (jnp.maximum((b * nj + j) * (LB_C // HIST_C) - 1, 0), 0)),
            pl.BlockSpec((1, CK, CW), lambda b, j: (layer, 0, 0)),
            pl.BlockSpec((1, 1, CW), lambda b, j: (layer, 0, 0)),
            pl.BlockSpec((1, 1, CW), lambda b, j: (layer, 0, 0)),
            pl.BlockSpec((1, 1, CW), lambda b, j: (layer, 0, 0)),
        ],
        out_specs=[
            pl.BlockSpec((LB_C, CW), lambda b, j: (b * nj + j, 0)),
            pl.BlockSpec((1, HIST_C, CW), lambda b, j: (b, 0, 0)),
        ],
        out_shape=[
            jax.ShapeDtypeStruct((T_P, CW), bf16),
            jax.ShapeDtypeStruct((NB_P, HIST_C, CW), f32),
        ],
        scratch_shapes=[pltpu.VMEM((HIST_C + LB_C, CW), f32),
                        pltpu.VMEM((SUBLANES - 1, HIST_C + LB_C - SUBLANES, CW), f32)],
        compiler_params=_cparams(("parallel", "arbitrary")),
        name=f"conf_prompt_{layer}",
    )(glu, glu, w_dw, b_dw, g_ln, b_ln)


BT_C = 16


def _conf_s_kernel(x_ref, hin_ref, wh_ref, wn_ref, b_ref, gl_ref, bl_ref, o_ref, hst_ref, y_scr):
    h = x_ref[...]
    trow = lax.broadcasted_iota(jnp.int32, (TPAD, 1), 0)
    nh = CK - 1
    for b in range(BT_C):
        hist = hin_ref[0, b]
        h8 = h[b * TPAD:(b + 1) * TPAD, :]
        y8 = jnp.zeros((TPAD, CW), f32)
        for t in range(L_S):
            yt = (jnp.sum(hist * wh_ref[0, t], axis=0, keepdims=True)
                  + jnp.sum(h8 * wn_ref[0, t], axis=0, keepdims=True))
            y8 = jnp.where(trow == t, yt, y8)
        y_scr[pl.ds(b * TPAD, TPAD), :] = y8
        hst_ref[b, pl.ds(0, nh - L_S), :] = hin_ref[0, b, pl.ds(L_S, nh - L_S), :]
        hst_ref[b, pl.ds(nh - L_S, L_S), :] = h8[:L_S, :]
    y = y_scr[...] + b_ref[0]
    o_ref[...] = _group_ln_silu(y, gl_ref, bl_ref).astype(o_ref.dtype)


def _conf_sample(layer, glu, hist_in, w_hist, w_new, b_dw, g_ln, b_ln):
    row0 = T_P // (BT_C * TPAD)
    return pl.pallas_call(
        _conf_s_kernel,
        grid=(NB_S // BT_C,),
        in_specs=[
            pl.BlockSpec((BT_C * TPAD, CW), lambda i: (row0 + i, 0)),
            pl.BlockSpec((1, BT_C, CK - 1, CW), lambda i: (layer, i, 0, 0)),
            pl.BlockSpec((1, L_S, CK - 1, CW), lambda i: (layer, 0, 0, 0)),
            pl.BlockSpec((1, L_S, TPAD, CW), lambda i: (layer, 0, 0, 0)),
            pl.BlockSpec((1, 1, CW), lambda i: (layer, 0, 0)),
            pl.BlockSpec((1, 1, CW), lambda i: (layer, 0, 0)),
            pl.BlockSpec((1, 1, CW), lambda i: (layer, 0, 0)),
        ],
        out_specs=[
            pl.BlockSpec((BT_C * TPAD, CW), lambda i: (i, 0)),
            pl.BlockSpec((BT_C, CK - 1, CW), lambda i: (i, 0, 0)),
        ],
        out_shape=[
            jax.ShapeDtypeStruct((T_S, CW), bf16),
            jax.ShapeDtypeStruct((NB_S, CK - 1, CW), f32),
        ],
        scratch_shapes=[pltpu.VMEM((BT_C * TPAD, CW), f32)],
        compiler_params=_cparams(("parallel",)),
        name=f"conf_sample_{layer}",
    )(glu, hist_in, w_hist, w_new, b_dw, g_ln, b_ln)


def _out_kernel(moe, x_ref, oap, obp, oas, obs, w_ref, g_ref,
                g1p, scp, shp, g1s, scs, shs, *rest):
    if moe:
        wr_ref, br_ref, xo_ref, h2_ref, rt_ref = rest
        hf_scr = h2_ref
    else:
        xo_ref, h2_ref, hf_scr = rest
    i = pl.program_id(0)

    def attn(oa, ob):
        return (jnp.dot(oa[...], w_ref[0, :GW, :], preferred_element_type=f32)
                + jnp.dot(ob[...], w_ref[0, GW:, :], preferred_element_type=f32))

    @pl.when(i < NT_P)
    def _():
        xn = x_ref[...] + g1p[0] * attn(oap, obp)
        xo_ref[...] = xn
        hf_scr[...] = _rms(xn, g_ref[0]) * (1.0 + scp[0]) + shp[0]

    @pl.when(i >= NT_P)
    def _():
        hf_scr[...] = attn(oas, obs)

        def body(rows, b):
            m = pl.ds(b, 1)
            xn = x_ref[rows, :] + g1s[m, :] * hf_scr[rows, :]
            xo_ref[rows, :] = xn
            hf_scr[rows, :] = _rms(xn, g_ref[0]) * (1.0 + scs[m, :]) + shs[m, :]
        _sample_rows(body)

    hf = hf_scr[...]
    if not moe:
        h2_ref[...] = hf.astype(h2_ref.dtype)
    if moe:
        logits = _dot_hi(hf, wr_ref[0]) + br_ref[0]
        lane = lax.broadcasted_iota(jnp.int32, logits.shape, 1)
        ex = jnp.exp(logits - jnp.max(logits, axis=-1, keepdims=True))
        probs = ex / jnp.sum(ex, axis=-1, keepdims=True)
        m1 = jnp.max(probs, axis=-1, keepdims=True)
        i1 = jnp.min(jnp.where(probs == m1, lane, LANES), axis=-1, keepdims=True)
        rest_p = jnp.where(lane == i1, -1.0, probs)
        m2 = jnp.max(rest_p, axis=-1, keepdims=True)
        i2 = jnp.min(jnp.where(rest_p == m2, lane, LANES), axis=-1, keepdims=True)
        den = m1 + m2
        rt_ref[...] = jnp.where(lane == 0, i1.astype(f32),
                                jnp.where(lane == 1, i2.astype(f32),
                                          jnp.where(lane == 2, m1 / den,
                                                    jnp.where(lane == 3, m2 / den, 0.0))))


def _out_proj(layer, moe, x, oa_p, ob_p, oa_s, ob_s, w_out, g2, modp, mods, wr=None, br=None):
    idx = layer // 2
    in_specs = [
        pl.BlockSpec((TM, D), lambda i: (i, 0)),
        pl.BlockSpec((TM, GW), lambda i: (jnp.minimum(i, NT_P - 1), 0)),
        pl.BlockSpec((TM, CW), lambda i: (jnp.minimum(i, NT_P - 1), 0)),
        pl.BlockSpec((TM, GW), lambda i: (0, 0)),
        pl.BlockSpec((TM, CW), lambda i: (0, 0)),
        pl.BlockSpec((1, D, D), lambda i: (layer, 0, 0)),
        pl.BlockSpec((1, 1, D), lambda i: (layer, 0, 0)),
        _mod_specs(G1, 1)[0], _mod_specs(SC2, 1)[0], _mod_specs(SH2, 1)[0],
        _mod_specs(G1, 1)[1], _mod_specs(SC2, 1)[1], _mod_specs(SH2, 1)[1],
    ]
    args = [x, oa_p, ob_p, oa_s, ob_s, w_out, g2, modp, modp, modp, mods, mods, mods]
    out_specs = [pl.BlockSpec((TM, D), lambda i: (i, 0)), pl.BlockSpec((TM, D), lambda i: (i, 0))]
    out_shape = [jax.ShapeDtypeStruct((T_ALL, D), f32),
                 jax.ShapeDtypeStruct((T_ALL, D), f32 if moe else bf16)]
    if moe:
        in_specs += [pl.BlockSpec((1, D, LANES), lambda i: (idx, 0, 0)),
                     pl.BlockSpec((1, 1, LANES), lambda i: (idx, 0, 0))]
        args += [wr, br]
        out_specs.append(pl.BlockSpec((TM, LANES), lambda i: (i, 0)))
        out_shape.append(jax.ShapeDtypeStruct((T_ALL, LANES), f32))
    return pl.pallas_call(
        functools.partial(_out_kernel, moe),
        grid=(NT,),
        in_specs=in_specs,
        out_specs=out_specs,
        out_shape=out_shape,
        scratch_shapes=[] if moe else [pltpu.VMEM((TM, D), f32)],
        compiler_params=_cparams(("parallel",)),
        name=f"out_proj_{layer}",
    )(*args)


def _residual(i, x_ref, f_ref, g2p, g2s, o_ref, gf_ref, os_ref=None):
    os_ref = o_ref if os_ref is None else os_ref

    def fin(v):
        return v if gf_ref is None else _rms(v, gf_ref[...])

    @pl.when(i < NT_P)
    def _():
        o_ref[...] = fin(x_ref[...] + g2p[0] * f_ref[...])

    @pl.when(i >= NT_P)
    def _():
        def body(rows, b):
            os_ref[rows, :] = fin(x_ref[rows, :] + g2s[pl.ds(b, 1), :] * f_ref[rows, :])
        _sample_rows(body)


def _ffn_kernel(h_ref, wg_ref, wu_ref, wd_ref, x_ref, g2p, g2s, o_ref, acc_scr):
    i = pl.program_id(0)
    h = h_ref[...]
    for c in range(FF_D // TF_D):
        cols = slice(c * TF_D, (c + 1) * TF_D)
        gate = jnp.dot(h, wg_ref[0, :, cols], preferred_element_type=f32)
        up = jnp.dot(h, wu_ref[0, :, cols], preferred_element_type=f32)
        part = jnp.dot((_silu(gate) * up).astype(bf16), wd_ref[0, cols, :], preferred_element_type=f32)
        if c == 0:
            acc_scr[...] = part
        else:
            acc_scr[...] += part
    _residual(i, x_ref, acc_scr, g2p, g2s, o_ref, None)


def _ffn_dense(layer, h2, x, wg, wu, wd, modp, mods):
    idx = layer // 2
    mp, ms = _mod_specs(G2, 1)
    return pl.pallas_call(
        _ffn_kernel,
        grid=(NT,),
        in_specs=[
            pl.BlockSpec((TM, D), lambda i: (i, 0)),
            _resident((1, D, FF_D), lambda i: (idx, 0, 0)),
            _resident((1, D, FF_D), lambda i: (idx, 0, 0)),
            _resident((1, FF_D, D), lambda i: (idx, 0, 0)),
            pl.BlockSpec((TM, D), lambda i: (i, 0)),
            mp, ms,
        ],
        out_specs=pl.BlockSpec((TM, D), lambda i: (i, 0)),
        out_shape=jax.ShapeDtypeStruct((T_ALL, D), f32),
        scratch_shapes=[pltpu.VMEM((TM, D), f32)],
        compiler_params=_cparams(("parallel",)),
        name=f"ffn_dense_{layer}",
    )(h2, wg, wu, wd, x, modp, mods)


def _gather_rows(idx_ref, base, stride, src_hbm, dst, sem):
    def issue(r, carry):
        row = idx_ref[base + stride * r]
        if len(src_hbm.shape) == 3:
            pltpu.make_async_copy(src_hbm.at[row], dst.at[r], sem).start()
        else:
            pltpu.make_async_copy(src_hbm.at[pl.ds(row, 1), :], dst.at[pl.ds(r, 1), :], sem).start()
        return carry
    lax.fori_loop(0, MOE_BLK, issue, 0, unroll=8)


def _wait_rows(src_hbm, dst, sem):
    pltpu.make_async_copy(src_hbm.at[pl.ds(0, MOE_BLK)], dst, sem).wait()


def _issue_rows(idx_ref, base, src_hbm, dst, sem, r0, n):
    for u in range(n):
        row = idx_ref[base + r0 + u]
        pltpu.make_async_copy(src_hbm.at[pl.ds(row, 1), :], dst.at[pl.ds(r0 + u, 1), :], sem).start()


def _expert_kernel(be_ref, tok_ref, nv_ref, nr_ref, h_hbm, wg_ref, wu_ref, wd_ref,
                   y_ref, xg_scr, xb_scr, acc_scr, sem):
    m = pl.program_id(0)
    f = pl.program_id(1)
    nv = nv_ref[0]
    active = m < nv
    slot = m % 2
    nxt = (m + 1) * MOE_BLK

    @pl.when(jnp.logical_and(f == 0, m == 0))
    def _():
        _gather_rows(tok_ref, 0, 1, h_hbm, xg_scr.at[0], sem.at[0])

    @pl.when(jnp.logical_and(f == 0, m <= nv))
    def _():
        _wait_rows(h_hbm, xg_scr.at[slot], sem.at[slot])

    @pl.when(jnp.logical_and(f == 0, active))
    def _():
        xb_scr[...] = xg_scr[slot].astype(bf16)
        _issue_rows(tok_ref, nxt, h_hbm, xg_scr.at[1 - slot], sem.at[1 - slot],
                    GATHER_CHUNK * (FF_E // TF_E), GATHER_TAIL)

    @pl.when(jnp.logical_and(f == 0, jnp.logical_not(active)))
    def _():
        y_ref[...] = jnp.zeros_like(y_ref)

    @pl.when(jnp.logical_and(f == 0, active))
    def _():
        acc_scr[...] = jnp.zeros_like(acc_scr)

    def ffn_step(rows):
        _issue_rows(tok_ref, nxt, h_hbm, xg_scr.at[1 - slot], sem.at[1 - slot], f * GATHER_CHUNK, GATHER_CHUNK)
        xb = xb_scr[pl.ds(0, rows), :]
        a = _silu(_dot(xb, wg_ref[0, 0])) * _dot(xb, wu_ref[0, 0])
        acc_scr[pl.ds(0, rows), :] += _dot(a, wd_ref[0, 0])

    short = nr_ref[m] <= MOE_BLK // 2

    @pl.when(jnp.logical_and(active, jnp.logical_not(short)))
    def _():
        ffn_step(MOE_BLK)

    @pl.when(jnp.logical_and(active, short))
    def _():
        ffn_step(MOE_BLK // 2)

    @pl.when(jnp.logical_and(f == pl.num_programs(1) - 1, active))
    def _():
        y_ref[...] = acc_scr[...].reshape(MOE_BLK, D // LANES, LANES)


def _experts(idx, h2, block_e, slot_tok, n_valid, n_rows, wg, wu, wd):
    nf = FF_E // TF_E

    def wmap_up(m, f, be, tok, nv, nr):
        return (idx, be[m], 0, jnp.where(m < nv[0], f, nf - 1))

    def wmap_down(m, f, be, tok, nv, nr):
        return (idx, be[m], jnp.where(m < nv[0], f, nf - 1), 0)

    grid_spec = pltpu.PrefetchScalarGridSpec(
        num_scalar_prefetch=4,
        grid=(N_BLK + 1, nf),
        in_specs=[
            pl.BlockSpec(memory_space=pl.ANY),
            pl.BlockSpec((1, 1, D, TF_E), wmap_up),
            pl.BlockSpec((1, 1, D, TF_E), wmap_up),
            pl.BlockSpec((1, 1, TF_E, D), wmap_down),
        ],
        out_specs=pl.BlockSpec((MOE_BLK, D // LANES, LANES), lambda m, f, be, tok, nv, nr: (m, 0, 0)),
        scratch_shapes=[pltpu.VMEM((2, MOE_BLK, D), f32), pltpu.VMEM((MOE_BLK, D), bf16),
                        pltpu.VMEM((MOE_BLK, D), f32), pltpu.SemaphoreType.DMA((2,))],
    )
    return pl.pallas_call(
        _expert_kernel,
        grid_spec=grid_spec,
        out_shape=jax.ShapeDtypeStruct(((N_BLK + 1) * MOE_BLK, D // LANES, LANES), f32),
        compiler_params=_cparams(("arbitrary", "arbitrary")),
        name=f"experts_{idx}",
    )(block_e, slot_tok, n_valid, n_rows, h2, wg, wu, wd)


def _combine_kernel(final, dest_ref, y_hbm, x_ref, rt_ref, g2p, g2s, *rest):
    if final:
        gf_ref, o_ref, os_ref, g_scr, f_scr, sem = rest
    else:
        o_ref, g_scr, f_scr, sem = rest
        gf_ref = os_ref = None
    i = pl.program_id(0)
    slot = i % 2

    def gather(tile, sl):
        for k in range(2):
            _gather_rows(dest_ref, 2 * tile * TM + k, 2, y_hbm, g_scr.at[sl, k], sem.at[sl, k])

    @pl.when(i == 0)
    def _():
        gather(0, 0)

    @pl.when(i + 1 < pl.num_programs(0))
    def _():
        gather(i + 1, 1 - slot)

    for k in range(2):
        _wait_rows(y_hbm, g_scr.at[slot, k], sem.at[slot, k])
    f_scr[...] = (rt_ref[:, 2:3] * g_scr[slot, 0].reshape(TM, D) + rt_ref[:, 3:4] * g_scr[slot, 1].reshape(TM, D))
    _residual(i, x_ref, f_scr, g2p, g2s, o_ref, gf_ref, os_ref)


def _combine(final, dest, yb, x, rt, modp, mods, g_final):
    mp, ms = _mod_specs(G2, 1)
    wrap = lambda spec: pl.BlockSpec(spec.block_shape, lambda i, d, _f=spec.index_map: _f(i))
    in_specs = [pl.BlockSpec(memory_space=pl.ANY),
                pl.BlockSpec((TM, D), lambda i, d: (i, 0)),
                pl.BlockSpec((TM, LANES), lambda i, d: (i, 0)), wrap(mp), wrap(ms)]
    args = [yb, x, rt, modp, mods]
    if final:
        in_specs.append(pl.BlockSpec((1, D), lambda i, d: (0, 0)))
        args.append(g_final)
        out_specs = [pl.BlockSpec((TM, D), lambda i, d: (jnp.minimum(i, NT_P - 1), 0)),
                     pl.BlockSpec((T_S, D), lambda i, d: (0, 0))]
        out_shape = [jax.ShapeDtypeStruct((T_P, D), f32), jax.ShapeDtypeStruct((T_S, D), f32)]
    else:
        out_specs = pl.BlockSpec((TM, D), lambda i, d: (i, 0))
        out_shape = jax.ShapeDtypeStruct((T_ALL, D), f32)
    grid_spec = pltpu.PrefetchScalarGridSpec(
        num_scalar_prefetch=1,
        grid=(NT,),
        in_specs=in_specs,
        out_specs=out_specs,
        scratch_shapes=[pltpu.VMEM((2, 2, TM, D // LANES, LANES), f32), pltpu.VMEM((TM, D), f32),
                        pltpu.SemaphoreType.DMA((2, 2))],
    )
    return pl.pallas_call(
        functools.partial(_combine_kernel, final),
        grid_spec=grid_spec,
        out_shape=out_shape,
        compiler_params=_cparams(("arbitrary",)),
        name="combine_final" if final else "combine",
    )(dest, *args)


def _route(rt):
    e = rt[:, :2].astype(jnp.int32).reshape(N_ASG)
    tok = jnp.arange(N_ASG, dtype=jnp.int32) // 2
    real = jnp.logical_or(tok < T_P, (tok - T_P) % TPAD < L_S)
    onehot = jnp.logical_and(e[:, None] == jnp.arange(NE, dtype=jnp.int32)[None, :], real[:, None]).astype(jnp.int32)
    csum = jnp.cumsum(onehot, axis=0)
    counts = csum[-1]
    rank = jnp.sum(onehot * (csum - 1), axis=1)
    padded = (counts + MOE_BLK - 1) // MOE_BLK * MOE_BLK
    pad_end = jnp.cumsum(padded)
    pad_start = pad_end - padded
    dest = (jnp.sum(onehot * pad_start[None, :], axis=1) + rank).astype(jnp.int32)
    dest = jnp.where(real, dest, jnp.arange(N_ASG, dtype=jnp.int32) % MOE_BLK)
    n_slots = (N_BLK + 1) * MOE_BLK
    slot_tok = jnp.zeros((n_slots,), jnp.int32).at[jnp.where(real, dest, n_slots)].set(
        tok, mode="drop", unique_indices=True)
    n_valid = (pad_end[-1] // MOE_BLK).astype(jnp.int32)
    blk = jnp.minimum(jnp.arange(N_BLK + 1, dtype=jnp.int32), n_valid - 1)
    first_slot = (blk * MOE_BLK)[:, None]
    block_e = jnp.minimum(jnp.sum((pad_end[None, :] <= first_slot).astype(jnp.int32), axis=1), NE - 1)
    n_rows = jnp.clip(counts[block_e] - (first_slot[:, 0] - pad_start[block_e]), 0, MOE_BLK).astype(jnp.int32)
    return dest, slot_tok, block_e, n_valid.reshape(1), n_rows


def kernel(x_prompt, x_sample, c_prompt, c_sample, state_gdn, state_gdn_conv, state_conf_conv, w_ada, b_ada, g_norm1, g_norm2, w_in, w_gdn_conv, a_log, dt_bias, g_gdn_out, w_conf_dw, b_conf_dw, g_conf_ln, b_conf_ln, w_out, w_ff_gate, w_ff_up, w_ff_down, w_router, b_router, w_exp_gate, w_exp_up, w_exp_down, g_final):
    xs_pad = jnp.pad(x_sample, ((0, 0), (0, TPAD - L_S), (0, 0)))
    x = jnp.concatenate([x_prompt.reshape(T_P, D), xs_pad.reshape(T_S, D)], axis=0)
    c_all = jnp.concatenate([c_prompt, c_sample], axis=0)
    o1, o2, o4 = QKV + GW, QKV + GW + 2 * NH, QKV + GW + 2 * NH + 2 * CW
    w_cat = jnp.concatenate([w_in[:, :, :o2], jnp.zeros((DEPTH, D, W_CAT - o4), f32), w_in[:, :, o2:o4]],
                            axis=-1).astype(bf16)
    lane_pad = ((0, 0), (NH, LANES - 2 * NH))
    gprm = jnp.stack([jnp.pad(a_log, lane_pad), jnp.pad(dt_bias, lane_pad)], axis=1)
    gprm = jnp.pad(gprm, ((0, 0), (0, SUBLANES - 2), (0, 0)))
    wr_pad = jnp.pad(w_router, ((0, 0), (0, 0), (0, LANES - NE)))
    br_pad = jnp.pad(b_router, ((0, 0), (0, LANES - NE)), constant_values=-1e30).reshape(-1, 1, LANES)
    w_shift = jnp.stack([jnp.pad(w_conf_dw, ((0, 0), (t, TPAD - 1 - t), (0, 0))) for t in range(L_S)], axis=1)
    w_hist, w_new = w_shift[:, :, :CK - 1], w_shift[:, :, CK - 1:]
    w_out = w_out.astype(bf16)
    w_ff_gate, w_ff_up, w_ff_down = (w.astype(bf16) for w in (w_ff_gate, w_ff_up, w_ff_down))
    g1 = g_norm1.reshape(DEPTH, 1, D)
    g2 = g_norm2.reshape(DEPTH, 1, D)
    gout = g_gdn_out.reshape(DEPTH, 1, DH)
    b_dw = b_conf_dw.reshape(DEPTH, 1, CW)
    g_ln = g_conf_ln.reshape(DEPTH, 1, CW)
    b_ln = b_conf_ln.reshape(DEPTH, 1, CW)

    mod = _ada(c_all, w_ada, b_ada)

    sp_l, cp_l, fp_l, ss_l, cs_l, fs_l = [], [], [], [], [], []
    for layer in range(DEPTH):
        modp = mod[layer, :NB_P].reshape(NB_P, 1, N_MOD * D)
        mods = mod[layer, NB_P:]
        proj, glu = _in_proj(layer, x, g1, modp, mods, w_cat)
        oa_p, s_p, c_p = _gdn_prompt(layer, proj, w_gdn_conv, gprm, gout)
        oa_s, s_s, c_s = _gdn_sample(layer, proj, state_gdn_conv, state_gdn, w_gdn_conv, gprm, gout)
        ss_l.append(s_s)
        ob_p, f_p = _conf_prompt(layer, glu, w_conf_dw, b_dw, g_ln, b_ln)
        ob_s, f_s = _conf_sample(layer, glu, state_conf_conv, w_hist, w_new, b_dw, g_ln, b_ln)
        sp_l.append(s_p)
        cp_l.append(c_p[:, TPAD - (GCONV - 1):, :])
        fp_l.append(f_p[:, HIST_C - (CK - 1):, :])
        cs_l.append(c_s)
        fs_l.append(f_s)
        if layer % 2 == 0:
            x, h2 = _out_proj(layer, False, x, oa_p, ob_p, oa_s, ob_s, w_out, g2, modp, mods)
            x = _ffn_dense(layer, h2, x, w_ff_gate, w_ff_up, w_ff_down, modp, mods)
        else:
            x, h2, rt = _out_proj(layer, True, x, oa_p, ob_p, oa_s, ob_s, w_out, g2, modp, mods, wr_pad, br_pad)
            dest, slot_tok, block_e, n_valid, n_rows = _route(rt)
            yb = _experts(layer // 2, h2, block_e, slot_tok, n_valid, n_rows, w_exp_gate, w_exp_up, w_exp_down)
            final = layer == DEPTH - 1
            x = _combine(final, dest, yb, x, rt, modp, mods, g_final.reshape(1, D))

    y_p, y_s = x
    y_prompt = y_p.reshape(NB_P, L_P, D)
    y_sample = y_s.reshape(NB_S, TPAD, D)[:, :L_S, :]
    return (y_prompt, y_sample, jnp.stack(sp_l), jnp.stack(cp_l), jnp.stack(fp_l),
            jnp.stack(ss_l), jnp.stack(cs_l), jnp.stack(fs_l))
```
